```python
import jax, jax.numpy as jnp
from jax import lax
import numpy as np

D_MODEL = 1024
BATCH = 4
SEQ = 4096
DEPTH = 2

GRID_W = 64
HEAD_DIM = 64
QBLK = 128
EPS = 1e-6
NEG_INF = -1e30
ROPE_THETA = 10000.0

A_HEADS = 4
A_WIN_R = 8
A_WIN_C = 16
A_COL_BLK = 16
A_KEY_COLS = 32
B_HEADS = 4
B_KV_HEADS = 2
B_WINDOW = 128
T5_BUCKETS = 32
T5_MAX_DIST = 128
C_HEADS = 4
C_KV_HEADS = 2
D_HEADS = 4
D_Q_LORA = 256
D_KV_LORA = 128
D_NOPE = 64
D_ROPE = 32
D_V = 64

N_BRANCH = 4
BRANCH_W = 256
D_FF = ((8 * D_MODEL + 3 * 256 - 1) // (3 * 256)) * 256

A_W = A_HEADS * HEAD_DIM
B_QW = B_HEADS * HEAD_DIM
B_KVW = B_KV_HEADS * HEAD_DIM
C_QW = C_HEADS * HEAD_DIM
C_KVW = C_KV_HEADS * HEAD_DIM
IN_SPLITS = (A_W, A_W, A_W, B_QW, B_KVW, B_KVW, C_QW, C_KVW, C_KVW,
             D_Q_LORA, D_KV_LORA, D_ROPE, N_BRANCH * D_MODEL)
IN_WIDTH = sum(IN_SPLITS)

kernel_name = 'hybrid_gated_encoder_block'


def rms_norm(x, g):
    xf = x.astype(jnp.float32)
    y = xf * lax.rsqrt(jnp.mean(xf * xf, axis=-1, keepdims=True) + EPS)
    return (y * g.astype(jnp.float32)).astype(x.dtype)


def rope_rotate(x, pos):
    d = x.shape[-1]
    half = d // 2
    inv_freq = 1.0 / (ROPE_THETA ** (jnp.arange(half, dtype=jnp.float32) * (2.0 / d)))
    ang = pos[:, None] * inv_freq[None, :]
    cos = jnp.cos(ang)[:, None, :]
    sin = jnp.sin(ang)[:, None, :]
    xf = x.astype(jnp.float32)
    x1, x2 = xf[..., :half], xf[..., half:]
    return jnp.concatenate([x1 * cos - x2 * sin, x2 * cos + x1 * sin], axis=-1).astype(x.dtype)


def axial_rope(x, row_pos, col_pos):
    h = x.shape[-1] // 2
    return jnp.concatenate([rope_rotate(x[..., :h], row_pos), rope_rotate(x[..., h:], col_pos)], axis=-1)


def t5_buckets(rel):
    nb = T5_BUCKETS // 2
    max_exact = nb // 2
    ret = (rel > 0).astype(np.int32) * nb
    n = np.abs(rel)
    large = max_exact + (np.log(np.maximum(n, 1) / max_exact) / np.log(T5_MAX_DIST / max_exact)
                         * (nb - max_exact)).astype(np.int32)
    large = np.minimum(large, nb - 1)
    return ret + np.where(n < max_exact, n, large)


def neighborhood_attention(q, k, v, rpb):
    b, s, h, d = q.shape
    rows = s // GRID_W
    kr = min(A_WIN_R, rows)
    ncb = GRID_W // A_COL_BLK
    nk = kr * A_KEY_COLS
    r = np.arange(rows)
    rs = np.clip(r - A_WIN_R // 2, 0, rows - kr)
    key_rows = rs[:, None] + np.arange(kr)[None, :]
    col0 = np.clip(np.arange(ncb) * A_COL_BLK - A_WIN_C // 2, 0, GRID_W - A_KEY_COLS)
    key_cols = col0[:, None] + np.arange(A_KEY_COLS)[None, :]
    idx = key_rows[:, None, :, None] * GRID_W + key_cols[None, :, None, :]
    flat = jnp.asarray(idx.reshape(-1))
    kg = jnp.take(k, flat, axis=1).reshape(b, rows, ncb, nk, h, d)
    vg = jnp.take(v, flat, axis=1).reshape(b, rows, ncb, nk, h, d)
    qb = q.reshape(b, rows, ncb, A_COL_BLK, h, d)
    q_cols = np.arange(ncb)[:, None] * A_COL_BLK + np.arange(A_COL_BLK)[None, :]
    cs = np.clip(q_cols - A_WIN_C // 2, 0, GRID_W - A_WIN_C)
    kc = key_cols[:, None, :]
    col_ok = (kc >= cs[:, :, None]) & (kc < cs[:, :, None] + A_WIN_C)
    mask = np.broadcast_to(col_ok[:, :, None, :], (ncb, A_COL_BLK, kr, A_KEY_COLS)).reshape(ncb, A_COL_BLK, nk)
    dr = key_rows - r[:, None] + (A_WIN_R - 1)
    dc = np.clip(kc - q_cols[:, :, None], -(A_WIN_C - 1), A_WIN_C - 1) + (A_WIN_C - 1)
    bias = rpb[:, dr[:, None, None, :, None], dc[None, :, :, None, :]]
    bias = bias.reshape(h, rows, ncb, A_COL_BLK, nk).transpose(1, 2, 0, 3, 4)
    sc = jnp.einsum('brjqhd,brjkhd->brjhqk', qb, kg, preferred_element_type=jnp.float32) * (d ** -0.5)
    sc = jnp.where(mask[None, None, :, None], sc + bias.astype(jnp.float32), NEG_INF)
    p = jax.nn.softmax(sc, axis=-1).astype(v.dtype)
    o = jnp.einsum('brjhqk,brjkhd->brjqhd', p, vg)
    return o.reshape(b, s, h * d)


def window_sink_attention(q, k, v, sink, t5_table):
    b, s, hkv, g, d = q.shape
    nb = s // QBLK
    kl = QBLK + 2 * B_WINDOW
    kp = jnp.pad(k, ((0, 0), (B_WINDOW, B_WINDOW), (0, 0), (0, 0)))
    vp = jnp.pad(v, ((0, 0), (B_WINDOW, B_WINDOW), (0, 0), (0, 0)))
    idx = np.arange(nb)[:, None] * QBLK + np.arange(kl)[None, :]
    flat = jnp.asarray(idx.reshape(-1))
    kb = jnp.take(kp, flat, axis=1).reshape(b, nb, kl, hkv, d)
    vb = jnp.take(vp, flat, axis=1).reshape(b, nb, kl, hkv, d)
    kpos = idx - B_WINDOW
    qpos = np.arange(nb)[:, None] * QBLK + np.arange(QBLK)[None, :]
    rel = kpos[:, None, :] - qpos[:, :, None]
    valid = (np.abs(rel) <= B_WINDOW) & (kpos[:, None, :] >= 0) & (kpos[:, None, :] < s)
    bucket = t5_buckets(rel[0])
    bias = t5_table[bucket].transpose(2, 0, 1).reshape(hkv, g, QBLK, kl).astype(jnp.float32)
    qb = q.reshape(b, nb, QBLK, hkv, g, d)
    sc = jnp.einsum('bnqhgd,bnkhd->bnhgqk', qb, kb, preferred_element_type=jnp.float32) * (d ** -0.5)
    sc = jnp.where(valid[None, :, None, None], sc + bias[None, None], NEG_INF)
    sk = sink.astype(jnp.float32).reshape(hkv, g)[None, None, :, :, None, None]
    m = jnp.maximum(jnp.max(sc, axis=-1, keepdims=True), sk)
    e = jnp.exp(sc - m)
    p = e / (jnp.sum(e, axis=-1, keepdims=True) + jnp.exp(sk - m))
    o = jnp.einsum('bnhgqk,bnkhd->bnqhgd', p.astype(v.dtype), vb)
    return o.reshape(b, s, hkv * g * d)


def dense_block_attention(q, k, v, scale):
    b, s, hkv, g, dq = q.shape
    dv = v.shape[-1]
    nb = s // QBLK
    qb = q.reshape(b, nb, QBLK, hkv, g, dq).transpose(1, 0, 2, 3, 4, 5)

    def one_block(qblk):
        sc = jnp.einsum('bqhgd,bkhd->bhgqk', qblk, k, preferred_element_type=jnp.float32) * scale
        p = jax.nn.softmax(sc, axis=-1).astype(v.dtype)
        return jnp.einsum('bhgqk,bkhd->bqhgd', p, v)

    o = lax.map(one_block, qb)
    return o.transpose(1, 0, 2, 3, 4, 5).reshape(b, s, hkv * g * dv)


def hybrid_mixer(xn, w_in, a_rpb, b_sink, t5_table, c_qn, c_kn, d_qn, d_kvn, d_wq, d_wkv, w_branch, w_out):
    b, s, _ = xn.shape
    proj = xn @ w_in
    offsets = np.cumsum(IN_SPLITS)[:-1].tolist()
    (aq, ak, av, bq, bk, bv, cq, ck, cv, dcq, dckv, dkr, gate_logits) = jnp.split(proj, offsets, axis=-1)
    t = jnp.arange(s)
    pos = t.astype(jnp.float32)
    row_pos = (t // GRID_W).astype(jnp.float32)
    col_pos = (t % GRID_W).astype(jnp.float32)

    o_a = neighborhood_attention(aq.reshape(b, s, A_HEADS, HEAD_DIM), ak.reshape(b, s, A_HEADS, HEAD_DIM),
                                 av.reshape(b, s, A_HEADS, HEAD_DIM), a_rpb)
    o_b = window_sink_attention(bq.reshape(b, s, B_KV_HEADS, B_HEADS // B_KV_HEADS, HEAD_DIM),
                                bk.reshape(b, s, B_KV_HEADS, HEAD_DIM), bv.reshape(b, s, B_KV_HEADS, HEAD_DIM),
                                b_sink, t5_table)
    qc = axial_rope(rms_norm(cq.reshape(b, s, C_HEADS, HEAD_DIM), c_qn), row_pos, col_pos)
    kc = axial_rope(rms_norm(ck.reshape(b, s, C_KV_HEADS, HEAD_DIM), c_kn), row_pos, col_pos)
    o_c = dense_block_attention(qc.reshape(b, s, C_KV_HEADS, C_HEADS // C_KV_HEADS, HEAD_DIM), kc,
                                cv.reshape(b, s, C_KV_HEADS, HEAD_DIM), HEAD_DIM ** -0.5)
    qd = (rms_norm(dcq, d_qn) @ d_wq).reshape(b, s, D_HEADS, D_NOPE + D_ROPE)
    qd = jnp.concatenate([qd[..., :D_NOPE], rope_rotate(qd[..., D_NOPE:], pos)], axis=-1)
    kvd = (rms_norm(dckv, d_kvn) @ d_wkv).reshape(b, s, D_HEADS, D_NOPE + D_V)
    k_rope = rope_rotate(dkr.reshape(b, s, 1, D_ROPE), pos)
    kd = jnp.concatenate([kvd[..., :D_NOPE], jnp.broadcast_to(k_rope, (b, s, D_HEADS, D_ROPE))], axis=-1)
    o_d = dense_block_attention(qd[:, :, :, None, :], kd, kvd[..., D_NOPE:], (D_NOPE + D_ROPE) ** -0.5)

    branches = jnp.stack([o_a, o_b, o_c, o_d], axis=2)
    y = jnp.einsum('bsnc,ncd->bsnd', branches, w_branch)
    gates = jax.nn.sigmoid(gate_logits.astype(jnp.float32)).astype(xn.dtype).reshape(b, s, N_BRANCH, D_MODEL)
    merged = jnp.sum(gates * y, axis=2)
    return merged @ w_out


def setup_inputs(seed: int = 0) -> dict:
    key = jax.random.key(seed)
    ks = jax.random.split(key, 20)
    f32 = jnp.float32

    def nrm(k, shape, scale):
        return jax.random.normal(k, shape, f32) * scale

    def gain(k, shape):
        return 1.0 + 0.05 * jax.random.normal(k, shape, f32)

    return {
        'x': nrm(ks[0], (BATCH, SEQ, D_MODEL), 1.0),
        'w_in': nrm(ks[1], (DEPTH, D_MODEL, IN_WIDTH), D_MODEL ** -0.5),
        'a_rpb': nrm(ks[2], (DEPTH, A_HEADS, 2 * A_WIN_R - 1, 2 * A_WIN_C - 1), 0.5),
        'b_sink': nrm(ks[3], (DEPTH, B_HEADS), 1.0),
        't5_bias': nrm(ks[4], (T5_BUCKETS, B_HEADS), 0.5),
        'c_q_norm': gain(ks[5], (DEPTH, HEAD_DIM)),
        'c_k_norm': gain(ks[6], (DEPTH, HEAD_DIM)),
        'd_q_norm': gain(ks[7], (DEPTH, D_Q_LORA)),
        'd_kv_norm': gain(ks[8], (DEPTH, D_KV_LORA)),
        'd_w_q_up': nrm(ks[9], (DEPTH, D_Q_LORA, D_HEADS * (D_NOPE + D_ROPE)), D_Q_LORA ** -0.5),
        'd_w_kv_up': nrm(ks[10], (DEPTH, D_KV_LORA, D_HEADS * (D_NOPE + D_V)), D_KV_LORA ** -0.5),
        'w_branch': nrm(ks[11], (DEPTH, N_BRANCH, BRANCH_W, D_MODEL), BRANCH_W ** -0.5),
        'w_out': nrm(ks[12], (DEPTH, D_MODEL, D_MODEL), D_MODEL ** -0.5),
        'ln_pre_mix': gain(ks[13], (DEPTH, D_MODEL)),
        'ln_post_mix': gain(ks[14], (DEPTH, D_MODEL)),
        'ln_pre_ffn': gain(ks[15], (DEPTH, D_MODEL)),
        'ln_post_ffn': gain(ks[16], (DEPTH, D_MODEL)),
        'ffn_w_gu': nrm(ks[17], (DEPTH, D_MODEL, 2 * D_FF), D_MODEL ** -0.5),
        'ffn_w_down': nrm(ks[18], (DEPTH, D_FF, D_MODEL), D_FF ** -0.5),
    }


def reference(x, w_in, a_rpb, b_sink, t5_bias, c_q_norm, c_k_norm, d_q_norm, d_kv_norm, d_w_q_up, d_w_kv_up,
              w_branch, w_out, ln_pre_mix, ln_post_mix, ln_pre_ffn, ln_post_ffn, ffn_w_gu, ffn_w_down):
    for l in range(DEPTH):
        h = rms_norm(x, ln_pre_mix[l])
        m = hybrid_mixer(h, w_in[l], a_rpb[l], b_sink[l], t5_bias, c_q_norm[l], c_k_norm[l], d_q_norm[l],
                         d_kv_norm[l], d_w_q_up[l], d_w_kv_up[l], w_branch[l], w_out[l])
        x = x + rms_norm(m, ln_post_mix[l])
        h = rms_norm(x, ln_pre_ffn[l])
        gu = h @ ffn_w_gu[l]
        f = (jax.nn.silu(gu[..., :D_FF]) * gu[..., D_FF:]) @ ffn_w_down[l]
        x = x + rms_norm(f, ln_post_ffn[l])
    return x
```

```python
import functools

import jax
import jax.numpy as jnp
import numpy as np
from jax import lax
from jax.experimental import pallas as pl
from jax.experimental.pallas import tpu as pltpu

D_MODEL = 1024
SEQ = 4096
GRID_W = 64
HEAD_DIM = 64
N_HEADS = 4
KV_GROUPED = 2
EPS = 1e-6
NEG_INF = -1e30
ROPE_THETA = 10000.0
A_WIN_R = 8
A_WIN_C = 16
B_WINDOW = 128
T5_BUCKETS = 32
T5_MAX_DIST = 128
D_Q_LORA = 256
D_KV_LORA = 128
D_NOPE = 64
D_ROPE = 32
D_V = 64
D_FF = 2816
BRANCH_W = 256

LANES = 128
QK_W = N_HEADS * LANES
KG_W = KV_GROUPED * LANES
ROPE_HALF = 16

A_QROWS = 4
A_KROWS = 12
A_TQ = A_QROWS * GRID_W
A_TK = A_KROWS * GRID_W
B_TQ = 256
B_TK = B_TQ + 2 * B_WINDOW
DENSE_TQ = 256
TOK_TILE = 512
FF_CHUNK = 256
VMEM_LIMIT = 56 * 1024 * 1024

BF16 = jnp.bfloat16
F32 = jnp.float32


def _resident(shape):
    nd = len(shape)
    return pl.BlockSpec(shape, lambda *_: (0,) * nd, pipeline_mode=pl.Buffered(1))


def _params(*sem):
    return pltpu.CompilerParams(dimension_semantics=sem, vmem_limit_bytes=VMEM_LIMIT)


def _rms(x):
    return x * lax.rsqrt(jnp.mean(x * x, axis=-1, keepdims=True) + EPS)


def _rope128(x, cos, sin_signed):
    lane = lax.broadcasted_iota(jnp.int32, x.shape, 1)
    first = (lane % (2 * ROPE_HALF)) < ROPE_HALF
    rot = jnp.where(first, pltpu.roll(x, LANES - ROPE_HALF, 1), pltpu.roll(x, ROPE_HALF, 1))
    return x * cos + rot * sin_signed


def _proj_kernel(x_ref, g_ref, wa_ref, wb_ref, wc_ref, wd_ref, gmat_ref, cgain_ref, ctab_ref, dtab_ref,
                 dqn_ref, dkvn_ref, dwq_ref, dwkv_ref,
                 aq_ref, ak_ref, av_ref, bq_ref, bk_ref, bv_ref, cq_ref, ck_ref, cv_ref,
                 dq_ref, dk_ref, dv_ref):
    h = (_rms(x_ref[...]) * g_ref[...]).astype(BF16)

    pa = jnp.dot(h, wa_ref[...], preferred_element_type=F32)
    aq_ref[...] = pa[:, :QK_W].astype(BF16)
    ak_ref[...] = pa[:, QK_W:2 * QK_W].astype(BF16)
    av_ref[...] = pa[:, 2 * QK_W:].astype(BF16)

    pb = jnp.dot(h, wb_ref[...], preferred_element_type=F32)
    bq_ref[...] = pb[:, :QK_W].astype(BF16)
    bk_ref[...] = pb[:, QK_W:QK_W + KG_W].astype(BF16)
    bv_ref[...] = pb[:, QK_W + KG_W:].astype(BF16)

    pc = jnp.dot(h, wc_ref[...], preferred_element_type=F32)
    cv_ref[...] = pc[:, QK_W + KG_W:].astype(BF16)
    ccos = ctab_ref[0]
    csin = ctab_ref[1]
    gmat = gmat_ref[...]
    for c in range((QK_W + KG_W) // (2 * LANES)):
        y = pc[:, 2 * LANES * c:2 * LANES * (c + 1)]
        sq = y * y
        hi = sq.astype(BF16)
        lo = (sq - hi.astype(F32)).astype(BF16)
        ms = jnp.dot(hi, gmat, preferred_element_type=F32) + jnp.dot(lo, gmat, preferred_element_type=F32)
        yn = y * lax.rsqrt(ms + EPS) * cgain_ref[:, 2 * LANES * c:2 * LANES * (c + 1)]
        for s in range(2):
            col = 2 * LANES * c + LANES * s
            out = _rope128(yn[:, LANES * s:LANES * (s + 1)], ccos, csin).astype(BF16)
            if col < QK_W:
                cq_ref[:, col:col + LANES] = out
            else:
                ck_ref[:, col - QK_W:col - QK_W + LANES] = out

    pd = jnp.dot(h, wd_ref[...], preferred_element_type=F32)
    dcos = dtab_ref[0]
    dsin = dtab_ref[1]
    cq = (_rms(pd[:, :D_Q_LORA]) * dqn_ref[...]).astype(BF16)
    qd = jnp.dot(cq, dwq_ref[...], preferred_element_type=F32)
    ckv = (_rms(pd[:, D_Q_LORA:D_Q_LORA + D_KV_LORA]) * dkvn_ref[...]).astype(BF16)
    kvd = jnp.dot(ckv, dwkv_ref[...], preferred_element_type=F32)
    kr = _rope128(pd[:, D_Q_LORA + D_KV_LORA:], dcos, dsin)
    scale = (D_NOPE + D_ROPE) ** -0.5
    for hd in range(N_HEADS):
        sl = slice(LANES * hd, LANES * (hd + 1))
        dq_ref[:, sl] = (_rope128(qd[:, sl], dcos, dsin) * scale).astype(BF16)
        dk_ref[:, sl] = (kvd[:, sl] + kr).astype(BF16)
    dv_ref[...] = kvd[:, QK_W:].astype(BF16)


def _proj(x2, g, wa, wb, wc, wd, gmat, cgain, ctab, dtab, dqn, dkvn, dwq, dwkv):
    t = x2.shape[0]
    tm = TOK_TILE
    seq_tiles = SEQ // tm

    def tok(w):
        return pl.BlockSpec((tm, w), lambda i: (i, 0))

    def tab():
        return pl.BlockSpec((2, tm, LANES), lambda i: (0, i % seq_tiles, 0))

    out_w = (QK_W, QK_W, BRANCH_W, QK_W, KG_W, BRANCH_W, QK_W, KG_W, BRANCH_W, QK_W, QK_W, BRANCH_W)
    return pl.pallas_call(
        _proj_kernel,
        grid=(t // tm,),
        in_specs=[tok(D_MODEL), _resident(g.shape), _resident(wa.shape), _resident(wb.shape),
                  _resident(wc.shape), _resident(wd.shape), _resident(gmat.shape), _resident(cgain.shape),
                  tab(), tab(), _resident(dqn.shape), _resident(dkvn.shape), _resident(dwq.shape),
                  _resident(dwkv.shape)],
        out_specs=[tok(w) for w in out_w],
        out_shape=[jax.ShapeDtypeStruct((t, w), BF16) for w in out_w],
        compiler_params=_params("parallel"),
        name="proj",
    )(x2, g, wa, wb, wc, wd, gmat, cgain, ctab, dtab, dqn, dkvn, dwq, dwkv)


def _head_lanes(rows):
    return lax.broadcasted_iota(jnp.int32, (rows, BRANCH_W), 1) // HEAD_DIM


def _scores(q, k):
    return lax.dot_general(q, k, (((1,), (1,)), ((), ())), preferred_element_type=F32)


def _nbr_kernel(q_ref, k_ref, v_ref, bias_ref, o_ref):
    g = pl.program_id(1)
    start = pl.multiple_of(jnp.clip(A_QROWS * g - A_WIN_R // 2, 0, GRID_W - A_KROWS) * GRID_W, A_TQ)
    v = v_ref[pl.ds(start, A_TK), :]
    lane_head = _head_lanes(A_TQ)
    o = jnp.zeros((A_TQ, BRANCH_W), F32)
    for h in range(N_HEADS):
        sl = slice(LANES * h, LANES * (h + 1))
        s = _scores(q_ref[:, sl], k_ref[pl.ds(start, A_TK), sl]) + bias_ref[0, h]
        e = jnp.exp(s - jnp.max(s, axis=-1, keepdims=True))
        r = 1.0 / jnp.sum(e, axis=-1, keepdims=True)
        pv = jnp.dot(e.astype(BF16), v, preferred_element_type=F32)
        o = jnp.where(lane_head == h, pv * r, o)
    o_ref[...] = o.astype(BF16)


def _nbr_attn(q, k, v, bias):
    t = q.shape[0]
    nb = t // SEQ
    ng = SEQ // A_TQ

    def btype(b, g):
        return ((g > 0).astype(jnp.int32) + (g == ng - 1).astype(jnp.int32), 0, 0, 0)

    return pl.pallas_call(
        _nbr_kernel,
        grid=(nb, ng),
        in_specs=[pl.BlockSpec((A_TQ, QK_W), lambda b, g: (b * ng + g, 0)),
                  pl.BlockSpec((SEQ, QK_W), lambda b, g: (b, 0)),
                  pl.BlockSpec((SEQ, BRANCH_W), lambda b, g: (b, 0)),
                  pl.BlockSpec((1, N_HEADS, A_TQ, A_TK), btype)],
        out_specs=pl.BlockSpec((A_TQ, BRANCH_W), lambda b, g: (b * ng + g, 0)),
        out_shape=jax.ShapeDtypeStruct((t, BRANCH_W), BF16),
        compiler_params=_params("parallel", "arbitrary"),
        name="nbr_attn",
    )(q, k, v, bias)


def _win_kernel(sink_ref, q_ref, k_ref, v_ref, bias_ref, o_ref):
    n = pl.program_id(1)
    start = pl.multiple_of(jnp.clip(B_TQ * n - B_WINDOW, 0, SEQ - B_TK), B_WINDOW)
    v = v_ref[pl.ds(start, B_TK), :]
    lane_head = _head_lanes(B_TQ)
    o = jnp.zeros((B_TQ, BRANCH_W), F32)
    for h in range(N_HEADS):
        kv = h // (N_HEADS // KV_GROUPED)
        s = _scores(q_ref[:, LANES * h:LANES * (h + 1)],
                    k_ref[pl.ds(start, B_TK), LANES * kv:LANES * (kv + 1)]) + bias_ref[0, h]
        sink = sink_ref[h]
        m = jnp.maximum(jnp.max(s, axis=-1, keepdims=True), sink)
        e = jnp.exp(s - m)
        r = 1.0 / (jnp.sum(e, axis=-1, keepdims=True) + jnp.exp(sink - m))
        pv = jnp.dot(e.astype(BF16), v, preferred_element_type=F32)
        o = jnp.where(lane_head == h, pv * r, o)
    o_ref[...] = o.astype(BF16)


def _win_attn(sink, q, k, v, bias):
    t = q.shape[0]
    nb = t // SEQ
    nq = SEQ // B_TQ

    def btype(b, n):
        return ((n > 0).astype(jnp.int32) + (n == nq - 1).astype(jnp.int32), 0, 0, 0)

    return pl.pallas_call(
        _win_kernel,
        grid=(nb, nq),
        in_specs=[pl.BlockSpec(memory_space=pltpu.SMEM),
                  pl.BlockSpec((B_TQ, QK_W), lambda b, n: (b * nq + n, 0)),
                  pl.BlockSpec((SEQ, KG_W), lambda b, n: (b, 0)),
                  pl.BlockSpec((SEQ, BRANCH_W), lambda b, n: (b, 0)),
                  pl.BlockSpec((1, N_HEADS, B_TQ, B_TK), btype)],
        out_specs=pl.BlockSpec((B_TQ, BRANCH_W), lambda b, n: (b * nq + n, 0)),
        out_shape=jax.ShapeDtypeStruct((t, BRANCH_W), BF16),
        compiler_params=_params("parallel", "arbitrary"),
        name="win_attn",
    )(sink, q, k, v, bias)


def _dense_kernel(q_ref, k_ref, v_ref, o_ref, *, kv_heads):
    v = v_ref[...]
    lane_head = _head_lanes(DENSE_TQ)
    o = jnp.zeros((DENSE_TQ, BRANCH_W), F32)
    for h in range(N_HEADS):
        kv = h // (N_HEADS // kv_heads)
        s = _scores(q_ref[:, LANES * h:LANES * (h + 1)], k_ref[:, LANES * kv:LANES * (kv + 1)])
        e = jnp.exp(s - jnp.max(s, axis=-1, keepdims=True))
        r = 1.0 / jnp.sum(e, axis=-1, keepdims=True)
        pv = jnp.dot(e.astype(BF16), v, preferred_element_type=F32)
        o = jnp.where(lane_head == h, pv * r, o)
    o_ref[...] = o.astype(BF16)


def _dense_attn(q, k, v, kv_heads, name):
    t = q.shape[0]
    nb = t // SEQ
    nq = SEQ // DENSE_TQ
    return pl.pallas_call(
        functools.partial(_dense_kernel, kv_heads=kv_heads),
        grid=(nb, nq),
        in_specs=[pl.BlockSpec((DENSE_TQ, QK_W), lambda b, n: (b * nq + n, 0)),
                  pl.BlockSpec((SEQ, kv_heads * LANES), lambda b, n: (b, 0)),
                  pl.BlockSpec((SEQ, BRANCH_W), lambda b, n: (b, 0))],
        out_specs=pl.BlockSpec((DENSE_TQ, BRANCH_W), lambda b, n: (b * nq + n, 0)),
        out_shape=jax.ShapeDtypeStruct((t, BRANCH_W), BF16),
        compiler_params=_params("parallel", "arbitrary"),
        name=name,
    )(q, k, v)


def _merge_kernel(x_ref, gpre_ref, gpost_ref, oa_ref, ob_ref, oc_ref, od_ref, wg_ref, wbr_ref, wo_ref, y_ref):
    x = x_ref[...]
    h = (_rms(x) * gpre_ref[...]).astype(BF16)
    merged = None
    for n, o_ref in enumerate((oa_ref, ob_ref, oc_ref, od_ref)):
        logits = jnp.dot(h, wg_ref[:, D_MODEL * n:D_MODEL * (n + 1)], preferred_element_type=F32)
        y = jnp.dot(o_ref[...], wbr_ref[n], preferred_element_type=F32)
        term = jax.nn.sigmoid(logits) * y
        merged = term if merged is None else merged + term
    m = jnp.dot(merged.astype(BF16), wo_ref[...], preferred_element_type=F32)
    y_ref[...] = x + _rms(m) * gpost_ref[...]


def _merge(x2, gpre, gpost, oa, ob, oc, od, wg, wbr, wo):
    t = x2.shape[0]
    tm = TOK_TILE

    def tok(w):
        return pl.BlockSpec((tm, w), lambda i: (i, 0))

    return pl.pallas_call(
        _merge_kernel,
        grid=(t // tm,),
        in_specs=[tok(D_MODEL), _resident(gpre.shape), _resident(gpost.shape), tok(BRANCH_W), tok(BRANCH_W),
                  tok(BRANCH_W), tok(BRANCH_W), _resident(wg.shape), _resident(wbr.shape), _resident(wo.shape)],
        out_specs=tok(D_MODEL),
        out_shape=jax.ShapeDtypeStruct((t, D_MODEL), F32),
        compiler_params=_params("parallel"),
        name="merge",
    )(x2, gpre, gpost, oa, ob, oc, od, wg, wbr, wo)


def _ffn_kernel(x_ref, gpre_ref, gpost_ref, wgu_ref, wdn_ref, y_ref):
    x = x_ref[...]
    h = (_rms(x) * gpre_ref[...]).astype(BF16)
    acc = None
    for c in range(D_FF // FF_CHUNK):
        lo = FF_CHUNK * c
        gate = jnp.dot(h, wgu_ref[:, lo:lo + FF_CHUNK], preferred_element_type=F32)
        up = jnp.dot(h, wgu_ref[:, D_FF + lo:D_FF + lo + FF_CHUNK], preferred_element_type=F32)
        act = (jax.nn.silu(gate) * up).astype(BF16)
        part = jnp.dot(act, wdn_ref[lo:lo + FF_CHUNK, :], preferred_element_type=F32)
        acc = part if acc is None else acc + part
    y_ref[...] = x + _rms(acc) * gpost_ref[...]


def _ffn(x2, gpre, gpost, wgu, wdn):
    t = x2.shape[0]
    tm = TOK_TILE
    tok = pl.BlockSpec((tm, D_MODEL), lambda i: (i, 0))
    return pl.pallas_call(
        _ffn_kernel,
        grid=(t // tm,),
        in_specs=[tok, _resident(gpre.shape), _resident(gpost.shape), _resident(wgu.shape), _resident(wdn.shape)],
        out_specs=tok,
        out_shape=jax.ShapeDtypeStruct((t, D_MODEL), F32),
        compiler_params=_params("parallel"),
        name="ffn",
    )(x2, gpre, gpost, wgu, wdn)


def _pad_heads(w, heads, dim):
    k = w.shape[0]
    w = w.reshape(k, heads, dim)
    return jnp.pad(w, ((0, 0), (0, 0), (0, LANES - dim))).reshape(k, heads * LANES)


def _dup_kv(w, kv_heads):
    k = w.shape[0]
    w = w.reshape(k, kv_heads, 1, HEAD_DIM)
    return jnp.broadcast_to(w, (k, kv_heads, N_HEADS // kv_heads, HEAD_DIM)).reshape(k, N_HEADS * HEAD_DIM)


def _rope_tables():
    t = jnp.arange(SEQ)
    inv_freq = 1.0 / (ROPE_THETA ** (jnp.arange(ROPE_HALF, dtype=F32) * (1.0 / ROPE_HALF)))

    def group(pos):
        ang = pos.astype(F32)[:, None] * inv_freq[None, :]
        c, s = jnp.cos(ang), jnp.sin(ang)
        return jnp.concatenate([c, c], axis=-1), jnp.concatenate([-s, s], axis=-1)

    rc, rs = group(t // GRID_W)
    cc, cs = group(t % GRID_W)
    pc, ps = group(t)
    ones = jnp.ones((SEQ, 2 * ROPE_HALF), F32)
    zeros = jnp.zeros((SEQ, 2 * ROPE_HALF), F32)
    ctab = jnp.stack([jnp.concatenate([rc, cc, ones, ones], axis=-1),
                      jnp.concatenate([rs, cs, zeros, zeros], axis=-1)])
    dtab = jnp.stack([jnp.concatenate([ones, ones, pc, ones], axis=-1),
                      jnp.concatenate([zeros, zeros, ps, zeros], axis=-1)])
    return ctab, dtab


def _group_mean_matrix():
    i = np.arange(2 * LANES)
    m = ((i[:, None] // LANES) == (i[None, :] // LANES)) & ((i[:, None] % LANES) < HEAD_DIM)
    return jnp.asarray(m.astype(np.float32) / HEAD_DIM, dtype=BF16)


def _t5_bucket_index(rel):
    nb = T5_BUCKETS // 2
    max_exact = nb // 2
    ret = (rel > 0).astype(np.int32) * nb
    n = np.abs(rel)
    large = max_exact + (np.log(np.maximum(n, 1) / max_exact) / np.log(T5_MAX_DIST / max_exact)
                         * (nb - max_exact)).astype(np.int32)
    large = np.minimum(large, nb - 1)
    return ret + np.where(n < max_exact, n, large)


def _nbr_bias(rpb):
    rows = SEQ // GRID_W
    ng = rows // A_QROWS
    dr_all, dc_all, ok_all = [], [], []
    for g in (0, 1, ng - 1):
        ws = int(np.clip(A_QROWS * g - A_WIN_R // 2, 0, rows - A_KROWS))
        qr = A_QROWS * g + np.arange(A_TQ) // GRID_W
        qc = np.arange(A_TQ) % GRID_W
        kr = ws + np.arange(A_TK) // GRID_W
        kc = np.arange(A_TK) % GRID_W
        rs = np.clip(qr - A_WIN_R // 2, 0, rows - A_WIN_R)
        cs = np.clip(qc - A_WIN_C // 2, 0, GRID_W - A_WIN_C)
        ok = ((kr[None, :] >= rs[:, None]) & (kr[None, :] < rs[:, None] + A_WIN_R)
              & (kc[None, :] >= cs[:, None]) & (kc[None, :] < cs[:, None] + A_WIN_C))
        dr = np.clip(kr[None, :] - qr[:, None] + (A_WIN_R - 1), 0, 2 * A_WIN_R - 2)
        dc = np.clip(kc[None, :] - qc[:, None] + (A_WIN_C - 1), 0, 2 * A_WIN_C - 2)
        dr_all.append(dr), dc_all.append(dc), ok_all.append(ok)
    dr, dc, ok = np.stack(dr_all), np.stack(dc_all), np.stack(ok_all)
    bias = rpb.astype(F32)[:, dr, dc]
    return jnp.where(ok[None], bias, NEG_INF).transpose(1, 0, 2, 3)


def _win_bias(t5_table):
    nq = SEQ // B_TQ
    bk_all, ok_all = [], []
    for n in (0, 1, nq - 1):
        start = int(np.clip(B_TQ * n - B_WINDOW, 0, SEQ - B_TK))
        rel = (start + np.arange(B_TK))[None, :] - (B_TQ * n + np.arange(B_TQ))[:, None]
        bk_all.append(_t5_bucket_index(rel)), ok_all.append(np.abs(rel) <= B_WINDOW)
    bucket, ok = np.stack(bk_all), np.stack(ok_all)
    bias = t5_table.astype(F32)[bucket]
    return jnp.where(ok[..., None], bias, NEG_INF).transpose(0, 3, 1, 2)


def _layer_weights(w_in, c_q_norm, c_k_norm, d_w_q_up, d_w_kv_up):
    off = np.cumsum((0, 256, 256, 256, 256, 128, 128, 256, 128, 128, D_Q_LORA, D_KV_LORA, D_ROPE))
    col = lambda i: w_in[:, off[i]:off[i + 1]]
    qs = HEAD_DIM ** -0.5
    wa = jnp.concatenate([_pad_heads(col(0) * qs, N_HEADS, HEAD_DIM), _pad_heads(col(1), N_HEADS, HEAD_DIM),
                          col(2)], axis=1)
    wb = jnp.concatenate([_pad_heads(col(3) * qs, N_HEADS, HEAD_DIM), _pad_heads(col(4), KV_GROUPED, HEAD_DIM),
                          _dup_kv(col(5), KV_GROUPED)], axis=1)
    wc = jnp.concatenate([_pad_heads(col(6), N_HEADS, HEAD_DIM), _pad_heads(col(7), KV_GROUPED, HEAD_DIM),
                          _dup_kv(col(8), KV_GROUPED)], axis=1)
    w_kr = jnp.pad(col(11), ((0, 0), (D_NOPE, LANES - D_NOPE - D_ROPE)))
    wd = jnp.concatenate([col(9), col(10), w_kr], axis=1)
    cgain = jnp.concatenate([_pad_heads(jnp.tile(c_q_norm * qs, N_HEADS)[None], N_HEADS, HEAD_DIM),
                             _pad_heads(jnp.tile(c_k_norm, KV_GROUPED)[None], KV_GROUPED, HEAD_DIM)], axis=1)
    dwq = _pad_heads(d_w_q_up, N_HEADS, D_NOPE + D_ROPE)
    kvu = d_w_kv_up.reshape(D_KV_LORA, N_HEADS, D_NOPE + D_V)
    dwkv = jnp.concatenate([_pad_heads(kvu[:, :, :D_NOPE].reshape(D_KV_LORA, -1), N_HEADS, D_NOPE),
                            kvu[:, :, D_NOPE:].reshape(D_KV_LORA, -1)], axis=1)
    bf = lambda w: w.astype(BF16)
    return bf(wa), bf(wb), bf(wc), bf(wd), cgain.astype(F32), bf(dwq), bf(dwkv), bf(w_in[:, off[-1]:])


def kernel(x, w_in, a_rpb, b_sink, t5_bias, c_q_norm, c_k_norm, d_q_norm, d_kv_norm, d_w_q_up, d_w_kv_up,
           w_branch, w_out, ln_pre_mix, ln_post_mix, ln_pre_ffn, ln_post_ffn, ffn_w_gu, ffn_w_down):
    b, s, d = x.shape
    assert (s, d) == (SEQ, D_MODEL)
    depth = w_in.shape[0]
    x2 = x.reshape(b * s, d)
    ctab, dtab = _rope_tables()
    gmat = _group_mean_matrix()
    wbias = _win_bias(t5_bias)
    row = lambda v: v.astype(F32)[None, :]
    for l in range(depth):
        wa, wb, wc, wd, cgain, dwq, dwkv, wg = _layer_weights(w_in[l], c_q_norm[l], c_k_norm[l],
                                                              d_w_q_up[l], d_w_kv_up[l])
        (aq, ak, av, bq, bk, bv, cq, ck, cv, dq, dk, dv) = _proj(
            x2, row(ln_pre_mix[l]), wa, wb, wc, wd, gmat, cgain, ctab, dtab,
            row(d_q_norm[l]), row(d_kv_norm[l]), dwq, dwkv)
        oa = _nbr_attn(aq, ak, av, _nbr_bias(a_rpb[l]))
        ob = _win_attn(b_sink[l].astype(F32), bq, bk, bv, wbias)
        oc = _dense_attn(cq, ck, cv, KV_GROUPED, "dense_c")
        od = _dense_attn(dq, dk, dv, N_HEADS, "dense_d")
        x2 = _merge(x2, row(ln_pre_mix[l]), row(ln_post_mix[l]), oa, ob, oc, od, wg,
                    w_branch[l].astype(BF16), w_out[l].astype(BF16))
        x2 = _ffn(x2, row(ln_pre_ffn[l]), row(ln_post_ffn[l]), ffn_w_gu[l].astype(BF16),
                  ffn_w_down[l].astype(BF16))
    return x2.reshape(b, s, d)
```

```python
import functools

import jax
import jax.numpy as jnp
import numpy as np
from jax import lax
from jax.experimental import pallas as pl
from jax.experimental.pallas import tpu as pltpu

D_MODEL = 1024
SEQ = 4096
GRID_W = 64
HEAD_DIM = 64
N_HEADS = 4
KV_GROUPED = 2
EPS = 1e-6
NEG_INF = -1e30
ROPE_THETA = 10000.0
A_WIN_R = 8
A_WIN_C = 16
B_WINDOW = 128
T5_BUCKETS = 32
T5_MAX_DIST = 128
D_Q_LORA = 256
D_KV_LORA = 128
D_NOPE = 64
D_ROPE = 32
D_V = 64
D_FF = 2816
BRANCH_W = 256

LANES = 128
QK_W = N_HEADS * LANES
KG_W = KV_GROUPED * LANES
ROPE_HALF = 16
LOG2E = 1.4426950408889634

A_QROWS = 4
A_KROWS = 12
A_TQ = A_QROWS * GRID_W
A_TK = A_KROWS * GRID_W
B_TQ = 256
B_TK = B_TQ + 2 * B_WINDOW
DENSE_TQ = 256
TOK_TILE = 512
FF_CHUNK = 256
VMEM_LIMIT = 56 * 1024 * 1024

BF16 = jnp.bfloat16
F32 = jnp.float32


def _resident(shape):
    nd = len(shape)
    return pl.BlockSpec(shape, lambda *_: (0,) * nd, pipeline_mode=pl.Buffered(1))


def _params(*sem):
    return pltpu.CompilerParams(dimension_semantics=sem, vmem_limit_bytes=VMEM_LIMIT)


def _rms(x):
    return x * lax.rsqrt(jnp.mean(x * x, axis=-1, keepdims=True) + EPS)


def _rope128(x, cos, sin_signed):
    lane = lax.broadcasted_iota(jnp.int32, x.shape, 1)
    first = (lane % (2 * ROPE_HALF)) < ROPE_HALF
    rot = jnp.where(first, pltpu.roll(x, LANES - ROPE_HALF, 1), pltpu.roll(x, ROPE_HALF, 1))
    return x * cos + rot * sin_signed


def _with_ones(v):
    lane = lax.broadcasted_iota(jnp.int32, v.shape, 1)
    return jnp.where(lane % LANES == HEAD_DIM, 1.0, v)


def _spread_heads(o_ref, x, ones=False):
    lane = lax.broadcasted_iota(jnp.int32, (x.shape[0], LANES), 1)
    low = lane < HEAD_DIM
    fill = jnp.where(lane == HEAD_DIM, 1.0, 0.0) if ones else 0.0
    for c in range(x.shape[1] // LANES):
        pair = x[:, LANES * c:LANES * (c + 1)]
        o_ref[:, 2 * LANES * c:2 * LANES * c + LANES] = jnp.where(low, pair, fill).astype(BF16)
        o_ref[:, 2 * LANES * c + LANES:2 * LANES * (c + 1)] = (
            jnp.where(low, pltpu.roll(pair, HEAD_DIM, 1), fill).astype(BF16))


def _proj_kernel(x_ref, g_ref, w_ref, gmat_ref, cgain_ref, ctab_ref, dtab_ref,
                 dqn_ref, dkvn_ref, dwq_ref, dwkv_ref,
                 aq_ref, ak_ref, av_ref, bq_ref, bk_ref, bv_ref, cq_ref, ck_ref, cv_ref,
                 dq_ref, dk_ref, dv_ref):
    h = (_rms(x_ref[...]) * g_ref[...]).astype(BF16)
    qs = HEAD_DIM ** -0.5 * LOG2E
    a0, b0, c0, d0 = 0, 3 * BRANCH_W, 5 * BRANCH_W, 7 * BRANCH_W

    pa = jnp.dot(h, w_ref[:, a0:b0], preferred_element_type=F32)
    _spread_heads(aq_ref, pa[:, :BRANCH_W] * qs)
    _spread_heads(ak_ref, pa[:, BRANCH_W:2 * BRANCH_W])
    _spread_heads(av_ref, pa[:, 2 * BRANCH_W:], ones=True)

    pb = jnp.dot(h, w_ref[:, b0:c0], preferred_element_type=F32)
    _spread_heads(bq_ref, pb[:, :BRANCH_W] * qs)
    _spread_heads(bk_ref, pb[:, BRANCH_W:BRANCH_W + LANES])
    _spread_heads(bv_ref, pb[:, BRANCH_W + LANES:], ones=True)

    pc = jnp.dot(h, w_ref[:, c0:d0], preferred_element_type=F32)
    _spread_heads(cv_ref, pc[:, BRANCH_W + LANES:], ones=True)
    ccos = ctab_ref[0]
    csin = ctab_ref[1]

    def norm_rope(y, gmat, gain):
        sq = y * y
        hi = sq.astype(BF16)
        lo = (sq - hi.astype(F32)).astype(BF16)
        ms = jnp.dot(hi, gmat, preferred_element_type=F32) + jnp.dot(lo, gmat, preferred_element_type=F32)
        yn = y * lax.rsqrt(ms + EPS) * gain
        return jnp.concatenate([_rope128(yn[:, LANES * c:LANES * (c + 1)], ccos, csin)
                                for c in range(y.shape[1] // LANES)], axis=1)

    _spread_heads(cq_ref, norm_rope(pc[:, :BRANCH_W], gmat_ref[...], cgain_ref[:, :BRANCH_W]))
    _spread_heads(ck_ref, norm_rope(pc[:, BRANCH_W:BRANCH_W + LANES], gmat_ref[:LANES, :LANES],
                                    cgain_ref[:, BRANCH_W:]))

    pd = jnp.dot(h, w_ref[:, d0:], preferred_element_type=F32)
    dcos = dtab_ref[0]
    dsin = dtab_ref[1]
    cq = (_rms(pd[:, :D_Q_LORA]) * dqn_ref[...]).astype(BF16)
    qd = jnp.dot(cq, dwq_ref[...], preferred_element_type=F32)
    ckv = (_rms(pd[:, D_Q_LORA:D_Q_LORA + D_KV_LORA]) * dkvn_ref[...]).astype(BF16)
    kvd = jnp.dot(ckv, dwkv_ref[...], preferred_element_type=F32)
    kr = _rope128(pd[:, D_Q_LORA + D_KV_LORA:], dcos, dsin)
    scale = (D_NOPE + D_ROPE) ** -0.5 * LOG2E
    for hd in range(N_HEADS):
        sl = slice(LANES * hd, LANES * (hd + 1))
        dq_ref[:, sl] = (_rope128(qd[:, sl], dcos, dsin) * scale).astype(BF16)
        dk_ref[:, sl] = (kvd[:, sl] + kr).astype(BF16)
    dv_ref[...] = _with_ones(kvd[:, QK_W:]).astype(BF16)


def _proj(x2, g, w, gmat, cgain, ctab, dtab, dqn, dkvn, dwq, dwkv):
    t = x2.shape[0]
    tm = TOK_TILE
    seq_tiles = SEQ // tm

    def tok(w):
        return pl.BlockSpec((tm, w), lambda i: (i, 0))

    def tab():
        return pl.BlockSpec((2, tm, LANES), lambda i: (0, i % seq_tiles, 0))

    out_w = (QK_W, QK_W, QK_W, QK_W, KG_W, KG_W, QK_W, KG_W, KG_W, QK_W, QK_W, QK_W)
    return pl.pallas_call(
        _proj_kernel,
        grid=(t // tm,),
        in_specs=[tok(D_MODEL), _resident(g.shape), _resident(w.shape), _resident(gmat.shape),
                  _resident(cgain.shape), tab(), tab(), _resident(dqn.shape), _resident(dkvn.shape),
                  _resident(dwq.shape), _resident(dwkv.shape)],
        out_specs=[tok(ow) for ow in out_w],
        out_shape=[jax.ShapeDtypeStruct((t, ow), BF16) for ow in out_w],
        compiler_params=_params("parallel"),
        name="proj",
    )(x2, g, w, gmat, cgain, ctab, dtab, dqn, dkvn, dwq, dwkv)


def _attend(q, k, v_ref, rows, kv_head, pv_heads, bias=None, sink=None):
    slab, blk = divmod(kv_head, pv_heads)
    v = v_ref[rows, pv_heads * LANES * slab:pv_heads * LANES * (slab + 1)]
    s = lax.dot_general(q, k, (((1,), (1,)), ((), ())), preferred_element_type=F32)
    if bias is not None:
        s = s + bias
    m = jnp.max(s, axis=-1, keepdims=True)
    if sink is not None:
        m = jnp.maximum(m, sink)
    pv = jnp.dot(jnp.exp2(s - m).astype(BF16), v, preferred_element_type=F32)[:, LANES * blk:LANES * (blk + 1)]
    total = pv[:, HEAD_DIM:HEAD_DIM + 1]
    if sink is not None:
        total = total + jnp.exp2(sink - m)
    return pv * (1.0 / total)


def _store_heads(o_ref, outs):
    low = lax.broadcasted_iota(jnp.int32, outs[0].shape, 1) < HEAD_DIM
    for g in range(N_HEADS // 2):
        pair = jnp.where(low, outs[2 * g], pltpu.roll(outs[2 * g + 1], HEAD_DIM, 1))
        o_ref[:, LANES * g:LANES * (g + 1)] = pair.astype(BF16)


def _nbr_kernel(q_ref, k_ref, v_ref, bias_ref, o_ref):
    g = pl.program_id(1)
    start = pl.multiple_of(jnp.clip(A_QROWS * g - A_WIN_R // 2, 0, GRID_W - A_KROWS) * GRID_W, A_TQ)
    rows = pl.ds(start, A_TK)
    outs = []
    for h in range(N_HEADS):
        sl = slice(LANES * h, LANES * (h + 1))
        outs.append(_attend(q_ref[:, sl], k_ref[rows, sl], v_ref, rows, h, 2, bias=bias_ref[0, h]))
    _store_heads(o_ref, outs)


def _nbr_attn(q, k, v, bias):
    t = q.shape[0]
    nb = t // SEQ
    ng = SEQ // A_TQ

    def btype(b, g):
        return ((g > 0).astype(jnp.int32) + (g == ng - 1).astype(jnp.int32), 0, 0, 0)

    return pl.pallas_call(
        _nbr_kernel,
        grid=(nb, ng),
        in_specs=[pl.BlockSpec((A_TQ, QK_W), lambda b, g: (b * ng + g, 0)),
                  pl.BlockSpec((SEQ, QK_W), lambda b, g: (b, 0)),
                  pl.BlockSpec((SEQ, QK_W), lambda b, g: (b, 0)),
                  pl.BlockSpec((1, N_HEADS, A_TQ, A_TK), btype)],
        out_specs=pl.BlockSpec((A_TQ, BRANCH_W), lambda b, g: (b * ng + g, 0)),
        out_shape=jax.ShapeDtypeStruct((t, BRANCH_W), BF16),
        compiler_params=_params("parallel", "arbitrary"),
        name="nbr_attn",
    )(q, k, v, bias)


def _win_kernel(sink_ref, q_ref, k_ref, v_ref, bias_ref, o_ref):
    n = pl.program_id(1)
    start = pl.multiple_of(jnp.clip(B_TQ * n - B_WINDOW, 0, SEQ - B_TK), B_WINDOW)
    rows = pl.ds(start, B_TK)
    outs = []
    for h in range(N_HEADS):
        kv = h // (N_HEADS // KV_GROUPED)
        outs.append(_attend(q_ref[:, LANES * h:LANES * (h + 1)], k_ref[rows, LANES * kv:LANES * (kv + 1)],
                            v_ref, rows, kv, 2, bias=bias_ref[0, h], sink=sink_ref[h]))
    _store_heads(o_ref, outs)


def _win_attn(sink, q, k, v, bias):
    t = q.shape[0]
    nb = t // SEQ
    nq = SEQ // B_TQ

    def btype(b, n):
        return ((n > 0).astype(jnp.int32) + (n == nq - 1).astype(jnp.int32), 0, 0, 0)

    return pl.pallas_call(
        _win_kernel,
        grid=(nb, nq),
        in_specs=[pl.BlockSpec(memory_space=pltpu.SMEM),
                  pl.BlockSpec((B_TQ, QK_W), lambda b, n: (b * nq + n, 0)),
                  pl.BlockSpec((SEQ, KG_W), lambda b, n: (b, 0)),
                  pl.BlockSpec((SEQ, KG_W), lambda b, n: (b, 0)),
                  pl.BlockSpec((1, N_HEADS, B_TQ, B_TK), btype)],
        out_specs=pl.BlockSpec((B_TQ, BRANCH_W), lambda b, n: (b * nq + n, 0)),
        out_shape=jax.ShapeDtypeStruct((t, BRANCH_W), BF16),
        compiler_params=_params("parallel", "arbitrary"),
        name="win_attn",
    )(sink, q, k, v, bias)


def _dense_kernel(q_ref, k_ref, v_ref, o_ref, *, kv_heads):
    rows = slice(None)
    outs = []
    for h in range(N_HEADS):
        kv = h // (N_HEADS // kv_heads)
        outs.append(_attend(q_ref[:, LANES * h:LANES * (h + 1)], k_ref[:, LANES * kv:LANES * (kv + 1)],
                            v_ref, rows, kv, 1))
    _store_heads(o_ref, outs)


def _dense_attn(q, k, v, kv_heads, name):
    t = q.shape[0]
    nb = t // SEQ
    nq = SEQ // DENSE_TQ
    return pl.pallas_call(
        functools.partial(_dense_kernel, kv_heads=kv_heads),
        grid=(nb, nq),
        in_specs=[pl.BlockSpec((DENSE_TQ, QK_W), lambda b, n: (b * nq + n, 0)),
                  pl.BlockSpec((SEQ, kv_heads * LANES), lambda b, n: (b, 0)),
                  pl.BlockSpec((SEQ, kv_heads * LANES), lambda b, n: (b, 0))],
        out_specs=pl.BlockSpec((DENSE_TQ, BRANCH_W), lambda b, n: (b * nq + n, 0)),
        out_shape=jax.ShapeDtypeStruct((t, BRANCH_W), BF16),
        compiler_params=_params("parallel", "arbitrary"),
        name=name,
    )(q, k, v)


def _merge_kernel(x_ref, gpre_ref, gpost_ref, oa_ref, ob_ref, oc_ref, od_ref, wg_ref, wbr_ref, wo_ref, y_ref):
    x = x_ref[...]
    h = (_rms(x) * gpre_ref[...]).astype(BF16)
    merged = None
    for n, o_ref in enumerate((oa_ref, ob_ref, oc_ref, od_ref)):
        logits = jnp.dot(h, wg_ref[:, D_MODEL * n:D_MODEL * (n + 1)], preferred_element_type=F32)
        y = jnp.dot(o_ref[...], wbr_ref[n], preferred_element_type=F32)
        term = jax.nn.sigmoid(logits) * y
        merged = term if merged is None else merged + term
    m = jnp.dot(merged.astype(BF16), wo_ref[...], preferred_element_type=F32)
    y_ref[...] = x + _rms(m) * gpost_ref[...]


def _merge(x2, gpre, gpost, oa, ob, oc, od, wg, wbr, wo):
    t = x2.shape[0]
    tm = TOK_TILE

    def tok(w):
        return pl.BlockSpec((tm, w), lambda i: (i, 0))

    return pl.pallas_call(
        _merge_kernel,
        grid=(t // tm,),
        in_specs=[tok(D_MODEL), _resident(gpre.shape), _resident(gpost.shape), tok(BRANCH_W), tok(BRANCH_W),
                  tok(BRANCH_W), tok(BRANCH_W), _resident(wg.shape), _resident(wbr.shape), _resident(wo.shape)],
        out_specs=tok(D_MODEL),
        out_shape=jax.ShapeDtypeStruct((t, D_MODEL), F32),
        compiler_params=_params("parallel"),
        name="merge",
    )(x2, gpre, gpost, oa, ob, oc, od, wg, wbr, wo)


def _ffn_kernel(x_ref, gpre_ref, gpost_ref, wgu_ref, wdn_ref, y_ref):
    x = x_ref[...]
    h = (_rms(x) * gpre_ref[...]).astype(BF16)
    acc = None
    for c in range(D_FF // FF_CHUNK):
        lo = FF_CHUNK * c
        gate = jnp.dot(h, wgu_ref[:, lo:lo + FF_CHUNK], preferred_element_type=F32)
        up = jnp.dot(h, wgu_ref[:, D_FF + lo:D_FF + lo + FF_CHUNK], preferred_element_type=F32)
        act = (jax.nn.silu(gate) * up).astype(BF16)
        part = jnp.dot(act, wdn_ref[lo:lo + FF_CHUNK, :], preferred_element_type=F32)
        acc = part if acc is None else acc + part
    y_ref[...] = x + _rms(acc) * gpost_ref[...]


def _ffn(x2, gpre, gpost, wgu, wdn):
    t = x2.shape[0]
    tm = TOK_TILE
    tok = pl.BlockSpec((tm, D_MODEL), lambda i: (i, 0))
    return pl.pallas_call(
        _ffn_kernel,
        grid=(t // tm,),
        in_specs=[tok, _resident(gpre.shape), _resident(gpost.shape), _resident(wgu.shape), _resident(wdn.shape)],
        out_specs=tok,
        out_shape=jax.ShapeDtypeStruct((t, D_MODEL), F32),
        compiler_params=_params("parallel"),
        name="ffn",
    )(x2, gpre, gpost, wgu, wdn)


def _pad_heads(w, heads, dim):
    k = w.shape[0]
    w = w.reshape(k, heads, dim)
    return jnp.pad(w, ((0, 0), (0, 0), (0, LANES - dim))).reshape(k, heads * LANES)


def _toeplitz(w, nq, nk, shift):
    lead, length = w.shape[:-1], w.shape[-1]
    period = nq + nk + length
    wp = jnp.pad(w, [(0, 0)] * len(lead) + [(0, period - length)])
    flat = jnp.broadcast_to(wp[..., None, :], lead + (nq, period)).reshape(lead + (nq * period,))
    skew = flat[..., :nq * (period - 1)].reshape(lead + (nq, period - 1))
    return skew[..., shift:shift + nk]


def _rope_tables():
    inv_freq = 1.0 / (ROPE_THETA ** (jnp.arange(ROPE_HALF, dtype=F32) * (1.0 / ROPE_HALF)))

    def group(pos):
        ang = pos.astype(F32)[:, None] * inv_freq[None, :]
        c, s = jnp.cos(ang), jnp.sin(ang)
        return jnp.concatenate([c, c], axis=-1), jnp.concatenate([-s, s], axis=-1)

    gc, gs = group(jnp.arange(GRID_W))
    rc, rs = jnp.repeat(gc, GRID_W, axis=0), jnp.repeat(gs, GRID_W, axis=0)
    cc, cs = jnp.tile(gc, (SEQ // GRID_W, 1)), jnp.tile(gs, (SEQ // GRID_W, 1))
    pc, ps = group(jnp.arange(SEQ))
    ones = jnp.ones((SEQ, 2 * ROPE_HALF), F32)
    zeros = jnp.zeros((SEQ, 2 * ROPE_HALF), F32)
    ctab = jnp.stack([jnp.concatenate([rc, cc, rc, cc], axis=-1),
                      jnp.concatenate([rs, cs, rs, cs], axis=-1)])
    dtab = jnp.stack([jnp.concatenate([ones, ones, pc, ones], axis=-1),
                      jnp.concatenate([zeros, zeros, ps, zeros], axis=-1)])
    return ctab, dtab


def _group_mean_matrix():
    i = np.arange(2 * LANES)
    m = (i[:, None] // HEAD_DIM) == (i[None, :] // HEAD_DIM)
    return jnp.asarray(m.astype(np.float32) / HEAD_DIM, dtype=BF16)


def _t5_bucket_index(rel):
    nb = T5_BUCKETS // 2
    max_exact = nb // 2
    ret = (rel > 0).astype(np.int32) * nb
    n = np.abs(rel)
    large = max_exact + (np.log(np.maximum(n, 1) / max_exact) / np.log(T5_MAX_DIST / max_exact)
                         * (nb - max_exact)).astype(np.int32)
    large = np.minimum(large, nb - 1)
    return ret + np.where(n < max_exact, n, large)


def _nbr_bias(rpb):
    rows = SEQ // GRID_W
    ng = rows // A_QROWS
    qc = np.arange(GRID_W)
    cs = np.clip(qc - A_WIN_C // 2, 0, GRID_W - A_WIN_C)
    col_ok = (qc[None, :] >= cs[:, None]) & (qc[None, :] < cs[:, None] + A_WIN_C)
    cols = _toeplitz(rpb.astype(F32) * LOG2E, GRID_W, GRID_W, A_WIN_C - 1)
    cols = jnp.where(col_ok, cols, NEG_INF)
    masked = jnp.full_like(cols[:, :1], NEG_INF)
    slabs = []
    for g in (0, 1, ng - 1):
        ws = int(np.clip(A_QROWS * g - A_WIN_R // 2, 0, rows - A_KROWS))
        for qr in range(A_QROWS * g, A_QROWS * (g + 1)):
            rs = int(np.clip(qr - A_WIN_R // 2, 0, rows - A_WIN_R))
            for kr in range(ws, ws + A_KROWS):
                dr = kr - qr + (A_WIN_R - 1)
                slabs.append(cols[:, dr:dr + 1] if rs <= kr < rs + A_WIN_R else masked)
    tbl = jnp.concatenate(slabs, axis=1).reshape(N_HEADS, 3, A_QROWS, A_KROWS, GRID_W, GRID_W)
    return tbl.transpose(1, 0, 2, 4, 3, 5).reshape(3, N_HEADS, A_TQ, A_TK)


def _win_bias(t5_table):
    nq = SEQ // B_TQ
    rel_values = np.arange(-B_WINDOW, B_WINDOW + 1)
    by_rel = t5_table.astype(F32)[_t5_bucket_index(rel_values)].T * LOG2E
    by_rel = jnp.pad(by_rel, ((0, 0), (B_WINDOW, 0)))
    tables = []
    for n in (0, 1, nq - 1):
        start = int(np.clip(B_TQ * n - B_WINDOW, 0, SEQ - B_TK))
        rel = (start + np.arange(B_TK))[None, :] - (B_TQ * n + np.arange(B_TQ))[:, None]
        bias = _toeplitz(by_rel, B_TQ, B_TK, start - B_TQ * n + 2 * B_WINDOW)
        tables.append(jnp.where(np.abs(rel) <= B_WINDOW, bias, NEG_INF))
    return jnp.stack(tables)


def _layer_weights(w_in, c_q_norm, c_k_norm, d_w_q_up, d_w_kv_up):
    n_kr = 7 * BRANCH_W + D_Q_LORA + D_KV_LORA
    n_mix = n_kr + D_ROPE
    qs = HEAD_DIM ** -0.5 * LOG2E
    w_kr = jnp.pad(w_in[:, n_kr:n_mix], ((0, 0), (D_NOPE, LANES - D_NOPE - D_ROPE)))
    w = jnp.concatenate([w_in[:, :n_kr], w_kr], axis=1)
    cgain = jnp.concatenate([jnp.tile(c_q_norm * qs, N_HEADS), jnp.tile(c_k_norm, KV_GROUPED)])[None]
    dwq = _pad_heads(d_w_q_up, N_HEADS, D_NOPE + D_ROPE)
    kvu = d_w_kv_up.reshape(D_KV_LORA, N_HEADS, D_NOPE + D_V)
    dwkv = jnp.concatenate([_pad_heads(kvu[:, :, :D_NOPE].reshape(D_KV_LORA, -1), N_HEADS, D_NOPE),
                            _pad_heads(kvu[:, :, D_NOPE:].reshape(D_KV_LORA, -1), N_HEADS, D_V)], axis=1)
    bf = lambda w: w.astype(BF16)
    return bf(w), cgain.astype(F32), bf(dwq), bf(dwkv), bf(w_in[:, n_mix:])


def kernel(x, w_in, a_rpb, b_sink, t5_bias, c_q_norm, c_k_norm, d_q_norm, d_kv_norm, d_w_q_up, d_w_kv_up,
           w_branch, w_out, ln_pre_mix, ln_post_mix, ln_pre_ffn, ln_post_ffn, ffn_w_gu, ffn_w_down):
    b, s, d = x.shape
    assert (s, d) == (SEQ, D_MODEL)
    depth = w_in.shape[0]
    x2 = x.reshape(b * s, d)
    ctab, dtab = _rope_tables()
    gmat = _group_mean_matrix()
    wbias = _win_bias(t5_bias)
    row = lambda v: v.astype(F32)[None, :]
    for l in range(depth):
        w, cgain, dwq, dwkv, wg = _layer_weights(w_in[l], c_q_norm[l], c_k_norm[l], d_w_q_up[l], d_w_kv_up[l])
        (aq, ak, av, bq, bk, bv, cq, ck, cv, dq, dk, dv) = _proj(
            x2, row(ln_pre_mix[l]), w, gmat, cgain, ctab, dtab,
            row(d_q_norm[l]), row(d_kv_norm[l]), dwq, dwkv)
        oa = _nbr_attn(aq, ak, av, _nbr_bias(a_rpb[l]))
        ob = _win_attn(b_sink[l].astype(F32) * LOG2E, bq, bk, bv, wbias)
        oc = _dense_attn(cq, ck, cv, KV_GROUPED, "dense_c")
        od = _dense_attn(dq, dk, dv, N_HEADS, "dense_d")
        x2 = _merge(x2, row(ln_pre_mix[l]), row(ln_post_mix[l]), oa, ob, oc, od, wg,
                    w_branch[l].astype(BF16), w_out[l].astype(BF16))
        x2 = _ffn(x2, row(ln_pre_ffn[l]), row(ln_post_ffn[l]), ffn_w_gu[l].astype(BF16),
                  ffn_w_down[l].astype(BF16))
    return x2.reshape(b, s, d)
```

```python
import functools

import jax
import jax.numpy as jnp
import numpy as np
from jax import lax
from jax.experimental import pallas as pl
from jax.experimental.pallas import tpu as pltpu

D_MODEL = 1024
SEQ = 4096
GRID_W = 64
HEAD_DIM = 64
N_HEADS = 4
KV_GROUPED = 2
EPS = 1e-6
NEG_INF = -1e30
ROPE_THETA = 10000.0
A_WIN_R = 8
A_WIN_C = 16
B_WINDOW = 128
T5_BUCKETS = 32
T5_MAX_DIST = 128
D_Q_LORA = 256
D_KV_LORA = 128
D_NOPE = 64
D_ROPE = 32
D_V = 64
D_FF = 2816
BRANCH_W = 256

LANES = 128
QK_W = N_HEADS * LANES
KG_W = KV_GROUPED * LANES
ROPE_HALF = 16
LOG2E = 1.4426950408889634

A_QROWS = 4
A_KROWS = 12
A_TQ = A_QROWS * GRID_W
A_TK = A_KROWS * GRID_W
B_TQ = 256
B_TK = B_TQ + 2 * B_WINDOW
DENSE_TQ = 256
TOK_TILE = 1024
FF_CHUNK = 256
VMEM_LIMIT = 56 * 1024 * 1024

BF16 = jnp.bfloat16
F32 = jnp.float32


def _resident(arr, layer=None):
    if layer is None:
        return pl.BlockSpec(arr.shape, lambda *_: (0,) * arr.ndim, pipeline_mode=pl.Buffered(1))
    return pl.BlockSpec((None,) + arr.shape[1:], lambda *_: (layer,) + (0,) * (arr.ndim - 1),
                        pipeline_mode=pl.Buffered(1))


def _params(*sem):
    return pltpu.CompilerParams(dimension_semantics=sem, vmem_limit_bytes=VMEM_LIMIT)


def _rms(x):
    return x * lax.rsqrt(jnp.mean(x * x, axis=-1, keepdims=True) + EPS)


def _rope128(x, cos, sin_signed):
    lane = lax.broadcasted_iota(jnp.int32, x.shape, 1)
    first = (lane % (2 * ROPE_HALF)) < ROPE_HALF
    rot = jnp.where(first, pltpu.roll(x, LANES - ROPE_HALF, 1), pltpu.roll(x, ROPE_HALF, 1))
    return x * cos + rot * sin_signed


def _with_ones(v):
    lane = lax.broadcasted_iota(jnp.int32, v.shape, 1)
    return jnp.where(lane % LANES == HEAD_DIM, 1.0, v)


def _spread_heads(o_ref, x, ones=False):
    lane = lax.broadcasted_iota(jnp.int32, (x.shape[0], LANES), 1)
    low = lane < HEAD_DIM
    fill = jnp.where(lane == HEAD_DIM, 1.0, 0.0) if ones else 0.0
    for c in range(x.shape[1] // LANES):
        pair = x[:, LANES * c:LANES * (c + 1)]
        o_ref[:, 2 * LANES * c:2 * LANES * c + LANES] = jnp.where(low, pair, fill).astype(BF16)
        o_ref[:, 2 * LANES * c + LANES:2 * LANES * (c + 1)] = (
            jnp.where(low, pltpu.roll(pair, HEAD_DIM, 1), fill).astype(BF16))


def _proj_kernel(x_ref, g_ref, w_ref, gmat_ref, cgain_ref, ctab_ref, dtab_ref,
                 dqn_ref, dkvn_ref, dwq_ref, dwkv_ref,
                 aq_ref, ak_ref, av_ref, bq_ref, bk_ref, bv_ref, cq_ref, ck_ref, cv_ref,
                 dq_ref, dk_ref, dv_ref):
    h = (_rms(x_ref[...]) * g_ref[...]).astype(BF16)
    qs = HEAD_DIM ** -0.5 * LOG2E
    a0, b0, c0, d0 = 0, 3 * BRANCH_W, 5 * BRANCH_W, 7 * BRANCH_W

    pa = jnp.dot(h, w_ref[:, a0:b0], preferred_element_type=F32)
    _spread_heads(aq_ref, pa[:, :BRANCH_W] * qs)
    _spread_heads(ak_ref, pa[:, BRANCH_W:2 * BRANCH_W])
    _spread_heads(av_ref, pa[:, 2 * BRANCH_W:], ones=True)

    pb = jnp.dot(h, w_ref[:, b0:c0], preferred_element_type=F32)
    _spread_heads(bq_ref, pb[:, :BRANCH_W] * qs)
    _spread_heads(bk_ref, pb[:, BRANCH_W:BRANCH_W + LANES])
    _spread_heads(bv_ref, pb[:, BRANCH_W + LANES:], ones=True)

    pc = jnp.dot(h, w_ref[:, c0:d0], preferred_element_type=F32)
    _spread_heads(cv_ref, pc[:, BRANCH_W + LANES:], ones=True)
    ccos = ctab_ref[0]
    csin = ctab_ref[1]

    def norm_rope(y, gmat, gain):
        sq = y * y
        hi = sq.astype(BF16)
        lo = (sq - hi.astype(F32)).astype(BF16)
        ms = jnp.dot(hi, gmat, preferred_element_type=F32) + jnp.dot(lo, gmat, preferred_element_type=F32)
        yn = y * lax.rsqrt(ms + EPS) * gain
        return jnp.concatenate([_rope128(yn[:, LANES * c:LANES * (c + 1)], ccos, csin)
                                for c in range(y.shape[1] // LANES)], axis=1)

    _spread_heads(cq_ref, norm_rope(pc[:, :BRANCH_W], gmat_ref[...], cgain_ref[:, :BRANCH_W]))
    _spread_heads(ck_ref, norm_rope(pc[:, BRANCH_W:BRANCH_W + LANES], gmat_ref[:LANES, :LANES],
                                    cgain_ref[:, BRANCH_W:]))

    pd = jnp.dot(h, w_ref[:, d0:], preferred_element_type=F32)
    dcos = dtab_ref[0]
    dsin = dtab_ref[1]
    cq = (_rms(pd[:, :D_Q_LORA]) * dqn_ref[...]).astype(BF16)
    qd = jnp.dot(cq, dwq_ref[...], preferred_element_type=F32)
    ckv = (_rms(pd[:, D_Q_LORA:D_Q_LORA + D_KV_LORA]) * dkvn_ref[...]).astype(BF16)
    kvd = jnp.dot(ckv, dwkv_ref[...], preferred_element_type=F32)
    kr = _rope128(pd[:, D_Q_LORA + D_KV_LORA:], dcos, dsin)
    scale = (D_NOPE + D_ROPE) ** -0.5 * LOG2E
    for hd in range(N_HEADS):
        sl = slice(LANES * hd, LANES * (hd + 1))
        dq_ref[:, sl] = (_rope128(qd[:, sl], dcos, dsin) * scale).astype(BF16)
        dk_ref[:, sl] = (kvd[:, sl] + kr).astype(BF16)
    dv_ref[...] = _with_ones(kvd[:, QK_W:]).astype(BF16)


def _proj(layer, x2, g, w, gmat, cgain, ctab, dtab, dqn, dkvn, dwq, dwkv):
    t = x2.shape[0]
    tm = TOK_TILE
    seq_tiles = SEQ // tm

    def tok(w):
        return pl.BlockSpec((tm, w), lambda i: (i, 0))

    def tab():
        return pl.BlockSpec((2, tm, LANES), lambda i: (0, i % seq_tiles, 0))

    out_w = (QK_W, QK_W, QK_W, QK_W, KG_W, KG_W, QK_W, KG_W, KG_W, QK_W, QK_W, QK_W)
    return pl.pallas_call(
        _proj_kernel,
        grid=(t // tm,),
        in_specs=[tok(D_MODEL), _resident(g, layer), _resident(w, layer), _resident(gmat),
                  _resident(cgain, layer), tab(), tab(), _resident(dqn, layer), _resident(dkvn, layer),
                  _resident(dwq, layer), _resident(dwkv, layer)],
        out_specs=[tok(ow) for ow in out_w],
        out_shape=[jax.ShapeDtypeStruct((t, ow), BF16) for ow in out_w],
        compiler_params=_params("parallel"),
        name="proj",
    )(x2, g, w, gmat, cgain, ctab, dtab, dqn, dkvn, dwq, dwkv)


def _attend(q, k, v_ref, rows, kv_head, pv_heads, bias=None, sink=None):
    slab, blk = divmod(kv_head, pv_heads)
    v = v_ref[rows, pv_heads * LANES * slab:pv_heads * LANES * (slab + 1)]
    s = lax.dot_general(q, k, (((1,), (1,)), ((), ())), preferred_element_type=F32)
    if bias is not None:
        s = s + bias
    m = jnp.max(s, axis=-1, keepdims=True)
    if sink is not None:
        m = jnp.maximum(m, sink)
    pv = jnp.dot(jnp.exp2(s - m).astype(BF16), v, preferred_element_type=F32)[:, LANES * blk:LANES * (blk + 1)]
    total = pv[:, HEAD_DIM:HEAD_DIM + 1]
    if sink is not None:
        total = total + jnp.exp2(sink - m)
    return pv * (1.0 / total)


def _store_heads(o_ref, outs):
    low = lax.broadcasted_iota(jnp.int32, outs[0].shape, 1) < HEAD_DIM
    for g in range(N_HEADS // 2):
        pair = jnp.where(low, outs[2 * g], pltpu.roll(outs[2 * g + 1], HEAD_DIM, 1))
        o_ref[:, LANES * g:LANES * (g + 1)] = pair.astype(BF16)


def _local_kernel(sink_ref, aq_ref, ak_ref, av_ref, abias_ref, bq_ref, bk_ref, bv_ref, bbias_ref, oa_ref, ob_ref):
    j = pl.program_id(1)
    a_start = pl.multiple_of(jnp.clip(A_QROWS * j - A_WIN_R // 2, 0, GRID_W - A_KROWS) * GRID_W, A_TQ)
    a_rows = pl.ds(a_start, A_TK)
    b_start = pl.multiple_of(jnp.clip(B_TQ * j - B_WINDOW, 0, SEQ - B_TK), B_WINDOW)
    b_rows = pl.ds(b_start, B_TK)
    a_outs, b_outs = [], []
    for h in range(N_HEADS):
        sl = slice(LANES * h, LANES * (h + 1))
        kv = h // (N_HEADS // KV_GROUPED)
        a_outs.append(_attend(aq_ref[:, sl], ak_ref[a_rows, sl], av_ref, a_rows, h, 2, bias=abias_ref[0, h]))
        b_outs.append(_attend(bq_ref[:, sl], bk_ref[b_rows, LANES * kv:LANES * (kv + 1)], bv_ref, b_rows, kv, 2,
                              bias=bbias_ref[0, h], sink=sink_ref[h]))
    _store_heads(oa_ref, a_outs)
    _store_heads(ob_ref, b_outs)


def _local_attn(sink, aq, ak, av, abias, bq, bk, bv, bbias):
    assert A_TQ == B_TQ
    t = aq.shape[0]
    nb = t // SEQ
    nj = SEQ // A_TQ

    def tok(w):
        return pl.BlockSpec((A_TQ, w), lambda b, j: (b * nj + j, 0))

    def seq(w):
        return pl.BlockSpec((SEQ, w), lambda b, j: (b, 0))

    def bias(tk):
        return pl.BlockSpec((1, N_HEADS, A_TQ, tk),
                            lambda b, j: ((j > 0).astype(jnp.int32) + (j == nj - 1).astype(jnp.int32), 0, 0, 0))

    return pl.pallas_call(
        _local_kernel,
        grid=(nb, nj),
        in_specs=[pl.BlockSpec(memory_space=pltpu.SMEM),
                  tok(QK_W), seq(QK_W), seq(QK_W), bias(A_TK),
                  tok(QK_W), seq(KG_W), seq(KG_W), bias(B_TK)],
        out_specs=[tok(BRANCH_W), tok(BRANCH_W)],
        out_shape=[jax.ShapeDtypeStruct((t, BRANCH_W), BF16)] * 2,
        compiler_params=_params("parallel", "arbitrary"),
        name="local_attn",
    )(sink, aq, ak, av, abias, bq, bk, bv, bbias)


def _dense_kernel(q_ref, k_ref, v_ref, o_ref, *, kv_heads):
    rows = slice(None)
    outs = []
    for h in range(N_HEADS):
        kv = h // (N_HEADS // kv_heads)
        outs.append(_attend(q_ref[:, LANES * h:LANES * (h + 1)], k_ref[:, LANES * kv:LANES * (kv + 1)],
                            v_ref, rows, kv, 1))
    _store_heads(o_ref, outs)


def _dense_attn(q, k, v, kv_heads, name):
    t = q.shape[0]
    nb = t // SEQ
    nq = SEQ // DENSE_TQ
    return pl.pallas_call(
        functools.partial(_dense_kernel, kv_heads=kv_heads),
        grid=(nb, nq),
        in_specs=[pl.BlockSpec((DENSE_TQ, QK_W), lambda b, n: (b * nq + n, 0)),
                  pl.BlockSpec((SEQ, kv_heads * LANES), lambda b, n: (b, 0)),
                  pl.BlockSpec((SEQ, kv_heads * LANES), lambda b, n: (b, 0))],
        out_specs=pl.BlockSpec((DENSE_TQ, BRANCH_W), lambda b, n: (b * nq + n, 0)),
        out_shape=jax.ShapeDtypeStruct((t, BRANCH_W), BF16),
        compiler_params=_params("parallel", "arbitrary"),
        name=name,
    )(q, k, v)


def _merge_kernel(x_ref, gpre_ref, gpost_ref, oa_ref, ob_ref, oc_ref, od_ref, wg_ref, wbr_ref, wo_ref, y_ref):
    x = x_ref[...]
    h = (_rms(x) * gpre_ref[...]).astype(BF16)
    merged = None
    for n, o_ref in enumerate((oa_ref, ob_ref, oc_ref, od_ref)):
        logits = jnp.dot(h, wg_ref[:, D_MODEL * n:D_MODEL * (n + 1)], preferred_element_type=F32)
        y = jnp.dot(o_ref[...], wbr_ref[n], preferred_element_type=F32)
        term = jax.nn.sigmoid(logits) * y
        merged = term if merged is None else merged + term
    m = jnp.dot(merged.astype(BF16), wo_ref[...], preferred_element_type=F32)
    y_ref[...] = x + _rms(m) * gpost_ref[...]


def _merge(layer, x2, gpre, gpost, oa, ob, oc, od, wg, wbr, wo):
    t = x2.shape[0]
    tm = TOK_TILE

    def tok(w):
        return pl.BlockSpec((tm, w), lambda i: (i, 0))

    return pl.pallas_call(
        _merge_kernel,
        grid=(t // tm,),
        in_specs=[tok(D_MODEL), _resident(gpre, layer), _resident(gpost, layer), tok(BRANCH_W), tok(BRANCH_W),
                  tok(BRANCH_W), tok(BRANCH_W), _resident(wg, layer), _resident(wbr, layer),
                  _resident(wo, layer)],
        out_specs=tok(D_MODEL),
        out_shape=jax.ShapeDtypeStruct((t, D_MODEL), F32),
        compiler_params=_params("parallel"),
        name="merge",
    )(x2, gpre, gpost, oa, ob, oc, od, wg, wbr, wo)


def _ffn_kernel(x_ref, gpre_ref, gpost_ref, wgu_ref, wdn_ref, y_ref):
    x = x_ref[...]
    h = (_rms(x) * gpre_ref[...]).astype(BF16)
    acc = None
    for c in range(D_FF // FF_CHUNK):
        lo = FF_CHUNK * c
        gate = jnp.dot(h, wgu_ref[:, lo:lo + FF_CHUNK], preferred_element_type=F32)
        up = jnp.dot(h, wgu_ref[:, D_FF + lo:D_FF + lo + FF_CHUNK], preferred_element_type=F32)
        act = (jax.nn.silu(gate) * up).astype(BF16)
        part = jnp.dot(act, wdn_ref[lo:lo + FF_CHUNK, :], preferred_element_type=F32)
        acc = part if acc is None else acc + part
    y_ref[...] = x + _rms(acc) * gpost_ref[...]


def _ffn(layer, x2, gpre, gpost, wgu, wdn):
    t = x2.shape[0]
    tm = TOK_TILE
    tok = pl.BlockSpec((tm, D_MODEL), lambda i: (i, 0))
    return pl.pallas_call(
        _ffn_kernel,
        grid=(t // tm,),
        in_specs=[tok, _resident(gpre, layer), _resident(gpost, layer), _resident(wgu, layer),
                  _resident(wdn, layer)],
        out_specs=tok,
        out_shape=jax.ShapeDtypeStruct((t, D_MODEL), F32),
        compiler_params=_params("parallel"),
        name="ffn",
    )(x2, gpre, gpost, wgu, wdn)


def _toeplitz(w, nq, nk, shift):
    lead, length = w.shape[:-1], w.shape[-1]
    period = nq + nk + length
    wp = jnp.pad(w, [(0, 0)] * len(lead) + [(0, period - length)])
    flat = jnp.broadcast_to(wp[..., None, :], lead + (nq, period)).reshape(lead + (nq * period,))
    skew = flat[..., :nq * (period - 1)].reshape(lead + (nq, period - 1))
    return skew[..., shift:shift + nk]


def _rope_tables():
    inv_freq = 1.0 / (ROPE_THETA ** (jnp.arange(ROPE_HALF, dtype=F32) * (1.0 / ROPE_HALF)))

    def group(pos):
        ang = pos.astype(F32)[:, None] * inv_freq[None, :]
        c, s = jnp.cos(ang), jnp.sin(ang)
        return jnp.concatenate([c, c], axis=-1), jnp.concatenate([-s, s], axis=-1)

    gc, gs = group(jnp.arange(GRID_W))
    rc, rs = jnp.repeat(gc, GRID_W, axis=0), jnp.repeat(gs, GRID_W, axis=0)
    cc, cs = jnp.tile(gc, (SEQ // GRID_W, 1)), jnp.tile(gs, (SEQ // GRID_W, 1))
    pc, ps = group(jnp.arange(SEQ))
    ones = jnp.ones((SEQ, 2 * ROPE_HALF), F32)
    zeros = jnp.zeros((SEQ, 2 * ROPE_HALF), F32)
    ctab = jnp.stack([jnp.concatenate([rc, cc, rc, cc], axis=-1),
                      jnp.concatenate([rs, cs, rs, cs], axis=-1)])
    dtab = jnp.stack([jnp.concatenate([ones, ones, pc, ones], axis=-1),
                      jnp.concatenate([zeros, zeros, ps, zeros], axis=-1)])
    return ctab, dtab


def _group_mean_matrix():
    i = np.arange(2 * LANES)
    m = (i[:, None] // HEAD_DIM) == (i[None, :] // HEAD_DIM)
    return jnp.asarray(m.astype(np.float32) / HEAD_DIM, dtype=BF16)


def _t5_bucket_index(rel):
    nb = T5_BUCKETS // 2
    max_exact = nb // 2
    ret = (rel > 0).astype(np.int32) * nb
    n = np.abs(rel)
    large = max_exact + (np.log(np.maximum(n, 1) / max_exact) / np.log(T5_MAX_DIST / max_exact)
                         * (nb - max_exact)).astype(np.int32)
    large = np.minimum(large, nb - 1)
    return ret + np.where(n < max_exact, n, large)


def _nbr_bias(rpb):
    rows = SEQ // GRID_W
    ng = rows // A_QROWS
    qc = np.arange(GRID_W)
    cs = np.clip(qc - A_WIN_C // 2, 0, GRID_W - A_WIN_C)
    col_ok = (qc[None, :] >= cs[:, None]) & (qc[None, :] < cs[:, None] + A_WIN_C)
    cols = _toeplitz(rpb.astype(F32) * LOG2E, GRID_W, GRID_W, A_WIN_C - 1)
    cols = jnp.where(col_ok, cols, NEG_INF).transpose(0, 2, 1, 3)
    blocks = []
    for g in (0, 1, ng - 1):
        ws = int(np.clip(A_QROWS * g - A_WIN_R // 2, 0, rows - A_KROWS))
        for qr in range(A_QROWS * g, A_QROWS * (g + 1)):
            rs = int(np.clip(qr - A_WIN_R // 2, 0, rows - A_WIN_R))
            dr0 = rs - qr + (A_WIN_R - 1)
            blk = jnp.pad(cols[:, :, dr0:dr0 + A_WIN_R], ((0, 0), (0, 0), (rs - ws, ws + A_KROWS - rs - A_WIN_R), (0, 0)),
                          constant_values=NEG_INF)
            blocks.append(blk.reshape(N_HEADS, GRID_W, A_TK))
    tbl = jnp.stack(blocks).reshape(3, A_QROWS, N_HEADS, GRID_W, A_TK)
    return tbl.transpose(0, 2, 1, 3, 4).reshape(3, N_HEADS, A_TQ, A_TK)


def _win_bias(t5_table):
    nq = SEQ // B_TQ
    rel_values = np.arange(-B_WINDOW, B_WINDOW + 1)
    by_rel = t5_table.astype(F32)[_t5_bucket_index(rel_values)].T * LOG2E
    by_rel = jnp.pad(by_rel, ((0, 0), (B_WINDOW, 0)))
    full = _toeplitz(by_rel, B_TQ, B_TK + 2 * B_WINDOW, 0)
    tables = []
    for n in (0, 1, nq - 1):
        start = int(np.clip(B_TQ * n - B_WINDOW, 0, SEQ - B_TK))
        rel = (start + np.arange(B_TK))[None, :] - (B_TQ * n + np.arange(B_TQ))[:, None]
        shift = start - B_TQ * n + 2 * B_WINDOW
        tables.append(jnp.where(np.abs(rel) <= B_WINDOW, full[:, :, shift:shift + B_TK], NEG_INF))
    return jnp.stack(tables)


def _stacked_weights(w_in, c_q_norm, c_k_norm, d_w_q_up, d_w_kv_up):
    depth = w_in.shape[0]
    n_kr = 7 * BRANCH_W + D_Q_LORA + D_KV_LORA
    n_mix = n_kr + D_ROPE
    qs = HEAD_DIM ** -0.5 * LOG2E
    w_kr = jnp.pad(w_in[:, :, n_kr:n_mix], ((0, 0), (0, 0), (D_NOPE, LANES - D_NOPE - D_ROPE)))
    w = jnp.concatenate([w_in[:, :, :n_kr], w_kr], axis=2).astype(BF16)
    wg = w_in[:, :, n_mix:].astype(BF16)
    cgain = jnp.concatenate([jnp.tile(c_q_norm * qs, (1, N_HEADS)), jnp.tile(c_k_norm, (1, KV_GROUPED))],
                            axis=1).astype(F32)[:, None, :]

    def pad_heads(a, dim):
        return jnp.pad(a, ((0, 0), (0, 0), (0, 0), (0, LANES - dim))).reshape(depth, a.shape[1], QK_W)

    dwq = pad_heads(d_w_q_up.reshape(depth, D_Q_LORA, N_HEADS, D_NOPE + D_ROPE), D_NOPE + D_ROPE)
    kvu = d_w_kv_up.reshape(depth, D_KV_LORA, N_HEADS, D_NOPE + D_V)
    dwkv = jnp.concatenate([pad_heads(kvu[..., :D_NOPE], D_NOPE), pad_heads(kvu[..., D_NOPE:], D_V)], axis=2)
    return w, wg, cgain, dwq.astype(BF16), dwkv.astype(BF16)


def kernel(x, w_in, a_rpb, b_sink, t5_bias, c_q_norm, c_k_norm, d_q_norm, d_kv_norm, d_w_q_up, d_w_kv_up,
           w_branch, w_out, ln_pre_mix, ln_post_mix, ln_pre_ffn, ln_post_ffn, ffn_w_gu, ffn_w_down):
    b, s, d = x.shape
    assert (s, d) == (SEQ, D_MODEL)
    depth = w_in.shape[0]
    x2 = x.reshape(b * s, d)
    ctab, dtab = _rope_tables()
    gmat = _group_mean_matrix()
    wbias = _win_bias(t5_bias)
    w, wg, cgain, dwq, dwkv = _stacked_weights(w_in, c_q_norm, c_k_norm, d_w_q_up, d_w_kv_up)
    wbr, wo, wgu, wdn = (a.astype(BF16) for a in (w_branch, w_out, ffn_w_gu, ffn_w_down))
    g_pre_mix, g_post_mix, g_pre_ffn, g_post_ffn, dqn, dkvn = (
        v.astype(F32)[:, None, :] for v in (ln_pre_mix, ln_post_mix, ln_pre_ffn, ln_post_ffn, d_q_norm, d_kv_norm))
    sink = b_sink.astype(F32) * LOG2E
    for l in range(depth):
        (aq, ak, av, bq, bk, bv, cq, ck, cv, dq, dk, dv) = _proj(
            l, x2, g_pre_mix, w, gmat, cgain, ctab, dtab, dqn, dkvn, dwq, dwkv)
        oa, ob = _local_attn(sink[l], aq, ak, av, _nbr_bias(a_rpb[l]), bq, bk, bv, wbias)
        oc = _dense_attn(cq, ck, cv, KV_GROUPED, "dense_c")
        od = _dense_attn(dq, dk, dv, N_HEADS, "dense_d")
        x2 = _merge(l, x2, g_pre_mix, g_post_mix, oa, ob, oc, od, wg, wbr, wo)
        x2 = _ffn(l, x2, g_pre_ffn, g_post_ffn, wgu, wdn)
    return x2.reshape(b, s, d)
```

```python
import functools

import jax
import jax.numpy as jnp
import numpy as np
from jax import lax
from jax.experimental import pallas as pl
from jax.experimental.pallas import tpu as pltpu

D_MODEL = 1024
SEQ = 4096
GRID_W = 64
HEAD_DIM = 64
N_HEADS = 4
KV_GROUPED = 2
EPS = 1e-6
NEG_INF = -1e30
ROPE_THETA = 10000.0
A_WIN_R = 8
A_WIN_C = 16
B_WINDOW = 128
T5_BUCKETS = 32
T5_MAX_DIST = 128
D_Q_LORA = 256
D_KV_LORA = 128
D_NOPE = 64
D_ROPE = 32
D_V = 64
D_FF = 2816
BRANCH_W = 256

LANES = 128
QK_W = N_HEADS * LANES
KG_W = KV_GROUPED * LANES
ROPE_HALF = 16
LOG2E = 1.4426950408889634

A_QROWS = 4
A_KROWS = 12
A_TQ = A_QROWS * GRID_W
A_TK = A_KROWS * GRID_W
B_TQ = 256
B_TK = B_TQ + 2 * B_WINDOW
DENSE_TQ = 512
DENSE_SUB = 256
TOK_TILE = 1024
FF_CHUNK = 256
MERGE_CHUNK = 256
VMEM_LIMIT = 56 * 1024 * 1024

BF16 = jnp.bfloat16
F32 = jnp.float32


def _resident(arr, layer=None):
    if layer is None:
        return pl.BlockSpec(arr.shape, lambda *_: (0,) * arr.ndim, pipeline_mode=pl.Buffered(1))
    return pl.BlockSpec((None,) + arr.shape[1:], lambda *_: (layer,) + (0,) * (arr.ndim - 1),
                        pipeline_mode=pl.Buffered(1))


def _params(*sem):
    return pltpu.CompilerParams(dimension_semantics=sem, vmem_limit_bytes=VMEM_LIMIT)


def _rms(x):
    return x * lax.rsqrt(jnp.mean(x * x, axis=-1, keepdims=True) + EPS)


def _rope128(x, cos, sin_signed):
    lane = lax.broadcasted_iota(jnp.int32, x.shape, 1)
    first = (lane % (2 * ROPE_HALF)) < ROPE_HALF
    rot = jnp.where(first, pltpu.roll(x, LANES - ROPE_HALF, 1), pltpu.roll(x, ROPE_HALF, 1))
    return x * cos + rot * sin_signed


def _with_ones(v):
    lane = lax.broadcasted_iota(jnp.int32, v.shape, 1)
    return jnp.where(lane % LANES == HEAD_DIM, 1.0, v)


def _spread_heads(o_ref, x, ones=False):
    lane = lax.broadcasted_iota(jnp.int32, (x.shape[0], LANES), 1)
    low = lane < HEAD_DIM
    fill = jnp.where(lane == HEAD_DIM, 1.0, 0.0) if ones else 0.0
    for c in range(x.shape[1] // LANES):
        pair = x[:, LANES * c:LANES * (c + 1)]
        o_ref[:, 2 * LANES * c:2 * LANES * c + LANES] = jnp.where(low, pair, fill).astype(BF16)
        o_ref[:, 2 * LANES * c + LANES:2 * LANES * (c + 1)] = (
            jnp.where(low, pltpu.roll(pair, HEAD_DIM, 1), fill).astype(BF16))


def _proj_kernel(x_ref, g_ref, w_ref, gmat_ref, cgain_ref, ctab_ref, dtab_ref,
                 dqn_ref, dkvn_ref, dwq_ref, dwkv_ref,
                 aq_ref, ak_ref, av_ref, bq_ref, bk_ref, bv_ref, cq_ref, ck_ref, cv_ref,
                 dq_ref, dk_ref, dv_ref):
    h = (_rms(x_ref[...]) * g_ref[...]).astype(BF16)
    qs = HEAD_DIM ** -0.5 * LOG2E
    a0, b0, c0, d0 = 0, 3 * BRANCH_W, 5 * BRANCH_W, 7 * BRANCH_W

    pa = jnp.dot(h, w_ref[:, a0:b0], preferred_element_type=F32)
    _spread_heads(aq_ref, pa[:, :BRANCH_W] * qs)
    _spread_heads(ak_ref, pa[:, BRANCH_W:2 * BRANCH_W])
    _spread_heads(av_ref, pa[:, 2 * BRANCH_W:], ones=True)

    pb = jnp.dot(h, w_ref[:, b0:c0], preferred_element_type=F32)
    _spread_heads(bq_ref, pb[:, :BRANCH_W] * qs)
    _spread_heads(bk_ref, pb[:, BRANCH_W:BRANCH_W + LANES])
    _spread_heads(bv_ref, pb[:, BRANCH_W + LANES:], ones=True)

    pc = jnp.dot(h, w_ref[:, c0:d0], preferred_element_type=F32)
    _spread_heads(cv_ref, pc[:, BRANCH_W + LANES:], ones=True)
    ccos = ctab_ref[0]
    csin = ctab_ref[1]

    def norm_rope(y, gmat, gain):
        sq = y * y
        hi = sq.astype(BF16)
        lo = (sq - hi.astype(F32)).astype(BF16)
        ms = jnp.dot(hi, gmat, preferred_element_type=F32) + jnp.dot(lo, gmat, preferred_element_type=F32)
        yn = y * lax.rsqrt(ms + EPS) * gain
        return jnp.concatenate([_rope128(yn[:, LANES * c:LANES * (c + 1)], ccos, csin)
                                for c in range(y.shape[1] // LANES)], axis=1)

    _spread_heads(cq_ref, norm_rope(pc[:, :BRANCH_W], gmat_ref[...], cgain_ref[:, :BRANCH_W]))
    _spread_heads(ck_ref, norm_rope(pc[:, BRANCH_W:BRANCH_W + LANES], gmat_ref[:LANES, :LANES],
                                    cgain_ref[:, BRANCH_W:]))

    pd = jnp.dot(h, w_ref[:, d0:], preferred_element_type=F32)
    dcos = dtab_ref[0]
    dsin = dtab_ref[1]
    cq = (_rms(pd[:, :D_Q_LORA]) * dqn_ref[...]).astype(BF16)
    qd = jnp.dot(cq, dwq_ref[...], preferred_element_type=F32)
    ckv = (_rms(pd[:, D_Q_LORA:D_Q_LORA + D_KV_LORA]) * dkvn_ref[...]).astype(BF16)
    kvd = jnp.dot(ckv, dwkv_ref[...], preferred_element_type=F32)
    kr = _rope128(pd[:, D_Q_LORA + D_KV_LORA:], dcos, dsin)
    scale = (D_NOPE + D_ROPE) ** -0.5 * LOG2E
    for hd in range(N_HEADS):
        sl = slice(LANES * hd, LANES * (hd + 1))
        dq_ref[:, sl] = (_rope128(qd[:, sl], dcos, dsin) * scale).astype(BF16)
        dk_ref[:, sl] = (kvd[:, sl] + kr).astype(BF16)
    dv_ref[...] = _with_ones(kvd[:, QK_W:]).astype(BF16)


def _proj(layer, x2, g, w, gmat, cgain, ctab, dtab, dqn, dkvn, dwq, dwkv):
    t = x2.shape[0]
    tm = TOK_TILE
    seq_tiles = SEQ // tm

    def tok(w):
        return pl.BlockSpec((tm, w), lambda i: (i, 0))

    def tab():
        return pl.BlockSpec((2, tm, LANES), lambda i: (0, i % seq_tiles, 0))

    out_w = (QK_W, QK_W, QK_W, QK_W, KG_W, KG_W, QK_W, KG_W, KG_W, QK_W, QK_W, QK_W)
    return pl.pallas_call(
        _proj_kernel,
        grid=(t // tm,),
        in_specs=[tok(D_MODEL), _resident(g, layer), _resident(w, layer), _resident(gmat),
                  _resident(cgain, layer), tab(), tab(), _resident(dqn, layer), _resident(dkvn, layer),
                  _resident(dwq, layer), _resident(dwkv, layer)],
        out_specs=[tok(ow) for ow in out_w],
        out_shape=[jax.ShapeDtypeStruct((t, ow), BF16) for ow in out_w],
        compiler_params=_params("parallel"),
        name="proj",
    )(x2, g, w, gmat, cgain, ctab, dtab, dqn, dkvn, dwq, dwkv)


def _attend(q, k, v_ref, rows, kv_head, pv_heads, bias=None, sink=None):
    slab, blk = divmod(kv_head, pv_heads)
    v = v_ref[rows, pv_heads * LANES * slab:pv_heads * LANES * (slab + 1)]
    s = lax.dot_general(q, k, (((1,), (1,)), ((), ())), preferred_element_type=F32)
    if bias is not None:
        s = s + bias
    m = jnp.max(s, axis=-1, keepdims=True)
    if sink is not None:
        m = jnp.maximum(m, sink)
    pv = jnp.dot(jnp.exp2(s - m).astype(BF16), v, preferred_element_type=F32)[:, LANES * blk:LANES * (blk + 1)]
    total = pv[:, HEAD_DIM:HEAD_DIM + 1]
    if sink is not None:
        total = total + jnp.exp2(sink - m)
    return pv * (1.0 / total)


def _store_heads(o_ref, outs, rows=slice(None)):
    low = lax.broadcasted_iota(jnp.int32, outs[0].shape, 1) < HEAD_DIM
    for g in range(N_HEADS // 2):
        pair = jnp.where(low, outs[2 * g], pltpu.roll(outs[2 * g + 1], HEAD_DIM, 1))
        o_ref[rows, LANES * g:LANES * (g + 1)] = pair.astype(BF16)


def _local_kernel(sink_ref, aq_ref, ak_ref, av_ref, abias_ref, bq_ref, bk_ref, bv_ref, bbias_ref, oa_ref, ob_ref):
    j = pl.program_id(1)
    a_start = pl.multiple_of(jnp.clip(A_QROWS * j - A_WIN_R // 2, 0, GRID_W - A_KROWS) * GRID_W, A_TQ)
    a_rows = pl.ds(a_start, A_TK)
    b_start = pl.multiple_of(jnp.clip(B_TQ * j - B_WINDOW, 0, SEQ - B_TK), B_WINDOW)
    b_rows = pl.ds(b_start, B_TK)
    a_outs, b_outs = [], []
    for h in range(N_HEADS):
        sl = slice(LANES * h, LANES * (h + 1))
        kv = h // (N_HEADS // KV_GROUPED)
        a_outs.append(_attend(aq_ref[:, sl], ak_ref[a_rows, sl], av_ref, a_rows, h, 2, bias=abias_ref[0, h]))
        b_outs.append(_attend(bq_ref[:, sl], bk_ref[b_rows, LANES * kv:LANES * (kv + 1)], bv_ref, b_rows, kv, 2,
                              bias=bbias_ref[0, h], sink=sink_ref[h]))
    _store_heads(oa_ref, a_outs)
    _store_heads(ob_ref, b_outs)


def _local_attn(layer, sink, aq, ak, av, abias, bq, bk, bv, bbias):
    assert A_TQ == B_TQ
    t = aq.shape[0]
    nb = t // SEQ
    nj = SEQ // A_TQ

    def tok(w):
        return pl.BlockSpec((A_TQ, w), lambda b, j: (b * nj + j, 0))

    def seq(w):
        return pl.BlockSpec((SEQ, w), lambda b, j: (b, 0))

    def kind(j):
        return (j > 0).astype(jnp.int32) + (j == nj - 1).astype(jnp.int32)

    a_bias = pl.BlockSpec((None, 1, N_HEADS, A_TQ, A_TK), lambda b, j: (layer, kind(j), 0, 0, 0))
    b_bias = pl.BlockSpec((1, N_HEADS, B_TQ, B_TK), lambda b, j: (kind(j), 0, 0, 0))

    return pl.pallas_call(
        _local_kernel,
        grid=(nb, nj),
        in_specs=[pl.BlockSpec(memory_space=pltpu.SMEM),
                  tok(QK_W), seq(QK_W), seq(QK_W), a_bias,
                  tok(QK_W), seq(KG_W), seq(KG_W), b_bias],
        out_specs=[tok(BRANCH_W), tok(BRANCH_W)],
        out_shape=[jax.ShapeDtypeStruct((t, BRANCH_W), BF16)] * 2,
        compiler_params=_params("parallel", "arbitrary"),
        name="local_attn",
    )(sink, aq, ak, av, abias, bq, bk, bv, bbias)


def _dense_kernel(q_ref, k_ref, v_ref, o_ref, *, kv_heads):
    keys = slice(None)
    for sub in range(DENSE_TQ // DENSE_SUB):
        rows = slice(DENSE_SUB * sub, DENSE_SUB * (sub + 1))
        outs = []
        for h in range(N_HEADS):
            kv = h // (N_HEADS // kv_heads)
            outs.append(_attend(q_ref[rows, LANES * h:LANES * (h + 1)], k_ref[:, LANES * kv:LANES * (kv + 1)],
                                v_ref, keys, kv, 1))
        _store_heads(o_ref, outs, rows)


def _dense_attn(q, k, v, kv_heads, name):
    t = q.shape[0]
    nb = t // SEQ
    nq = SEQ // DENSE_TQ
    return pl.pallas_call(
        functools.partial(_dense_kernel, kv_heads=kv_heads),
        grid=(nb, nq),
        in_specs=[pl.BlockSpec((DENSE_TQ, QK_W), lambda b, n: (b * nq + n, 0)),
                  pl.BlockSpec((SEQ, kv_heads * LANES), lambda b, n: (b, 0)),
                  pl.BlockSpec((SEQ, kv_heads * LANES), lambda b, n: (b, 0))],
        out_specs=pl.BlockSpec((DENSE_TQ, BRANCH_W), lambda b, n: (b * nq + n, 0)),
        out_shape=jax.ShapeDtypeStruct((t, BRANCH_W), BF16),
        compiler_params=_params("parallel", "arbitrary"),
        name=name,
    )(q, k, v)


def _merge_kernel(x_ref, gpre_ref, gpost_ref, oa_ref, ob_ref, oc_ref, od_ref, wg_ref, wbr_ref, wo_ref, y_ref):
    x = x_ref[...]
    h = (_rms(x) * gpre_ref[...]).astype(BF16)
    m = None
    for c in range(D_MODEL // MERGE_CHUNK):
        cols = slice(MERGE_CHUNK * c, MERGE_CHUNK * (c + 1))
        merged = None
        for n, o_ref in enumerate((oa_ref, ob_ref, oc_ref, od_ref)):
            logits = jnp.dot(h, wg_ref[:, D_MODEL * n + MERGE_CHUNK * c:D_MODEL * n + MERGE_CHUNK * (c + 1)],
                             preferred_element_type=F32)
            y = jnp.dot(o_ref[...], wbr_ref[n, :, cols], preferred_element_type=F32)
            term = jax.nn.sigmoid(logits) * y
            merged = term if merged is None else merged + term
        part = jnp.dot(merged.astype(BF16), wo_ref[cols, :], preferred_element_type=F32)
        m = part if m is None else m + part
    y_ref[...] = x + _rms(m) * gpost_ref[...]


def _merge(layer, x2, gpre, gpost, oa, ob, oc, od, wg, wbr, wo):
    t = x2.shape[0]
    tm = TOK_TILE

    def tok(w):
        return pl.BlockSpec((tm, w), lambda i: (i, 0))

    return pl.pallas_call(
        _merge_kernel,
        grid=(t // tm,),
        in_specs=[tok(D_MODEL), _resident(gpre, layer), _resident(gpost, layer), tok(BRANCH_W), tok(BRANCH_W),
                  tok(BRANCH_W), tok(BRANCH_W), _resident(wg, layer), _resident(wbr, layer),
                  _resident(wo, layer)],
        out_specs=tok(D_MODEL),
        out_shape=jax.ShapeDtypeStruct((t, D_MODEL), F32),
        compiler_params=_params("parallel"),
        name="merge",
    )(x2, gpre, gpost, oa, ob, oc, od, wg, wbr, wo)


def _ffn_kernel(x_ref, gpre_ref, gpost_ref, wgu_ref, wdn_ref, y_ref):
    x = x_ref[...]
    h = (_rms(x) * gpre_ref[...]).astype(BF16)
    acc = None
    for c in range(D_FF // FF_CHUNK):
        lo = FF_CHUNK * c
        gate = jnp.dot(h, wgu_ref[:, lo:lo + FF_CHUNK], preferred_element_type=F32)
        up = jnp.dot(h, wgu_ref[:, D_FF + lo:D_FF + lo + FF_CHUNK], preferred_element_type=F32)
        act = (jax.nn.silu(gate) * up).astype(BF16)
        part = jnp.dot(act, wdn_ref[lo:lo + FF_CHUNK, :], preferred_element_type=F32)
        acc = part if acc is None else acc + part
    y_ref[...] = x + _rms(acc) * gpost_ref[...]


def _ffn(layer, x2, gpre, gpost, wgu, wdn):
    t = x2.shape[0]
    tm = TOK_TILE
    tok = pl.BlockSpec((tm, D_MODEL), lambda i: (i, 0))
    return pl.pallas_call(
        _ffn_kernel,
        grid=(t // tm,),
        in_specs=[tok, _resident(gpre, layer), _resident(gpost, layer), _resident(wgu, layer),
                  _resident(wdn, layer)],
        out_specs=tok,
        out_shape=jax.ShapeDtypeStruct((t, D_MODEL), F32),
        compiler_params=_params("parallel"),
        name="ffn",
    )(x2, gpre, gpost, wgu, wdn)


def _toeplitz(w, nq, nk, shift):
    lead, length = w.shape[:-1], w.shape[-1]
    period = nq + nk + length
    wp = jnp.pad(w, [(0, 0)] * len(lead) + [(0, period - length)])
    flat = jnp.broadcast_to(wp[..., None, :], lead + (nq, period)).reshape(lead + (nq * period,))
    skew = flat[..., :nq * (period - 1)].reshape(lead + (nq, period - 1))
    return skew[..., shift:shift + nk]


def _toeplitz_rows(w, nq, nk, shift):
    lead, (length, c) = w.shape[:-2], w.shape[-2:]
    period = nq + nk + length
    wp = jnp.pad(w, [(0, 0)] * len(lead) + [(0, period - length), (0, 0)])
    flat = jnp.broadcast_to(wp[..., None, :, :], lead + (nq, period, c)).reshape(lead + (nq * period, c))
    skew = flat[..., :nq * (period - 1), :].reshape(lead + (nq, period - 1, c))
    return skew[..., shift:shift + nk, :]


def _rope_tables():
    inv_freq = 1.0 / (ROPE_THETA ** (jnp.arange(ROPE_HALF, dtype=F32) * (1.0 / ROPE_HALF)))

    def group(pos):
        ang = pos.astype(F32)[:, None] * inv_freq[None, :]
        c, s = jnp.cos(ang), jnp.sin(ang)
        return jnp.concatenate([c, c], axis=-1), jnp.concatenate([-s, s], axis=-1)

    gc, gs = group(jnp.arange(GRID_W))
    rc, rs = jnp.repeat(gc, GRID_W, axis=0), jnp.repeat(gs, GRID_W, axis=0)
    cc, cs = jnp.tile(gc, (SEQ // GRID_W, 1)), jnp.tile(gs, (SEQ // GRID_W, 1))
    pc, ps = group(jnp.arange(SEQ))
    ones = jnp.ones((SEQ, 2 * ROPE_HALF), F32)
    zeros = jnp.zeros((SEQ, 2 * ROPE_HALF), F32)
    ctab = jnp.stack([jnp.concatenate([rc, cc, rc, cc], axis=-1),
                      jnp.concatenate([rs, cs, rs, cs], axis=-1)])
    dtab = jnp.stack([jnp.concatenate([ones, ones, pc, ones], axis=-1),
                      jnp.concatenate([zeros, zeros, ps, zeros], axis=-1)])
    return ctab, dtab


def _group_mean_matrix():
    i = np.arange(2 * LANES)
    m = (i[:, None] // HEAD_DIM) == (i[None, :] // HEAD_DIM)
    return jnp.asarray(m.astype(np.float32) / HEAD_DIM, dtype=BF16)


def _t5_bucket_index(rel):
    nb = T5_BUCKETS // 2
    max_exact = nb // 2
    ret = (rel > 0).astype(np.int32) * nb
    n = np.abs(rel)
    large = max_exact + (np.log(np.maximum(n, 1) / max_exact) / np.log(T5_MAX_DIST / max_exact)
                         * (nb - max_exact)).astype(np.int32)
    large = np.minimum(large, nb - 1)
    return ret + np.where(n < max_exact, n, large)


def _nbr_bias(rpb):
    depth = rpb.shape[0]
    rows = SEQ // GRID_W
    ng = rows // A_QROWS
    qc = np.arange(GRID_W)
    cs = np.clip(qc - A_WIN_C // 2, 0, GRID_W - A_WIN_C)
    col_ok = (qc[None, :] >= cs[:, None]) & (qc[None, :] < cs[:, None] + A_WIN_C)
    cols = _toeplitz(rpb.astype(F32) * LOG2E, GRID_W, GRID_W, A_WIN_C - 1)
    cols = jnp.where(col_ok, cols, NEG_INF).transpose(0, 1, 3, 2, 4)
    shifts, row_ok = [], []
    for g in (0, 1, ng - 1):
        ws = int(np.clip(A_QROWS * g - A_WIN_R // 2, 0, rows - A_KROWS))
        shifts.append(ws - A_QROWS * g + A_WIN_R - 1)
        qr = A_QROWS * g + np.arange(A_QROWS)
        rs = np.clip(qr - A_WIN_R // 2, 0, rows - A_WIN_R)
        kr = ws + np.arange(A_KROWS)
        row_ok.append((kr[None, :] >= rs[:, None]) & (kr[None, :] < rs[:, None] + A_WIN_R))
    front = -min(shifts)
    span = max(shifts) + front
    skew = _toeplitz_rows(jnp.pad(cols, ((0, 0), (0, 0), (0, 0), (front, 0), (0, 0))), A_QROWS, A_KROWS + span, 0)
    tbl = jnp.stack([skew[..., s + front:s + front + A_KROWS, :] for s in shifts], axis=1)
    ok = np.stack(row_ok)[None, :, None, None, :, :, None]
    tbl = jnp.where(ok, tbl, NEG_INF).transpose(0, 1, 2, 4, 3, 5, 6)
    return tbl.reshape(depth, 3, N_HEADS, A_TQ, A_TK)


def _win_bias(t5_table):
    nq = SEQ // B_TQ
    rel_values = np.arange(-B_WINDOW, B_WINDOW + 1)
    by_rel = t5_table.astype(F32)[_t5_bucket_index(rel_values)].T * LOG2E
    by_rel = jnp.pad(by_rel, ((0, 0), (B_WINDOW, 0)))
    full = _toeplitz(by_rel, B_TQ, B_TK + 2 * B_WINDOW, 0)
    tables = []
    for n in (0, 1, nq - 1):
        start = int(np.clip(B_TQ * n - B_WINDOW, 0, SEQ - B_TK))
        rel = (start + np.arange(B_TK))[None, :] - (B_TQ * n + np.arange(B_TQ))[:, None]
        shift = start - B_TQ * n + 2 * B_WINDOW
        tables.append(jnp.where(np.abs(rel) <= B_WINDOW, full[:, :, shift:shift + B_TK], NEG_INF))
    return jnp.stack(tables)


def _stacked_weights(w_in, c_q_norm, c_k_norm, d_w_q_up, d_w_kv_up):
    depth = w_in.shape[0]
    n_kr = 7 * BRANCH_W + D_Q_LORA + D_KV_LORA
    n_mix = n_kr + D_ROPE
    qs = HEAD_DIM ** -0.5 * LOG2E
    w_in = w_in.astype(BF16)
    w_kr = jnp.pad(w_in[:, :, n_kr:n_mix], ((0, 0), (0, 0), (D_NOPE, LANES - D_NOPE - D_ROPE)))
    w = jnp.concatenate([w_in[:, :, :n_kr], w_kr], axis=2)
    wg = w_in[:, :, n_mix:]
    cgain = jnp.concatenate([jnp.tile(c_q_norm * qs, (1, N_HEADS)), jnp.tile(c_k_norm, (1, KV_GROUPED))],
                            axis=1).astype(F32)[:, None, :]

    def pad_heads(a, dim):
        return jnp.pad(a, ((0, 0), (0, 0), (0, 0), (0, LANES - dim))).reshape(depth, a.shape[1], QK_W)

    dwq = pad_heads(d_w_q_up.reshape(depth, D_Q_LORA, N_HEADS, D_NOPE + D_ROPE), D_NOPE + D_ROPE)
    kvu = d_w_kv_up.reshape(depth, D_KV_LORA, N_HEADS, D_NOPE + D_V)
    dwkv = jnp.concatenate([pad_heads(kvu[..., :D_NOPE], D_NOPE), pad_heads(kvu[..., D_NOPE:], D_V)], axis=2)
    return w, wg, cgain, dwq.astype(BF16), dwkv.astype(BF16)


def kernel(x, w_in, a_rpb, b_sink, t5_bias, c_q_norm, c_k_norm, d_q_norm, d_kv_norm, d_w_q_up, d_w_kv_up,
           w_branch, w_out, ln_pre_mix, ln_post_mix, ln_pre_ffn, ln_post_ffn, ffn_w_gu, ffn_w_down):
    b, s, d = x.shape
    assert (s, d) == (SEQ, D_MODEL)
    depth = w_in.shape[0]
    x2 = x.reshape(b * s, d)
    ctab, dtab = _rope_tables()
    gmat = _group_mean_matrix()
    wbias = _win_bias(t5_bias)
    abias = _nbr_bias(a_rpb)
    w, wg, cgain, dwq, dwkv = _stacked_weights(w_in, c_q_norm, c_k_norm, d_w_q_up, d_w_kv_up)
    wbr, wo, wgu, wdn = (a.astype(BF16) for a in (w_branch, w_out, ffn_w_gu, ffn_w_down))
    g_pre_mix, g_post_mix, g_pre_ffn, g_post_ffn, dqn, dkvn = (
        v.astype(F32)[:, None, :] for v in (ln_pre_mix, ln_post_mix, ln_pre_ffn, ln_post_ffn, d_q_norm, d_kv_norm))
    sink = b_sink.astype(F32) * LOG2E
    for l in range(depth):
        (aq, ak, av, bq, bk, bv, cq, ck, cv, dq, dk, dv) = _proj(
            l, x2, g_pre_mix, w, gmat, cgain, ctab, dtab, dqn, dkvn, dwq, dwkv)
        oa, ob = _local_attn(l, sink[l], aq, ak, av, abias, bq, bk, bv, wbias)
        oc = _dense_attn(cq, ck, cv, KV_GROUPED, "dense_c")
        od = _dense_attn(dq, dk, dv, N_HEADS, "dense_d")
        x2 = _merge(l, x2, g_pre_mix, g_post_mix, oa, ob, oc, od, wg, wbr, wo)
        x2 = _ffn(l, x2, g_pre_ffn, g_post_ffn, wgu, wdn)
    return x2.reshape(b, s, d)
```

```python
import functools

import jax
import jax.numpy as jnp
import numpy as np
from jax import lax
from jax.experimental import pallas as pl
from jax.experimental.pallas import tpu as pltpu

D_MODEL = 1024
SEQ = 4096
GRID_W = 64
HEAD_DIM = 64
N_HEADS = 4
KV_GROUPED = 2
EPS = 1e-6
NEG_INF = -1e30
ROPE_THETA = 10000.0
A_WIN_R = 8
A_WIN_C = 16
B_WINDOW = 128
T5_BUCKETS = 32
T5_MAX_DIST = 128
D_Q_LORA = 256
D_KV_LORA = 128
D_NOPE = 64
D_ROPE = 32
D_V = 64
D_FF = 2816
BRANCH_W = 256

LANES = 128
QK_W = N_HEADS * LANES
KG_W = KV_GROUPED * LANES
ROPE_HALF = 16
LOG2E = 1.4426950408889634

A_QROWS = 4
A_KROWS = 12
A_TQ = A_QROWS * GRID_W
A_TK = A_KROWS * GRID_W
B_TQ = 256
B_TK = B_TQ + 2 * B_WINDOW
DENSE_TQ = 512
DENSE_SUB = 256
TOK_TILE = 1024
FF_CHUNK = 256
MERGE_CHUNK = 256
W_MIX = 9 * BRANCH_W
W_MIX_COL = 2
VMEM_LIMIT = 56 * 1024 * 1024

BF16 = jnp.bfloat16
F32 = jnp.float32


def _resident(arr, layer=None, cols=None):
    if layer is None:
        return pl.BlockSpec(arr.shape, lambda *_: (0,) * arr.ndim, pipeline_mode=pl.Buffered(1))
    width, col = cols if cols is not None else (arr.shape[-1], 0)
    return pl.BlockSpec((None,) + arr.shape[1:-1] + (width,), lambda *_: (layer,) + (0,) * (arr.ndim - 2) + (col,),
                        pipeline_mode=pl.Buffered(1))


def _params(*sem):
    return pltpu.CompilerParams(dimension_semantics=sem, vmem_limit_bytes=VMEM_LIMIT)


def _rms(x):
    return x * lax.rsqrt(jnp.mean(x * x, axis=-1, keepdims=True) + EPS)


def _rope128(x, cos, sin_signed):
    lane = lax.broadcasted_iota(jnp.int32, x.shape, 1)
    first = (lane % (2 * ROPE_HALF)) < ROPE_HALF
    rot = jnp.where(first, pltpu.roll(x, LANES - ROPE_HALF, 1), pltpu.roll(x, ROPE_HALF, 1))
    return x * cos + rot * sin_signed


def _with_ones(v):
    lane = lax.broadcasted_iota(jnp.int32, v.shape, 1)
    return jnp.where(lane % LANES == HEAD_DIM, 1.0, v)


def _spread_heads(o_ref, x, ones=False):
    lane = lax.broadcasted_iota(jnp.int32, (x.shape[0], LANES), 1)
    low = lane < HEAD_DIM
    fill = jnp.where(lane == HEAD_DIM, 1.0, 0.0) if ones else 0.0
    for c in range(x.shape[1] // LANES):
        pair = x[:, LANES * c:LANES * (c + 1)]
        o_ref[:, 2 * LANES * c:2 * LANES * c + LANES] = jnp.where(low, pair, fill).astype(BF16)
        o_ref[:, 2 * LANES * c + LANES:2 * LANES * (c + 1)] = (
            jnp.where(low, pltpu.roll(pair, HEAD_DIM, 1), fill).astype(BF16))


def _proj_kernel(x_ref, g_ref, w_ref, gmat_ref, cgain_ref, ctab_ref, dtab_ref,
                 dqn_ref, dkvn_ref, dwq_ref, dwkv_ref,
                 aq_ref, ak_ref, av_ref, bq_ref, bk_ref, bv_ref, cq_ref, ck_ref, cv_ref,
                 dq_ref, dk_ref, dv_ref):
    h = (_rms(x_ref[...]) * g_ref[...]).astype(BF16)
    qs = HEAD_DIM ** -0.5 * LOG2E
    a0, b0, c0, d0 = 0, 3 * BRANCH_W, 5 * BRANCH_W, 7 * BRANCH_W

    pa = jnp.dot(h, w_ref[:, a0:b0], preferred_element_type=F32)
    _spread_heads(aq_ref, pa[:, :BRANCH_W] * qs)
    _spread_heads(ak_ref, pa[:, BRANCH_W:2 * BRANCH_W])
    _spread_heads(av_ref, pa[:, 2 * BRANCH_W:], ones=True)

    pb = jnp.dot(h, w_ref[:, b0:c0], preferred_element_type=F32)
    _spread_heads(bq_ref, pb[:, :BRANCH_W] * qs)
    _spread_heads(bk_ref, pb[:, BRANCH_W:BRANCH_W + LANES])
    _spread_heads(bv_ref, pb[:, BRANCH_W + LANES:], ones=True)

    pc = jnp.dot(h, w_ref[:, c0:d0], preferred_element_type=F32)
    _spread_heads(cv_ref, pc[:, BRANCH_W + LANES:], ones=True)
    ccos = ctab_ref[0]
    csin = ctab_ref[1]

    def norm_rope(y, gmat, gain):
        sq = y * y
        hi = sq.astype(BF16)
        lo = (sq - hi.astype(F32)).astype(BF16)
        ms = jnp.dot(hi, gmat, preferred_element_type=F32) + jnp.dot(lo, gmat, preferred_element_type=F32)
        yn = y * lax.rsqrt(ms + EPS) * gain
        return jnp.concatenate([_rope128(yn[:, LANES * c:LANES * (c + 1)], ccos, csin)
                                for c in range(y.shape[1] // LANES)], axis=1)

    _spread_heads(cq_ref, norm_rope(pc[:, :BRANCH_W], gmat_ref[...], cgain_ref[:, :BRANCH_W]))
    _spread_heads(ck_ref, norm_rope(pc[:, BRANCH_W:BRANCH_W + LANES], gmat_ref[:LANES, :LANES],
                                    cgain_ref[:, BRANCH_W:]))

    pd = jnp.dot(h, w_ref[:, d0:], preferred_element_type=F32)
    dcos = dtab_ref[0]
    dsin = dtab_ref[1]
    cq = (_rms(pd[:, :D_Q_LORA]) * dqn_ref[...]).astype(BF16)
    qd = jnp.dot(cq, dwq_ref[...], preferred_element_type=F32)
    ckv = (_rms(pd[:, D_Q_LORA:D_Q_LORA + D_KV_LORA]) * dkvn_ref[...]).astype(BF16)
    kvd = jnp.dot(ckv, dwkv_ref[...], preferred_element_type=F32)
    kr = _rope128(pd[:, D_Q_LORA + D_KV_LORA:], dcos, dsin)
    scale = (D_NOPE + D_ROPE) ** -0.5 * LOG2E
    for hd in range(N_HEADS):
        sl = slice(LANES * hd, LANES * (hd + 1))
        dq_ref[:, sl] = (_rope128(qd[:, sl], dcos, dsin) * scale).astype(BF16)
        dk_ref[:, sl] = (kvd[:, sl] + kr).astype(BF16)
    dv_ref[...] = _with_ones(kvd[:, QK_W:]).astype(BF16)


def _proj(layer, x2, g, w, gmat, cgain, ctab, dtab, dqn, dkvn, dwq, dwkv):
    t = x2.shape[0]
    tm = TOK_TILE
    seq_tiles = SEQ // tm

    def tok(w):
        return pl.BlockSpec((tm, w), lambda i: (i, 0))

    def tab():
        return pl.BlockSpec((2, tm, LANES), lambda i: (0, i % seq_tiles, 0))

    out_w = (QK_W, QK_W, QK_W, QK_W, KG_W, KG_W, QK_W, KG_W, KG_W, QK_W, QK_W, QK_W)
    return pl.pallas_call(
        _proj_kernel,
        grid=(t // tm,),
        in_specs=[tok(D_MODEL), _resident(g, layer), _resident(w, layer, (W_MIX, W_MIX_COL)), _resident(gmat),
                  _resident(cgain, layer), tab(), tab(), _resident(dqn, layer), _resident(dkvn, layer),
                  _resident(dwq, layer), _resident(dwkv, layer)],
        out_specs=[tok(ow) for ow in out_w],
        out_shape=[jax.ShapeDtypeStruct((t, ow), BF16) for ow in out_w],
        compiler_params=_params("parallel"),
        name="proj",
    )(x2, g, w, gmat, cgain, ctab, dtab, dqn, dkvn, dwq, dwkv)


def _attend(q, k, v_ref, rows, kv_head, pv_heads, bias=None, sink=None):
    slab, blk = divmod(kv_head, pv_heads)
    v = v_ref[rows, pv_heads * LANES * slab:pv_heads * LANES * (slab + 1)]
    s = lax.dot_general(q, k, (((1,), (1,)), ((), ())), preferred_element_type=F32)
    if bias is not None:
        s = s + bias
    m = jnp.max(s, axis=-1, keepdims=True)
    if sink is not None:
        m = jnp.maximum(m, sink)
    pv = jnp.dot(jnp.exp2(s - m).astype(BF16), v, preferred_element_type=F32)[:, LANES * blk:LANES * (blk + 1)]
    total = pv[:, HEAD_DIM:HEAD_DIM + 1]
    if sink is not None:
        total = total + jnp.exp2(sink - m)
    return pv * (1.0 / total)


def _store_heads(o_ref, outs, rows=slice(None)):
    low = lax.broadcasted_iota(jnp.int32, outs[0].shape, 1) < HEAD_DIM
    for g in range(N_HEADS // 2):
        pair = jnp.where(low, outs[2 * g], pltpu.roll(outs[2 * g + 1], HEAD_DIM, 1))
        o_ref[rows, LANES * g:LANES * (g + 1)] = pair.astype(BF16)


def _local_kernel(sink_ref, aq_ref, ak_ref, av_ref, abias_ref, bq_ref, bk_ref, bv_ref, bbias_ref, oa_ref, ob_ref):
    j = pl.program_id(1)
    a_start = pl.multiple_of(jnp.clip(A_QROWS * j - A_WIN_R // 2, 0, GRID_W - A_KROWS) * GRID_W, A_TQ)
    a_rows = pl.ds(a_start, A_TK)
    b_start = pl.multiple_of(jnp.clip(B_TQ * j - B_WINDOW, 0, SEQ - B_TK), B_WINDOW)
    b_rows = pl.ds(b_start, B_TK)
    a_outs, b_outs = [], []
    for h in range(N_HEADS):
        sl = slice(LANES * h, LANES * (h + 1))
        kv = h // (N_HEADS // KV_GROUPED)
        a_outs.append(_attend(aq_ref[:, sl], ak_ref[a_rows, sl], av_ref, a_rows, h, 2, bias=abias_ref[0, h]))
        b_outs.append(_attend(bq_ref[:, sl], bk_ref[b_rows, LANES * kv:LANES * (kv + 1)], bv_ref, b_rows, kv, 2,
                              bias=bbias_ref[0, h], sink=sink_ref[h]))
    _store_heads(oa_ref, a_outs)
    _store_heads(ob_ref, b_outs)


def _local_attn(layer, sink, aq, ak, av, abias, bq, bk, bv, bbias):
    assert A_TQ == B_TQ
    t = aq.shape[0]
    nb = t // SEQ
    nj = SEQ // A_TQ

    def tok(w):
        return pl.BlockSpec((A_TQ, w), lambda b, j: (b * nj + j, 0))

    def seq(w):
        return pl.BlockSpec((SEQ, w), lambda b, j: (b, 0))

    def kind(j):
        return (j > 0).astype(jnp.int32) + (j == nj - 1).astype(jnp.int32)

    a_bias = pl.BlockSpec((None, 1, N_HEADS, A_TQ, A_TK), lambda b, j: (layer, kind(j), 0, 0, 0))
    b_bias = pl.BlockSpec((1, N_HEADS, B_TQ, B_TK), lambda b, j: (kind(j), 0, 0, 0))

    return pl.pallas_call(
        _local_kernel,
        grid=(nb, nj),
        in_specs=[pl.BlockSpec(memory_space=pltpu.SMEM),
                  tok(QK_W), seq(QK_W), seq(QK_W), a_bias,
                  tok(QK_W), seq(KG_W), seq(KG_W), b_bias],
        out_specs=[tok(BRANCH_W), tok(BRANCH_W)],
        out_shape=[jax.ShapeDtypeStruct((t, BRANCH_W), BF16)] * 2,
        compiler_params=_params("parallel", "arbitrary"),
        name="local_attn",
    )(sink, aq, ak, av, abias, bq, bk, bv, bbias)


def _dense_kernel(q_ref, k_ref, v_ref, o_ref, *, kv_heads):
    keys = slice(None)
    for sub in range(DENSE_TQ // DENSE_SUB):
        rows = slice(DENSE_SUB * sub, DENSE_SUB * (sub + 1))
        outs = []
        for h in range(N_HEADS):
            kv = h // (N_HEADS // kv_heads)
            outs.append(_attend(q_ref[rows, LANES * h:LANES * (h + 1)], k_ref[:, LANES * kv:LANES * (kv + 1)],
                                v_ref, keys, kv, 1))
        _store_heads(o_ref, outs, rows)


def _dense_attn(q, k, v, kv_heads, name):
    t = q.shape[0]
    nb = t // SEQ
    nq = SEQ // DENSE_TQ
    return pl.pallas_call(
        functools.partial(_dense_kernel, kv_heads=kv_heads),
        grid=(nb, nq),
        in_specs=[pl.BlockSpec((DENSE_TQ, QK_W), lambda b, n: (b * nq + n, 0)),
                  pl.BlockSpec((SEQ, kv_heads * LANES), lambda b, n: (b, 0)),
                  pl.BlockSpec((SEQ, kv_heads * LANES), lambda b, n: (b, 0))],
        out_specs=pl.BlockSpec((DENSE_TQ, BRANCH_W), lambda b, n: (b * nq + n, 0)),
        out_shape=jax.ShapeDtypeStruct((t, BRANCH_W), BF16),
        compiler_params=_params("parallel", "arbitrary"),
        name=name,
    )(q, k, v)


def _merge_kernel(x_ref, gpre_ref, gpost_ref, oa_ref, ob_ref, oc_ref, od_ref, wg_ref, wbr_ref, wo_ref, y_ref):
    x = x_ref[...]
    h = (_rms(x) * gpre_ref[...]).astype(BF16)
    m = None
    for c in range(D_MODEL // MERGE_CHUNK):
        cols = slice(MERGE_CHUNK * c, MERGE_CHUNK * (c + 1))
        merged = None
        for n, o_ref in enumerate((oa_ref, ob_ref, oc_ref, od_ref)):
            logits = jnp.dot(h, wg_ref[:, D_MODEL * n + MERGE_CHUNK * c:D_MODEL * n + MERGE_CHUNK * (c + 1)],
                             preferred_element_type=F32)
            y = jnp.dot(o_ref[...], wbr_ref[n, :, cols], preferred_element_type=F32)
            term = jax.nn.sigmoid(logits) * y
            merged = term if merged is None else merged + term
        part = jnp.dot(merged.astype(BF16), wo_ref[cols, :], preferred_element_type=F32)
        m = part if m is None else m + part
    y_ref[...] = x + _rms(m) * gpost_ref[...]


def _merge(layer, x2, gpre, gpost, oa, ob, oc, od, wg, wbr, wo):
    t = x2.shape[0]
    tm = TOK_TILE

    def tok(w):
        return pl.BlockSpec((tm, w), lambda i: (i, 0))

    return pl.pallas_call(
        _merge_kernel,
        grid=(t // tm,),
        in_specs=[tok(D_MODEL), _resident(gpre, layer), _resident(gpost, layer), tok(BRANCH_W), tok(BRANCH_W),
                  tok(BRANCH_W), tok(BRANCH_W), _resident(wg, layer, (N_HEADS * D_MODEL, 0)), _resident(wbr, layer),
                  _resident(wo, layer)],
        out_specs=tok(D_MODEL),
        out_shape=jax.ShapeDtypeStruct((t, D_MODEL), F32),
        compiler_params=_params("parallel"),
        name="merge",
    )(x2, gpre, gpost, oa, ob, oc, od, wg, wbr, wo)


def _ffn_kernel(x_ref, gpre_ref, gpost_ref, wgu_ref, wdn_ref, y_ref):
    x = x_ref[...]
    h = (_rms(x) * gpre_ref[...]).astype(BF16)
    acc = None
    for c in range(D_FF // FF_CHUNK):
        lo = FF_CHUNK * c
        gate = jnp.dot(h, wgu_ref[:, lo:lo + FF_CHUNK], preferred_element_type=F32)
        up = jnp.dot(h, wgu_ref[:, D_FF + lo:D_FF + lo + FF_CHUNK], preferred_element_type=F32)
        act = (jax.nn.silu(gate) * up).astype(BF16)
        part = jnp.dot(act, wdn_ref[lo:lo + FF_CHUNK, :], preferred_element_type=F32)
        acc = part if acc is None else acc + part
    y_ref[...] = x + _rms(acc) * gpost_ref[...]


def _ffn(layer, x2, gpre, gpost, wgu, wdn):
    t = x2.shape[0]
    tm = TOK_TILE
    tok = pl.BlockSpec((tm, D_MODEL), lambda i: (i, 0))
    return pl.pallas_call(
        _ffn_kernel,
        grid=(t // tm,),
        in_specs=[tok, _resident(gpre, layer), _resident(gpost, layer), _resident(wgu, layer),
                  _resident(wdn, layer)],
        out_specs=tok,
        out_shape=jax.ShapeDtypeStruct((t, D_MODEL), F32),
        compiler_params=_params("parallel"),
        name="ffn",
    )(x2, gpre, gpost, wgu, wdn)


def _rope_tables():
    inv_freq = 1.0 / (ROPE_THETA ** (jnp.arange(ROPE_HALF, dtype=F32) * (1.0 / ROPE_HALF)))

    def group(pos):
        ang = pos.astype(F32)[:, None] * inv_freq[None, :]
        c, s = jnp.cos(ang), jnp.sin(ang)
        return jnp.concatenate([c, c], axis=-1), jnp.concatenate([-s, s], axis=-1)

    gc, gs = group(jnp.arange(GRID_W))
    rc, rs = jnp.repeat(gc, GRID_W, axis=0), jnp.repeat(gs, GRID_W, axis=0)
    cc, cs = jnp.tile(gc, (SEQ // GRID_W, 1)), jnp.tile(gs, (SEQ // GRID_W, 1))
    pc, ps = group(jnp.arange(SEQ))
    ones = jnp.ones((SEQ, 2 * ROPE_HALF), F32)
    zeros = jnp.zeros((SEQ, 2 * ROPE_HALF), F32)
    ctab = jnp.stack([jnp.concatenate([rc, cc, rc, cc], axis=-1),
                      jnp.concatenate([rs, cs, rs, cs], axis=-1)])
    dtab = jnp.stack([jnp.concatenate([ones, ones, pc, ones], axis=-1),
                      jnp.concatenate([zeros, zeros, ps, zeros], axis=-1)])
    return ctab, dtab


def _group_mean_matrix():
    i = np.arange(2 * LANES)
    m = (i[:, None] // HEAD_DIM) == (i[None, :] // HEAD_DIM)
    return jnp.asarray(m.astype(np.float32) / HEAD_DIM, dtype=BF16)


def _t5_bucket_index(rel):
    nb = T5_BUCKETS // 2
    max_exact = nb // 2
    ret = (rel > 0).astype(np.int32) * nb
    n = np.abs(rel)
    large = max_exact + (np.log(np.maximum(n, 1) / max_exact) / np.log(T5_MAX_DIST / max_exact)
                         * (nb - max_exact)).astype(np.int32)
    large = np.minimum(large, nb - 1)
    return ret + np.where(n < max_exact, n, large)


def _nbr_group_kinds():
    rows = SEQ // GRID_W
    ng = rows // A_QROWS
    kinds = []
    for g in (0, 1, ng - 1):
        ws = int(np.clip(A_QROWS * g - A_WIN_R // 2, 0, rows - A_KROWS))
        qr = A_QROWS * g + np.arange(A_QROWS)
        rs = np.clip(qr - A_WIN_R // 2, 0, rows - A_WIN_R)
        kr = ws + np.arange(A_KROWS)
        ok = (kr[None, :] >= rs[:, None]) & (kr[None, :] < rs[:, None] + A_WIN_R)
        kinds.append((ws - A_QROWS * g + A_WIN_R - 1, ok))
    return kinds


def _nbr_bias_kernel(r_ref, o_ref):
    lane = lax.broadcasted_iota(jnp.int32, (GRID_W, LANES), 1)
    qc = lax.broadcasted_iota(jnp.int32, (GRID_W, LANES), 0)
    kc = lane % GRID_W
    cs = jnp.clip(qc - A_WIN_C // 2, 0, GRID_W - A_WIN_C)
    col_ok = (kc >= cs) & (kc < cs + A_WIN_C)
    left = lane < GRID_W
    pairs = [jnp.where(col_ok, pltpu.roll(jnp.broadcast_to(r_ref[j:j + 1, :], (GRID_W, LANES)),
                                          LANES - (A_WIN_C - 1), 1, stride=1, stride_axis=0), NEG_INF)
             for j in range(2 * A_WIN_R)]
    masked = jnp.full((GRID_W, LANES), NEG_INF, F32)
    for t, (s, row_ok) in enumerate(_nbr_group_kinds()):
        for qr in range(A_QROWS):
            for p in range(A_KROWS // 2):
                j = 2 * p - qr + s + 1
                ok_a, ok_b = bool(row_ok[qr][2 * p]), bool(row_ok[qr][2 * p + 1])
                if not (ok_a or ok_b):
                    blk = masked
                elif ok_a and ok_b:
                    blk = pairs[j]
                else:
                    blk = jnp.where(left if ok_a else ~left, pairs[j], NEG_INF)
                o_ref[t, GRID_W * qr:GRID_W * (qr + 1), LANES * p:LANES * (p + 1)] = blk


def _nbr_bias(rpb):
    depth = rpb.shape[0]
    r = jnp.pad(rpb.astype(F32) * LOG2E, ((0, 0), (0, 0), (1, 1), (0, GRID_W - (2 * A_WIN_C - 1))))
    r = jnp.concatenate([r[:, :, :-1], r[:, :, 1:]], axis=-1)
    return pl.pallas_call(
        _nbr_bias_kernel,
        grid=(depth, N_HEADS),
        in_specs=[pl.BlockSpec((None, None, 2 * A_WIN_R, LANES), lambda l, h: (l, h, 0, 0))],
        out_specs=pl.BlockSpec((None, 3, None, A_TQ, A_TK), lambda l, h: (l, 0, h, 0, 0)),
        out_shape=jax.ShapeDtypeStruct((depth, 3, N_HEADS, A_TQ, A_TK), F32),
        compiler_params=_params("parallel", "parallel"),
        name="nbr_bias",
    )(r)


def _win_bias(t5_table):
    rel_values = np.arange(-B_WINDOW, B_WINDOW + 1)
    by_rel = t5_table.astype(F32)[_t5_bucket_index(rel_values)].T * LOG2E
    by_rel = jnp.pad(by_rel, ((0, 0), (LANES, LANES - 1)))[:, None, :]
    return pl.pallas_call(
        _win_bias_kernel,
        grid=(N_HEADS,),
        in_specs=[pl.BlockSpec((None, 1, 4 * LANES), lambda h: (h, 0, 0))],
        out_specs=pl.BlockSpec((3, None, B_TQ, B_TK), lambda h: (0, h, 0, 0)),
        out_shape=jax.ShapeDtypeStruct((3, N_HEADS, B_TQ, B_TK), F32),
        compiler_params=_params("parallel"),
        name="win_bias",
    )(by_rel)


def _win_bias_kernel(u_ref, o_ref):
    assert B_WINDOW == LANES
    nq = SEQ // B_TQ
    qr = lax.broadcasted_iota(jnp.int32, (LANES, LANES), 0)
    lane = lax.broadcasted_iota(jnp.int32, (LANES, LANES), 1)
    tiles = {}
    for d in (-LANES, 0, LANES):
        src = jnp.broadcast_to(u_ref[:, LANES + d:3 * LANES + d], (LANES, 2 * LANES))
        skew = pltpu.roll(src, 0, 1, stride=1, stride_axis=0)[:, LANES:]
        tiles[d] = jnp.where(jnp.abs(lane - qr + d) <= B_WINDOW, skew, NEG_INF)
    masked = jnp.full((LANES, LANES), NEG_INF, F32)
    for t, n in enumerate((0, 1, nq - 1)):
        start = int(np.clip(B_TQ * n - B_WINDOW, 0, SEQ - B_TK))
        for qb in range(B_TQ // LANES):
            for kb in range(B_TK // LANES):
                d = (start + LANES * kb) - (B_TQ * n + LANES * qb)
                o_ref[t, LANES * qb:LANES * (qb + 1), LANES * kb:LANES * (kb + 1)] = tiles.get(d, masked)


def _stacked_weights(w_in, c_q_norm, c_k_norm, d_w_q_up, d_w_kv_up):
    depth = w_in.shape[0]
    n_kr = 7 * BRANCH_W + D_Q_LORA + D_KV_LORA
    n_mix = n_kr + D_ROPE
    qs = HEAD_DIM ** -0.5 * LOG2E
    gate_w = w_in.shape[2] - n_mix
    zeros = jnp.zeros((depth, D_MODEL, W_MIX_COL * W_MIX - gate_w), BF16)
    w_kr = jnp.pad(w_in[:, :, n_kr:n_mix], ((0, 0), (0, 0), (D_NOPE, LANES - D_NOPE - D_ROPE)))
    w = jnp.concatenate([w_in[:, :, n_mix:].astype(BF16), zeros, w_in[:, :, :n_kr].astype(BF16),
                         w_kr.astype(BF16)], axis=2)
    cgain = jnp.concatenate([jnp.tile(c_q_norm * qs, (1, N_HEADS)), jnp.tile(c_k_norm, (1, KV_GROUPED))],
                            axis=1).astype(F32)[:, None, :]

    def pad_heads(a, dim):
        return jnp.pad(a, ((0, 0), (0, 0), (0, 0), (0, LANES - dim))).reshape(depth, a.shape[1], QK_W)

    dwq = pad_heads(d_w_q_up.reshape(depth, D_Q_LORA, N_HEADS, D_NOPE + D_ROPE), D_NOPE + D_ROPE)
    kvu = d_w_kv_up.reshape(depth, D_KV_LORA, N_HEADS, D_NOPE + D_V)
    dwkv = jnp.concatenate([pad_heads(kvu[..., :D_NOPE], D_NOPE), pad_heads(kvu[..., D_NOPE:], D_V)], axis=2)
    return w, cgain, dwq.astype(BF16), dwkv.astype(BF16)


def kernel(x, w_in, a_rpb, b_sink, t5_bias, c_q_norm, c_k_norm, d_q_norm, d_kv_norm, d_w_q_up, d_w_kv_up,
           w_branch, w_out, ln_pre_mix, ln_post_mix, ln_pre_ffn, ln_post_ffn, ffn_w_gu, ffn_w_down):
    b, s, d = x.shape
    assert (s, d) == (SEQ, D_MODEL)
    depth = w_in.shape[0]
    x2 = x.reshape(b * s, d)
    ctab, dtab = _rope_tables()
    gmat = _group_mean_matrix()
    wbias = _win_bias(t5_bias)
    abias = _nbr_bias(a_rpb)
    w, cgain, dwq, dwkv = _stacked_weights(w_in, c_q_norm, c_k_norm, d_w_q_up, d_w_kv_up)
    wbr, wo, wgu, wdn = (a.astype(BF16) for a in (w_branch, w_out, ffn_w_gu, ffn_w_down))
    g_pre_mix, g_post_mix, g_pre_ffn, g_post_ffn, dqn, dkvn = (
        v.astype(F32)[:, None, :] for v in (ln_pre_mix, ln_post_mix, ln_pre_ffn, ln_post_ffn, d_q_norm, d_kv_norm))
    sink = b_sink.astype(F32) * LOG2E
    for l in range(depth):
        (aq, ak, av, bq, bk, bv, cq, ck, cv, dq, dk, dv) = _proj(
            l, x2, g_pre_mix, w, gmat, cgain, ctab, dtab, dqn, dkvn, dwq, dwkv)
        oa, ob = _local_attn(l, sink[l], aq, ak, av, abias, bq, bk, bv, wbias)
        oc = _dense_attn(cq, ck, cv, KV_GROUPED, "dense_c")
        od = _dense_attn(dq, dk, dv, N_HEADS, "dense_d")
        x2 = _merge(l, x2, g_pre_mix, g_post_mix, oa, ob, oc, od, w, wbr, wo)
        x2 = _ffn(l, x2, g_pre_ffn, g_post_ffn, wgu, wdn)
    return x2.reshape(b, s, d)
```

```python
import functools

import jax
import jax.numpy as jnp
import numpy as np
from jax import lax
from jax.experimental import pallas as pl
from jax.experimental.pallas import tpu as pltpu

D_MODEL = 1024
SEQ = 4096
GRID_W = 64
HEAD_DIM = 64
N_HEADS = 4
KV_GROUPED = 2
EPS = 1e-6
NEG_INF = -1e30
ROPE_THETA = 10000.0
A_WIN_R = 8
A_WIN_C = 16
B_WINDOW = 128
T5_BUCKETS = 32
T5_MAX_DIST = 128
D_Q_LORA = 256
D_KV_LORA = 128
D_NOPE = 64
D_ROPE = 32
D_V = 64
D_FF = 2816
BRANCH_W = 256

LANES = 128
QK_W = N_HEADS * LANES
KG_W = KV_GROUPED * LANES
ROPE_HALF = 16
LOG2E = 1.4426950408889634

A_QROWS = 4
A_KROWS = 12
A_TQ = A_QROWS * GRID_W
A_TK = A_KROWS * GRID_W
B_TQ = 256
LOCAL_SUBS = 2
B_TK = B_TQ + 2 * B_WINDOW
DENSE_TQ = 512
DENSE_SUB = 256
TOK_TILE = 1024
FF_CHUNK = 256
MERGE_CHUNK = 256
W_MIX = 9 * BRANCH_W
W_MIX_COL = 2
VMEM_LIMIT = 56 * 1024 * 1024

BF16 = jnp.bfloat16
F32 = jnp.float32


def _resident(arr, layer=None, cols=None):
    if layer is None:
        return pl.BlockSpec(arr.shape, lambda *_: (0,) * arr.ndim, pipeline_mode=pl.Buffered(1))
    width, col = cols if cols is not None else (arr.shape[-1], 0)
    return pl.BlockSpec((None,) + arr.shape[1:-1] + (width,), lambda *_: (layer,) + (0,) * (arr.ndim - 2) + (col,),
                        pipeline_mode=pl.Buffered(1))


def _params(*sem):
    return pltpu.CompilerParams(dimension_semantics=sem, vmem_limit_bytes=VMEM_LIMIT)


def _rms(x):
    return x * lax.rsqrt(jnp.mean(x * x, axis=-1, keepdims=True) + EPS)


def _rope128(x, cos, sin_signed):
    lane = lax.broadcasted_iota(jnp.int32, x.shape, 1)
    first = (lane % (2 * ROPE_HALF)) < ROPE_HALF
    rot = jnp.where(first, pltpu.roll(x, LANES - ROPE_HALF, 1), pltpu.roll(x, ROPE_HALF, 1))
    return x * cos + rot * sin_signed


def _with_ones(v):
    lane = lax.broadcasted_iota(jnp.int32, v.shape, 1)
    return jnp.where(lane % LANES == HEAD_DIM, 1.0, v)


def _spread_heads(o_ref, x, ones=False):
    lane = lax.broadcasted_iota(jnp.int32, (x.shape[0], LANES), 1)
    low = lane < HEAD_DIM
    fill = jnp.where(lane == HEAD_DIM, 1.0, 0.0) if ones else 0.0
    for c in range(x.shape[1] // LANES):
        pair = x[:, LANES * c:LANES * (c + 1)]
        o_ref[:, 2 * LANES * c:2 * LANES * c + LANES] = jnp.where(low, pair, fill).astype(BF16)
        o_ref[:, 2 * LANES * c + LANES:2 * LANES * (c + 1)] = (
            jnp.where(low, pltpu.roll(pair, HEAD_DIM, 1), fill).astype(BF16))


def _proj_kernel(x_ref, g_ref, w_ref, gmat_ref, cgain_ref, ctab_ref, dtab_ref,
                 dqn_ref, dkvn_ref, dwq_ref, dwkv_ref,
                 aq_ref, ak_ref, av_ref, bq_ref, bk_ref, bv_ref, cq_ref, ck_ref, cv_ref,
                 dq_ref, dk_ref, dv_ref):
    h = (_rms(x_ref[...]) * g_ref[...]).astype(BF16)
    qs = HEAD_DIM ** -0.5 * LOG2E
    a0, b0, c0, d0 = 0, 3 * BRANCH_W, 5 * BRANCH_W, 7 * BRANCH_W

    pa = jnp.dot(h, w_ref[:, a0:b0], preferred_element_type=F32)
    _spread_heads(aq_ref, pa[:, :BRANCH_W] * qs)
    _spread_heads(ak_ref, pa[:, BRANCH_W:2 * BRANCH_W])
    _spread_heads(av_ref, pa[:, 2 * BRANCH_W:], ones=True)

    pb = jnp.dot(h, w_ref[:, b0:c0], preferred_element_type=F32)
    _spread_heads(bq_ref, pb[:, :BRANCH_W] * qs)
    _spread_heads(bk_ref, pb[:, BRANCH_W:BRANCH_W + LANES])
    _spread_heads(bv_ref, pb[:, BRANCH_W + LANES:], ones=True)

    pc = jnp.dot(h, w_ref[:, c0:d0], preferred_element_type=F32)
    _spread_heads(cv_ref, pc[:, BRANCH_W + LANES:], ones=True)
    ccos = ctab_ref[0]
    csin = ctab_ref[1]

    def norm_rope(y, gmat, gain):
        sq = y * y
        hi = sq.astype(BF16)
        lo = (sq - hi.astype(F32)).astype(BF16)
        ms = jnp.dot(hi, gmat, preferred_element_type=F32) + jnp.dot(lo, gmat, preferred_element_type=F32)
        yn = y * lax.rsqrt(ms + EPS) * gain
        return jnp.concatenate([_rope128(yn[:, LANES * c:LANES * (c + 1)], ccos, csin)
                                for c in range(y.shape[1] // LANES)], axis=1)

    _spread_heads(cq_ref, norm_rope(pc[:, :BRANCH_W], gmat_ref[...], cgain_ref[:, :BRANCH_W]))
    _spread_heads(ck_ref, norm_rope(pc[:, BRANCH_W:BRANCH_W + LANES], gmat_ref[:LANES, :LANES],
                                    cgain_ref[:, BRANCH_W:]))

    pd = jnp.dot(h, w_ref[:, d0:], preferred_element_type=F32)
    dcos = dtab_ref[0]
    dsin = dtab_ref[1]
    cq = (_rms(pd[:, :D_Q_LORA]) * dqn_ref[...]).astype(BF16)
    qd = jnp.dot(cq, dwq_ref[...], preferred_element_type=F32)
    ckv = (_rms(pd[:, D_Q_LORA:D_Q_LORA + D_KV_LORA]) * dkvn_ref[...]).astype(BF16)
    kvd = jnp.dot(ckv, dwkv_ref[...], preferred_element_type=F32)
    kr = _rope128(pd[:, D_Q_LORA + D_KV_LORA:], dcos, dsin)
    scale = (D_NOPE + D_ROPE) ** -0.5 * LOG2E
    for hd in range(N_HEADS):
        sl = slice(LANES * hd, LANES * (hd + 1))
        dq_ref[:, sl] = (_rope128(qd[:, sl], dcos, dsin) * scale).astype(BF16)
        dk_ref[:, sl] = (kvd[:, sl] + kr).astype(BF16)
    dv_ref[...] = _with_ones(kvd[:, QK_W:]).astype(BF16)


def _proj(layer, x2, g, w, gmat, cgain, ctab, dtab, dqn, dkvn, dwq, dwkv):
    t = x2.shape[0]
    tm = TOK_TILE
    seq_tiles = SEQ // tm

    def tok(w):
        return pl.BlockSpec((tm, w), lambda i: (i, 0))

    def tab():
        return pl.BlockSpec((2, tm, LANES), lambda i: (0, i % seq_tiles, 0))

    out_w = (QK_W, QK_W, QK_W, QK_W, KG_W, KG_W, QK_W, KG_W, KG_W, QK_W, QK_W, QK_W)
    return pl.pallas_call(
        _proj_kernel,
        grid=(t // tm,),
        in_specs=[tok(D_MODEL), _resident(g, layer), _resident(w, layer, (W_MIX, W_MIX_COL)), _resident(gmat),
                  _resident(cgain, layer), tab(), tab(), _resident(dqn, layer), _resident(dkvn, layer),
                  _resident(dwq, layer), _resident(dwkv, layer)],
        out_specs=[tok(ow) for ow in out_w],
        out_shape=[jax.ShapeDtypeStruct((t, ow), BF16) for ow in out_w],
        compiler_params=_params("parallel"),
        name="proj",
    )(x2, g, w, gmat, cgain, ctab, dtab, dqn, dkvn, dwq, dwkv)


def _attend(q, k, v_ref, rows, kv_head, pv_heads, bias=None, sink=None):
    slab, blk = divmod(kv_head, pv_heads)
    v = v_ref[rows, pv_heads * LANES * slab:pv_heads * LANES * (slab + 1)]
    s = lax.dot_general(q, k, (((1,), (1,)), ((), ())), preferred_element_type=F32)
    if bias is not None:
        s = s + bias
    m = jnp.max(s, axis=-1, keepdims=True)
    if sink is not None:
        m = jnp.maximum(m, sink)
    pv = jnp.dot(jnp.exp2(s - m).astype(BF16), v, preferred_element_type=F32)[:, LANES * blk:LANES * (blk + 1)]
    total = pv[:, HEAD_DIM:HEAD_DIM + 1]
    if sink is not None:
        total = total + jnp.exp2(sink - m)
    return pv * (1.0 / total)


def _store_heads(o_ref, outs, rows=slice(None)):
    low = lax.broadcasted_iota(jnp.int32, outs[0].shape, 1) < HEAD_DIM
    for g in range(N_HEADS // 2):
        pair = jnp.where(low, outs[2 * g], pltpu.roll(outs[2 * g + 1], HEAD_DIM, 1))
        o_ref[rows, LANES * g:LANES * (g + 1)] = pair.astype(BF16)


def _local_kernel(sink_ref, aq_ref, ak_ref, av_ref, abias0_ref, abias1_ref, bq_ref, bk_ref, bv_ref,
                  bbias0_ref, bbias1_ref, oa_ref, ob_ref):
    for sub, (abias_ref, bbias_ref) in enumerate(((abias0_ref, bbias0_ref), (abias1_ref, bbias1_ref))):
        j = LOCAL_SUBS * pl.program_id(1) + sub
        q_rows = slice(A_TQ * sub, A_TQ * (sub + 1))
        a_start = pl.multiple_of(jnp.clip(A_QROWS * j - A_WIN_R // 2, 0, GRID_W - A_KROWS) * GRID_W, A_TQ)
        a_rows = pl.ds(a_start, A_TK)
        b_start = pl.multiple_of(jnp.clip(B_TQ * j - B_WINDOW, 0, SEQ - B_TK), B_WINDOW)
        b_rows = pl.ds(b_start, B_TK)
        a_outs, b_outs = [], []
        for h in range(N_HEADS):
            sl = slice(LANES * h, LANES * (h + 1))
            kv = h // (N_HEADS // KV_GROUPED)
            a_outs.append(_attend(aq_ref[q_rows, sl], ak_ref[a_rows, sl], av_ref, a_rows, h, 2,
                                  bias=abias_ref[0, h]))
            b_outs.append(_attend(bq_ref[q_rows, sl], bk_ref[b_rows, LANES * kv:LANES * (kv + 1)], bv_ref, b_rows,
                                  kv, 2, bias=bbias_ref[0, h], sink=sink_ref[h]))
        _store_heads(oa_ref, a_outs, q_rows)
        _store_heads(ob_ref, b_outs, q_rows)


def _local_attn(layer, sink, aq, ak, av, abias, bq, bk, bv, bbias):
    assert A_TQ == B_TQ and LOCAL_SUBS == 2
    t = aq.shape[0]
    nb = t // SEQ
    nblk = SEQ // A_TQ
    nj = nblk // LOCAL_SUBS

    def tok(w):
        return pl.BlockSpec((LOCAL_SUBS * A_TQ, w), lambda b, j: (b * nj + j, 0))

    def seq(w):
        return pl.BlockSpec((SEQ, w), lambda b, j: (b, 0))

    def kind(j, sub):
        blk = LOCAL_SUBS * j + sub
        return (blk > 0).astype(jnp.int32) + (blk == nblk - 1).astype(jnp.int32)

    def a_bias(sub):
        return pl.BlockSpec((None, 1, N_HEADS, A_TQ, A_TK), lambda b, j: (layer, kind(j, sub), 0, 0, 0),
                            pipeline_mode=pl.Buffered(1))

    def b_bias(sub):
        return pl.BlockSpec((1, N_HEADS, B_TQ, B_TK), lambda b, j: (kind(j, sub), 0, 0, 0),
                            pipeline_mode=pl.Buffered(1))

    return pl.pallas_call(
        _local_kernel,
        grid=(nb, nj),
        in_specs=[pl.BlockSpec(memory_space=pltpu.SMEM),
                  tok(QK_W), seq(QK_W), seq(QK_W), a_bias(0), a_bias(1),
                  tok(QK_W), seq(KG_W), seq(KG_W), b_bias(0), b_bias(1)],
        out_specs=[tok(BRANCH_W), tok(BRANCH_W)],
        out_shape=[jax.ShapeDtypeStruct((t, BRANCH_W), BF16)] * 2,
        compiler_params=_params("parallel", "arbitrary"),
        name="local_attn",
    )(sink, aq, ak, av, abias, abias, bq, bk, bv, bbias, bbias)


def _dense_kernel(q_ref, k_ref, v_ref, o_ref, *, kv_heads):
    keys = slice(None)
    for sub in range(DENSE_TQ // DENSE_SUB):
        rows = slice(DENSE_SUB * sub, DENSE_SUB * (sub + 1))
        outs = []
        for h in range(N_HEADS):
            kv = h // (N_HEADS // kv_heads)
            outs.append(_attend(q_ref[rows, LANES * h:LANES * (h + 1)], k_ref[:, LANES * kv:LANES * (kv + 1)],
                                v_ref, keys, kv, 1))
        _store_heads(o_ref, outs, rows)


def _dense_attn(q, k, v, kv_heads, name):
    t = q.shape[0]
    nb = t // SEQ
    nq = SEQ // DENSE_TQ
    return pl.pallas_call(
        functools.partial(_dense_kernel, kv_heads=kv_heads),
        grid=(nb, nq),
        in_specs=[pl.BlockSpec((DENSE_TQ, QK_W), lambda b, n: (b * nq + n, 0)),
                  pl.BlockSpec((SEQ, kv_heads * LANES), lambda b, n: (b, 0)),
                  pl.BlockSpec((SEQ, kv_heads * LANES), lambda b, n: (b, 0))],
        out_specs=pl.BlockSpec((DENSE_TQ, BRANCH_W), lambda b, n: (b * nq + n, 0)),
        out_shape=jax.ShapeDtypeStruct((t, BRANCH_W), BF16),
        compiler_params=_params("parallel", "arbitrary"),
        name=name,
    )(q, k, v)


def _merge_kernel(x_ref, gpre_ref, gpost_ref, oa_ref, ob_ref, oc_ref, od_ref, wg_ref, wbr_ref, wo_ref, y_ref):
    x = x_ref[...]
    h = (_rms(x) * gpre_ref[...]).astype(BF16)
    m = None
    for c in range(D_MODEL // MERGE_CHUNK):
        cols = slice(MERGE_CHUNK * c, MERGE_CHUNK * (c + 1))
        merged = None
        for n, o_ref in enumerate((oa_ref, ob_ref, oc_ref, od_ref)):
            logits = jnp.dot(h, wg_ref[:, D_MODEL * n + MERGE_CHUNK * c:D_MODEL * n + MERGE_CHUNK * (c + 1)],
                             preferred_element_type=F32)
            y = jnp.dot(o_ref[...], wbr_ref[n, :, cols], preferred_element_type=F32)
            term = jax.nn.sigmoid(logits) * y
            merged = term if merged is None else merged + term
        part = jnp.dot(merged.astype(BF16), wo_ref[cols, :], preferred_element_type=F32)
        m = part if m is None else m + part
    y_ref[...] = x + _rms(m) * gpost_ref[...]


def _merge(layer, x2, gpre, gpost, oa, ob, oc, od, wg, wbr, wo):
    t = x2.shape[0]
    tm = TOK_TILE

    def tok(w):
        return pl.BlockSpec((tm, w), lambda i: (i, 0))

    return pl.pallas_call(
        _merge_kernel,
        grid=(t // tm,),
        in_specs=[tok(D_MODEL), _resident(gpre, layer), _resident(gpost, layer), tok(BRANCH_W), tok(BRANCH_W),
                  tok(BRANCH_W), tok(BRANCH_W), _resident(wg, layer, (N_HEADS * D_MODEL, 0)), _resident(wbr, layer),
                  _resident(wo, layer)],
        out_specs=tok(D_MODEL),
        out_shape=jax.ShapeDtypeStruct((t, D_MODEL), F32),
        compiler_params=_params("parallel"),
        name="merge",
    )(x2, gpre, gpost, oa, ob, oc, od, wg, wbr, wo)


def _ffn_kernel(x_ref, gpre_ref, gpost_ref, wgu_ref, wdn_ref, y_ref):
    x = x_ref[...]
    h = (_rms(x) * gpre_ref[...]).astype(BF16)
    acc = None
    for c in range(D_FF // FF_CHUNK):
        lo = FF_CHUNK * c
        gate = jnp.dot(h, wgu_ref[:, lo:lo + FF_CHUNK], preferred_element_type=F32)
        up = jnp.dot(h, wgu_ref[:, D_FF + lo:D_FF + lo + FF_CHUNK], preferred_element_type=F32)
        act = (jax.nn.silu(gate) * up).astype(BF16)
        part = jnp.dot(act, wdn_ref[lo:lo + FF_CHUNK, :], preferred_element_type=F32)
        acc = part if acc is None else acc + part
    y_ref[...] = x + _rms(acc) * gpost_ref[...]


def _ffn(layer, x2, gpre, gpost, wgu, wdn):
    t = x2.shape[0]
    tm = TOK_TILE
    tok = pl.BlockSpec((tm, D_MODEL), lambda i: (i, 0))
    return pl.pallas_call(
        _ffn_kernel,
        grid=(t // tm,),
        in_specs=[tok, _resident(gpre, layer), _resident(gpost, layer), _resident(wgu, layer),
                  _resident(wdn, layer)],
        out_specs=tok,
        out_shape=jax.ShapeDtypeStruct((t, D_MODEL), F32),
        compiler_params=_params("parallel"),
        name="ffn",
    )(x2, gpre, gpost, wgu, wdn)


def _rope_tables():
    inv_freq = 1.0 / (ROPE_THETA ** (jnp.arange(ROPE_HALF, dtype=F32) * (1.0 / ROPE_HALF)))

    def group(pos):
        ang = pos.astype(F32)[:, None] * inv_freq[None, :]
        c, s = jnp.cos(ang), jnp.sin(ang)
        return jnp.concatenate([c, c], axis=-1), jnp.concatenate([-s, s], axis=-1)

    gc, gs = group(jnp.arange(GRID_W))
    rc, rs = jnp.repeat(gc, GRID_W, axis=0), jnp.repeat(gs, GRID_W, axis=0)
    cc, cs = jnp.tile(gc, (SEQ // GRID_W, 1)), jnp.tile(gs, (SEQ // GRID_W, 1))
    pc, ps = group(jnp.arange(SEQ))
    ones = jnp.ones((SEQ, 2 * ROPE_HALF), F32)
    zeros = jnp.zeros((SEQ, 2 * ROPE_HALF), F32)
    ctab = jnp.stack([jnp.concatenate([rc, cc, rc, cc], axis=-1),
                      jnp.concatenate([rs, cs, rs, cs], axis=-1)])
    dtab = jnp.stack([jnp.concatenate([ones, ones, pc, ones], axis=-1),
                      jnp.concatenate([zeros, zeros, ps, zeros], axis=-1)])
    return ctab, dtab


def _group_mean_matrix():
    i = np.arange(2 * LANES)
    m = (i[:, None] // HEAD_DIM) == (i[None, :] // HEAD_DIM)
    return jnp.asarray(m.astype(np.float32) / HEAD_DIM, dtype=BF16)


def _t5_bucket_index(rel):
    nb = T5_BUCKETS // 2
    max_exact = nb // 2
    ret = (rel > 0).astype(np.int32) * nb
    n = np.abs(rel)
    large = max_exact + (np.log(np.maximum(n, 1) / max_exact) / np.log(T5_MAX_DIST / max_exact)
                         * (nb - max_exact)).astype(np.int32)
    large = np.minimum(large, nb - 1)
    return ret + np.where(n < max_exact, n, large)


def _nbr_group_kinds():
    rows = SEQ // GRID_W
    ng = rows // A_QROWS
    kinds = []
    for g in (0, 1, ng - 1):
        ws = int(np.clip(A_QROWS * g - A_WIN_R // 2, 0, rows - A_KROWS))
        qr = A_QROWS * g + np.arange(A_QROWS)
        rs = np.clip(qr - A_WIN_R // 2, 0, rows - A_WIN_R)
        kr = ws + np.arange(A_KROWS)
        ok = (kr[None, :] >= rs[:, None]) & (kr[None, :] < rs[:, None] + A_WIN_R)
        kinds.append((ws - A_QROWS * g + A_WIN_R - 1, ok))
    return kinds


def _nbr_bias_kernel(r_ref, o_ref):
    lane = lax.broadcasted_iota(jnp.int32, (GRID_W, LANES), 1)
    qc = lax.broadcasted_iota(jnp.int32, (GRID_W, LANES), 0)
    kc = lane % GRID_W
    cs = jnp.clip(qc - A_WIN_C // 2, 0, GRID_W - A_WIN_C)
    col_ok = (kc >= cs) & (kc < cs + A_WIN_C)
    left = lane < GRID_W
    pairs = [jnp.where(col_ok, pltpu.roll(jnp.broadcast_to(r_ref[j:j + 1, :], (GRID_W, LANES)),
                                          LANES - (A_WIN_C - 1), 1, stride=1, stride_axis=0), NEG_INF)
             for j in range(2 * A_WIN_R)]
    masked = jnp.full((GRID_W, LANES), NEG_INF, F32)
    for t, (s, row_ok) in enumerate(_nbr_group_kinds()):
        for qr in range(A_QROWS):
            for p in range(A_KROWS // 2):
                j = 2 * p - qr + s + 1
                ok_a, ok_b = bool(row_ok[qr][2 * p]), bool(row_ok[qr][2 * p + 1])
                if not (ok_a or ok_b):
                    blk = masked
                elif ok_a and ok_b:
                    blk = pairs[j]
                else:
                    blk = jnp.where(left if ok_a else ~left, pairs[j], NEG_INF)
                o_ref[t, GRID_W * qr:GRID_W * (qr + 1), LANES * p:LANES * (p + 1)] = blk


def _nbr_bias(rpb):
    depth = rpb.shape[0]
    r = jnp.pad(rpb.astype(F32) * LOG2E, ((0, 0), (0, 0), (1, 1), (0, GRID_W - (2 * A_WIN_C - 1))))
    r = jnp.concatenate([r[:, :, :-1], r[:, :, 1:]], axis=-1)
    return pl.pallas_call(
        _nbr_bias_kernel,
        grid=(depth, N_HEADS),
        in_specs=[pl.BlockSpec((None, None, 2 * A_WIN_R, LANES), lambda l, h: (l, h, 0, 0))],
        out_specs=pl.BlockSpec((None, 3, None, A_TQ, A_TK), lambda l, h: (l, 0, h, 0, 0)),
        out_shape=jax.ShapeDtypeStruct((depth, 3, N_HEADS, A_TQ, A_TK), F32),
        compiler_params=_params("parallel", "parallel"),
        name="nbr_bias",
    )(r)


def _win_bias(t5_table):
    rel_values = np.arange(-B_WINDOW, B_WINDOW + 1)
    by_rel = t5_table.astype(F32)[_t5_bucket_index(rel_values)].T * LOG2E
    by_rel = jnp.pad(by_rel, ((0, 0), (LANES, LANES - 1)))[:, None, :]
    return pl.pallas_call(
        _win_bias_kernel,
        grid=(N_HEADS,),
        in_specs=[pl.BlockSpec((None, 1, 4 * LANES), lambda h: (h, 0, 0))],
        out_specs=pl.BlockSpec((3, None, B_TQ, B_TK), lambda h: (0, h, 0, 0)),
        out_shape=jax.ShapeDtypeStruct((3, N_HEADS, B_TQ, B_TK), F32),
        compiler_params=_params("parallel"),
        name="win_bias",
    )(by_rel)


def _win_bias_kernel(u_ref, o_ref):
    assert B_WINDOW == LANES
    nq = SEQ // B_TQ
    qr = lax.broadcasted_iota(jnp.int32, (LANES, LANES), 0)
    lane = lax.broadcasted_iota(jnp.int32, (LANES, LANES), 1)
    tiles = {}
    for d in (-LANES, 0, LANES):
        src = jnp.broadcast_to(u_ref[:, LANES + d:3 * LANES + d], (LANES, 2 * LANES))
        skew = pltpu.roll(src, 0, 1, stride=1, stride_axis=0)[:, LANES:]
        tiles[d] = jnp.where(jnp.abs(lane - qr + d) <= B_WINDOW, skew, NEG_INF)
    masked = jnp.full((LANES, LANES), NEG_INF, F32)
    for t, n in enumerate((0, 1, nq - 1)):
        start = int(np.clip(B_TQ * n - B_WINDOW, 0, SEQ - B_TK))
        for qb in range(B_TQ // LANES):
            for kb in range(B_TK // LANES):
                d = (start + LANES * kb) - (B_TQ * n + LANES * qb)
                o_ref[t, LANES * qb:LANES * (qb + 1), LANES * kb:LANES * (kb + 1)] = tiles.get(d, masked)


def _pack_w_kernel(w_ref, o_ref):
    n_kr = W_MIX - LANES
    gates = N_HEADS * D_MODEL
    o_ref[:, :gates] = w_ref[:, n_kr + D_ROPE:].astype(BF16)
    o_ref[:, gates:W_MIX_COL * W_MIX] = jnp.zeros((o_ref.shape[0], W_MIX_COL * W_MIX - gates), BF16)
    o_ref[:, W_MIX_COL * W_MIX:W_MIX_COL * W_MIX + n_kr] = w_ref[:, :n_kr].astype(BF16)
    tail = w_ref[:, n_kr:n_kr + LANES]
    lane = lax.broadcasted_iota(jnp.int32, tail.shape, 1)
    o_ref[:, W_MIX_COL * W_MIX + n_kr:] = jnp.where((lane >= D_NOPE) & (lane < D_NOPE + D_ROPE),
                                                    pltpu.roll(tail, D_NOPE, 1), 0.0).astype(BF16)


def _pack_w_in(w_in):
    depth, rows, width = w_in.shape
    assert width == W_MIX - LANES + D_ROPE + N_HEADS * D_MODEL
    tr = 256
    return pl.pallas_call(
        _pack_w_kernel,
        grid=(depth, rows // tr),
        in_specs=[pl.BlockSpec((None, tr, width), lambda l, i: (l, i, 0))],
        out_specs=pl.BlockSpec((None, tr, (W_MIX_COL + 1) * W_MIX), lambda l, i: (l, i, 0)),
        out_shape=jax.ShapeDtypeStruct((depth, rows, (W_MIX_COL + 1) * W_MIX), BF16),
        compiler_params=_params("parallel", "parallel"),
        name="pack_w",
    )(w_in)


def _stacked_weights(w_in, c_q_norm, c_k_norm, d_w_q_up, d_w_kv_up):
    depth = w_in.shape[0]
    qs = HEAD_DIM ** -0.5 * LOG2E
    w = _pack_w_in(w_in)
    cgain = jnp.concatenate([jnp.tile(c_q_norm * qs, (1, N_HEADS)), jnp.tile(c_k_norm, (1, KV_GROUPED))],
                            axis=1).astype(F32)[:, None, :]

    def pad_heads(a, dim):
        return jnp.pad(a, ((0, 0), (0, 0), (0, 0), (0, LANES - dim))).reshape(depth, a.shape[1], QK_W)

    dwq = pad_heads(d_w_q_up.reshape(depth, D_Q_LORA, N_HEADS, D_NOPE + D_ROPE), D_NOPE + D_ROPE)
    kvu = d_w_kv_up.reshape(depth, D_KV_LORA, N_HEADS, D_NOPE + D_V)
    dwkv = jnp.concatenate([pad_heads(kvu[..., :D_NOPE], D_NOPE), pad_heads(kvu[..., D_NOPE:], D_V)], axis=2)
    return w, cgain, dwq.astype(BF16), dwkv.astype(BF16)


def kernel(x, w_in, a_rpb, b_sink, t5_bias, c_q_norm, c_k_norm, d_q_norm, d_kv_norm, d_w_q_up, d_w_kv_up,
           w_branch, w_out, ln_pre_mix, ln_post_mix, ln_pre_ffn, ln_post_ffn, ffn_w_gu, ffn_w_down):
    b, s, d = x.shape
    assert (s, d) == (SEQ, D_MODEL)
    depth = w_in.shape[0]
    x2 = x.reshape(b * s, d)
    ctab, dtab = _rope_tables()
    gmat = _group_mean_matrix()
    wbias = _win_bias(t5_bias)
    abias = _nbr_bias(a_rpb)
    w, cgain, dwq, dwkv = _stacked_weights(w_in, c_q_norm, c_k_norm, d_w_q_up, d_w_kv_up)
    wbr, wo, wgu, wdn = (a.astype(BF16) for a in (w_branch, w_out, ffn_w_gu, ffn_w_down))
    g_pre_mix, g_post_mix, g_pre_ffn, g_post_ffn, dqn, dkvn = (
        v.astype(F32)[:, None, :] for v in (ln_pre_mix, ln_post_mix, ln_pre_ffn, ln_post_ffn, d_q_norm, d_kv_norm))
    sink = b_sink.astype(F32) * LOG2E
    for l in range(depth):
        (aq, ak, av, bq, bk, bv, cq, ck, cv, dq, dk, dv) = _proj(
            l, x2, g_pre_mix, w, gmat, cgain, ctab, dtab, dqn, dkvn, dwq, dwkv)
        oa, ob = _local_attn(l, sink[l], aq, ak, av, abias, bq, bk, bv, wbias)
        oc = _dense_attn(cq, ck, cv, KV_GROUPED, "dense_c")
        od = _dense_attn(dq, dk, dv, N_HEADS, "dense_d")
        x2 = _merge(l, x2, g_pre_mix, g_post_mix, oa, ob, oc, od, w, wbr, wo)
        x2 = _ffn(l, x2, g_pre_ffn, g_post_ffn, wgu, wdn)
    return x2.reshape(b, s, d)
```

```python
import functools

import jax
import jax.numpy as jnp
import numpy as np
from jax import lax
from jax.experimental import pallas as pl
from jax.experimental.pallas import tpu as pltpu

D_MODEL = 1024
SEQ = 4096
GRID_W = 64
HEAD_DIM = 64
N_HEADS = 4
KV_GROUPED = 2
EPS = 1e-6
NEG_INF = -1e30
ROPE_THETA = 10000.0
A_WIN_R = 8
A_WIN_C = 16
B_WINDOW = 128
T5_BUCKETS = 32
T5_MAX_DIST = 128
D_Q_LORA = 256
D_KV_LORA = 128
D_NOPE = 64
D_ROPE = 32
D_V = 64
D_FF = 2816
BRANCH_W = 256

LANES = 128
QK_W = N_HEADS * LANES
KG_W = KV_GROUPED * LANES
ROPE_HALF = 16
LOG2E = 1.4426950408889634

A_QROWS = 4
A_KROWS = 12
A_TQ = A_QROWS * GRID_W
A_TK = A_KROWS * GRID_W
B_TQ = 256
B_TK = B_TQ + 2 * B_WINDOW
DENSE_TQ = 512
DENSE_SUB = 256
TOK_TILE = 1024
FF_CHUNK = 256
MERGE_CHUNK = 256
PROJ_SUBS = 2
TOK_SUBS = 2
W_MIX = 9 * BRANCH_W
W_MIX_COL = 2
VMEM_LIMIT = 56 * 1024 * 1024

BF16 = jnp.bfloat16
F32 = jnp.float32


def _resident(arr, layer=None, cols=None):
    if layer is None:
        return pl.BlockSpec(arr.shape, lambda *_: (0,) * arr.ndim, pipeline_mode=pl.Buffered(1))
    width, col = cols if cols is not None else (arr.shape[-1], 0)
    return pl.BlockSpec((None,) + arr.shape[1:-1] + (width,), lambda *_: (layer,) + (0,) * (arr.ndim - 2) + (col,),
                        pipeline_mode=pl.Buffered(1))


def _params(*sem):
    return pltpu.CompilerParams(dimension_semantics=sem, vmem_limit_bytes=VMEM_LIMIT)


def _rms(x):
    return x * lax.rsqrt(jnp.mean(x * x, axis=-1, keepdims=True) + EPS)


def _rope128(x, cos, sin_signed):
    lane = lax.broadcasted_iota(jnp.int32, x.shape, 1)
    first = (lane % (2 * ROPE_HALF)) < ROPE_HALF
    rot = jnp.where(first, pltpu.roll(x, LANES - ROPE_HALF, 1), pltpu.roll(x, ROPE_HALF, 1))
    return x * cos + rot * sin_signed


def _with_ones(v):
    lane = lax.broadcasted_iota(jnp.int32, v.shape, 1)
    return jnp.where(lane % LANES == HEAD_DIM, 1.0, v)


def _spread_heads(o_ref, x, ones=False):
    lane = lax.broadcasted_iota(jnp.int32, (x.shape[0], LANES), 1)
    low = lane < HEAD_DIM
    fill = jnp.where(lane == HEAD_DIM, 1.0, 0.0) if ones else 0.0
    for c in range(x.shape[1] // LANES):
        pair = x[:, LANES * c:LANES * (c + 1)]
        o_ref[:, 2 * LANES * c:2 * LANES * c + LANES] = jnp.where(low, pair, fill).astype(BF16)
        o_ref[:, 2 * LANES * c + LANES:2 * LANES * (c + 1)] = (
            jnp.where(low, pltpu.roll(pair, HEAD_DIM, 1), fill).astype(BF16))


def _proj_kernel(x_ref, g_ref, w_ref, gmat_ref, cgain_ref, ctab_ref, dtab_ref,
                 dqn_ref, dkvn_ref, dwq_ref, dwkv_ref, *out_refs):
    sub_rows = x_ref.shape[0] // PROJ_SUBS
    for sub in range(PROJ_SUBS):
        rows = slice(sub_rows * sub, sub_rows * (sub + 1))
        _proj_rows(x_ref.at[rows], g_ref, w_ref, gmat_ref, cgain_ref, ctab_ref.at[:, rows], dtab_ref.at[:, rows],
                   dqn_ref, dkvn_ref, dwq_ref, dwkv_ref, *(o.at[rows] for o in out_refs))


def _proj_rows(x_ref, g_ref, w_ref, gmat_ref, cgain_ref, ctab_ref, dtab_ref,
               dqn_ref, dkvn_ref, dwq_ref, dwkv_ref,
               aq_ref, ak_ref, av_ref, bq_ref, bk_ref, bv_ref, cq_ref, ck_ref, cv_ref,
               dq_ref, dk_ref, dv_ref):
    h = (_rms(x_ref[...]) * g_ref[...]).astype(BF16)
    qs = HEAD_DIM ** -0.5 * LOG2E
    a0, b0, c0, d0 = 0, 3 * BRANCH_W, 5 * BRANCH_W, 7 * BRANCH_W

    pa = jnp.dot(h, w_ref[:, a0:b0], preferred_element_type=F32)
    _spread_heads(aq_ref, pa[:, :BRANCH_W] * qs)
    _spread_heads(ak_ref, pa[:, BRANCH_W:2 * BRANCH_W])
    _spread_heads(av_ref, pa[:, 2 * BRANCH_W:], ones=True)

    pb = jnp.dot(h, w_ref[:, b0:c0], preferred_element_type=F32)
    _spread_heads(bq_ref, pb[:, :BRANCH_W] * qs)
    _spread_heads(bk_ref, pb[:, BRANCH_W:BRANCH_W + LANES])
    _spread_heads(bv_ref, pb[:, BRANCH_W + LANES:], ones=True)

    pc = jnp.dot(h, w_ref[:, c0:d0], preferred_element_type=F32)
    _spread_heads(cv_ref, pc[:, BRANCH_W + LANES:], ones=True)
    ccos = ctab_ref[0]
    csin = ctab_ref[1]

    def norm_rope(y, gmat, gain):
        sq = y * y
        hi = sq.astype(BF16)
        lo = (sq - hi.astype(F32)).astype(BF16)
        ms = jnp.dot(hi, gmat, preferred_element_type=F32) + jnp.dot(lo, gmat, preferred_element_type=F32)
        yn = y * lax.rsqrt(ms + EPS) * gain
        return jnp.concatenate([_rope128(yn[:, LANES * c:LANES * (c + 1)], ccos, csin)
                                for c in range(y.shape[1] // LANES)], axis=1)

    _spread_heads(cq_ref, norm_rope(pc[:, :BRANCH_W], gmat_ref[...], cgain_ref[:, :BRANCH_W]))
    _spread_heads(ck_ref, norm_rope(pc[:, BRANCH_W:BRANCH_W + LANES], gmat_ref[:LANES, :LANES],
                                    cgain_ref[:, BRANCH_W:]))

    pd = jnp.dot(h, w_ref[:, d0:], preferred_element_type=F32)
    dcos = dtab_ref[0]
    dsin = dtab_ref[1]
    cq = (_rms(pd[:, :D_Q_LORA]) * dqn_ref[...]).astype(BF16)
    qd = jnp.dot(cq, dwq_ref[...], preferred_element_type=F32)
    ckv = (_rms(pd[:, D_Q_LORA:D_Q_LORA + D_KV_LORA]) * dkvn_ref[...]).astype(BF16)
    kvd = jnp.dot(ckv, dwkv_ref[...], preferred_element_type=F32)
    kr = _rope128(pd[:, D_Q_LORA + D_KV_LORA:], dcos, dsin)
    scale = (D_NOPE + D_ROPE) ** -0.5 * LOG2E
    for hd in range(N_HEADS):
        sl = slice(LANES * hd, LANES * (hd + 1))
        dq_ref[:, sl] = (_rope128(qd[:, sl], dcos, dsin) * scale).astype(BF16)
        dk_ref[:, sl] = (kvd[:, sl] + kr).astype(BF16)
    dv_ref[...] = _with_ones(kvd[:, QK_W:]).astype(BF16)


def _proj(layer, x2, g, w, gmat, cgain, ctab, dtab, dqn, dkvn, dwq, dwkv):
    t = x2.shape[0]
    tm = TOK_TILE
    seq_tiles = SEQ // tm

    def tok(w):
        return pl.BlockSpec((tm, w), lambda i: (i, 0))

    def tab():
        return pl.BlockSpec((2, tm, LANES), lambda i: (0, i % seq_tiles, 0))

    out_w = (QK_W, QK_W, QK_W, QK_W, KG_W, KG_W, QK_W, KG_W, KG_W, QK_W, QK_W, QK_W)
    return pl.pallas_call(
        _proj_kernel,
        grid=(t // tm,),
        in_specs=[tok(D_MODEL), _resident(g, layer), _resident(w, layer, (W_MIX, W_MIX_COL)), _resident(gmat),
                  _resident(cgain, layer), tab(), tab(), _resident(dqn, layer), _resident(dkvn, layer),
                  _resident(dwq, layer), _resident(dwkv, layer)],
        out_specs=[tok(ow) for ow in out_w],
        out_shape=[jax.ShapeDtypeStruct((t, ow), BF16) for ow in out_w],
        compiler_params=_params("parallel"),
        name="proj",
    )(x2, g, w, gmat, cgain, ctab, dtab, dqn, dkvn, dwq, dwkv)


def _attend(q, k, v_ref, rows, kv_head, pv_heads, bias=None, sink=None):
    slab, blk = divmod(kv_head, pv_heads)
    v = v_ref[rows, pv_heads * LANES * slab:pv_heads * LANES * (slab + 1)]
    s = lax.dot_general(q, k, (((1,), (1,)), ((), ())), preferred_element_type=F32)
    if bias is not None:
        s = s + bias
    m = jnp.max(s, axis=-1, keepdims=True)
    if sink is not None:
        m = jnp.maximum(m, sink)
    pv = jnp.dot(jnp.exp2(s - m).astype(BF16), v, preferred_element_type=F32)[:, LANES * blk:LANES * (blk + 1)]
    total = pv[:, HEAD_DIM:HEAD_DIM + 1]
    if sink is not None:
        total = total + jnp.exp2(sink - m)
    return pv * (1.0 / total)


def _store_heads(o_ref, outs, rows=slice(None)):
    low = lax.broadcasted_iota(jnp.int32, outs[0].shape, 1) < HEAD_DIM
    for g in range(N_HEADS // 2):
        pair = jnp.where(low, outs[2 * g], pltpu.roll(outs[2 * g + 1], HEAD_DIM, 1))
        o_ref[rows, LANES * g:LANES * (g + 1)] = pair.astype(BF16)


def _local_kernel(sink_ref, aq_ref, ak_ref, av_ref, abias_ref, bq_ref, bk_ref, bv_ref, bbias_ref, oa_ref, ob_ref):
    j = pl.program_id(1)
    a_start = pl.multiple_of(jnp.clip(A_QROWS * j - A_WIN_R // 2, 0, GRID_W - A_KROWS) * GRID_W, A_TQ)
    a_rows = pl.ds(a_start, A_TK)
    b_start = pl.multiple_of(jnp.clip(B_TQ * j - B_WINDOW, 0, SEQ - B_TK), B_WINDOW)
    b_rows = pl.ds(b_start, B_TK)
    a_outs, b_outs = [], []
    for h in range(N_HEADS):
        sl = slice(LANES * h, LANES * (h + 1))
        kv = h // (N_HEADS // KV_GROUPED)
        a_outs.append(_attend(aq_ref[:, sl], ak_ref[a_rows, sl], av_ref, a_rows, h, 2, bias=abias_ref[0, h]))
        b_outs.append(_attend(bq_ref[:, sl], bk_ref[b_rows, LANES * kv:LANES * (kv + 1)], bv_ref, b_rows, kv, 2,
                              bias=bbias_ref[0, h], sink=sink_ref[h]))
    _store_heads(oa_ref, a_outs)
    _store_heads(ob_ref, b_outs)


def _local_attn(layer, sink, aq, ak, av, abias, bq, bk, bv, bbias):
    assert A_TQ == B_TQ
    t = aq.shape[0]
    nb = t // SEQ
    nj = SEQ // A_TQ

    def tok(w):
        return pl.BlockSpec((A_TQ, w), lambda b, j: (b * nj + j, 0))

    def seq(w):
        return pl.BlockSpec((SEQ, w), lambda b, j: (b, 0))

    def kind(j):
        return (j > 0).astype(jnp.int32) + (j == nj - 1).astype(jnp.int32)

    a_bias = pl.BlockSpec((None, 1, N_HEADS, A_TQ, A_TK), lambda b, j: (layer, kind(j), 0, 0, 0))
    b_bias = pl.BlockSpec((1, N_HEADS, B_TQ, B_TK), lambda b, j: (kind(j), 0, 0, 0))

    return pl.pallas_call(
        _local_kernel,
        grid=(nb, nj),
        in_specs=[pl.BlockSpec(memory_space=pltpu.SMEM),
                  tok(QK_W), seq(QK_W), seq(QK_W), a_bias,
                  tok(QK_W), seq(KG_W), seq(KG_W), b_bias],
        out_specs=[tok(BRANCH_W), tok(BRANCH_W)],
        out_shape=[jax.ShapeDtypeStruct((t, BRANCH_W), BF16)] * 2,
        compiler_params=_params("parallel", "arbitrary"),
        name="local_attn",
    )(sink, aq, ak, av, abias, bq, bk, bv, bbias)


def _dense_kernel(q_ref, k_ref, v_ref, o_ref, *, kv_heads):
    keys = slice(None)
    for sub in range(DENSE_TQ // DENSE_SUB):
        rows = slice(DENSE_SUB * sub, DENSE_SUB * (sub + 1))
        outs = []
        for h in range(N_HEADS):
            kv = h // (N_HEADS // kv_heads)
            outs.append(_attend(q_ref[rows, LANES * h:LANES * (h + 1)], k_ref[:, LANES * kv:LANES * (kv + 1)],
                                v_ref, keys, kv, 1))
        _store_heads(o_ref, outs, rows)


def _dense_attn(q, k, v, kv_heads, name):
    t = q.shape[0]
    nb = t // SEQ
    nq = SEQ // DENSE_TQ
    return pl.pallas_call(
        functools.partial(_dense_kernel, kv_heads=kv_heads),
        grid=(nb, nq),
        in_specs=[pl.BlockSpec((DENSE_TQ, QK_W), lambda b, n: (b * nq + n, 0)),
                  pl.BlockSpec((SEQ, kv_heads * LANES), lambda b, n: (b, 0)),
                  pl.BlockSpec((SEQ, kv_heads * LANES), lambda b, n: (b, 0))],
        out_specs=pl.BlockSpec((DENSE_TQ, BRANCH_W), lambda b, n: (b * nq + n, 0)),
        out_shape=jax.ShapeDtypeStruct((t, BRANCH_W), BF16),
        compiler_params=_params("parallel", "arbitrary"),
        name=name,
    )(q, k, v)


def _merge_kernel(x_ref, gpre_ref, gpost_ref, oa_ref, ob_ref, oc_ref, od_ref, wg_ref, wbr_ref, wo_ref, y_ref):
    x = x_ref[...]
    h = (_rms(x) * gpre_ref[...]).astype(BF16)
    m = None
    for c in range(D_MODEL // MERGE_CHUNK):
        cols = slice(MERGE_CHUNK * c, MERGE_CHUNK * (c + 1))
        merged = None
        for n, o_ref in enumerate((oa_ref, ob_ref, oc_ref, od_ref)):
            logits = jnp.dot(h, wg_ref[:, D_MODEL * n + MERGE_CHUNK * c:D_MODEL * n + MERGE_CHUNK * (c + 1)],
                             preferred_element_type=F32)
            y = jnp.dot(o_ref[...], wbr_ref[n, :, cols], preferred_element_type=F32)
            term = jax.nn.sigmoid(logits) * y
            merged = term if merged is None else merged + term
        part = jnp.dot(merged.astype(BF16), wo_ref[cols, :], preferred_element_type=F32)
        m = part if m is None else m + part
    y_ref[...] = x + _rms(m) * gpost_ref[...]


def _merge(layer, x2, gpre, gpost, oa, ob, oc, od, wg, wbr, wo):
    t = x2.shape[0]
    tm = TOK_TILE

    def tok(w):
        return pl.BlockSpec((tm, w), lambda i: (i, 0))

    return pl.pallas_call(
        _merge_kernel,
        grid=(t // tm,),
        in_specs=[tok(D_MODEL), _resident(gpre, layer), _resident(gpost, layer), tok(BRANCH_W), tok(BRANCH_W),
                  tok(BRANCH_W), tok(BRANCH_W), _resident(wg, layer, (N_HEADS * D_MODEL, 0)), _resident(wbr, layer),
                  _resident(wo, layer)],
        out_specs=tok(D_MODEL),
        out_shape=jax.ShapeDtypeStruct((t, D_MODEL), F32),
        compiler_params=_params("parallel"),
        name="merge",
    )(x2, gpre, gpost, oa, ob, oc, od, wg, wbr, wo)


def _ffn_kernel(x_ref, gpre_ref, gpost_ref, wgu_ref, wdn_ref, y_ref):
    sub_rows = x_ref.shape[0] // TOK_SUBS
    for sub in range(TOK_SUBS):
        rows = slice(sub_rows * sub, sub_rows * (sub + 1))
        _ffn_rows(x_ref.at[rows], gpre_ref, gpost_ref, wgu_ref, wdn_ref, y_ref.at[rows])


def _ffn_rows(x_ref, gpre_ref, gpost_ref, wgu_ref, wdn_ref, y_ref):
    x = x_ref[...]
    h = (_rms(x) * gpre_ref[...]).astype(BF16)
    acc = None
    for c in range(D_FF // FF_CHUNK):
        lo = FF_CHUNK * c
        gate = jnp.dot(h, wgu_ref[:, lo:lo + FF_CHUNK], preferred_element_type=F32)
        up = jnp.dot(h, wgu_ref[:, D_FF + lo:D_FF + lo + FF_CHUNK], preferred_element_type=F32)
        act = (jax.nn.silu(gate) * up).astype(BF16)
        part = jnp.dot(act, wdn_ref[lo:lo + FF_CHUNK, :], preferred_element_type=F32)
        acc = part if acc is None else acc + part
    y_ref[...] = x + _rms(acc) * gpost_ref[...]


def _ffn(layer, x2, gpre, gpost, wgu, wdn):
    t = x2.shape[0]
    tm = TOK_TILE
    tok = pl.BlockSpec((tm, D_MODEL), lambda i: (i, 0))
    return pl.pallas_call(
        _ffn_kernel,
        grid=(t // tm,),
        in_specs=[tok, _resident(gpre, layer), _resident(gpost, layer), _resident(wgu, layer),
                  _resident(wdn, layer)],
        out_specs=tok,
        out_shape=jax.ShapeDtypeStruct((t, D_MODEL), F32),
        compiler_params=_params("parallel"),
        name="ffn",
    )(x2, gpre, gpost, wgu, wdn)


def _rope_tables():
    inv_freq = 1.0 / (ROPE_THETA ** (jnp.arange(ROPE_HALF, dtype=F32) * (1.0 / ROPE_HALF)))

    def group(pos):
        ang = pos.astype(F32)[:, None] * inv_freq[None, :]
        c, s = jnp.cos(ang), jnp.sin(ang)
        return jnp.concatenate([c, c], axis=-1), jnp.concatenate([-s, s], axis=-1)

    gc, gs = group(jnp.arange(GRID_W))
    rc, rs = jnp.repeat(gc, GRID_W, axis=0), jnp.repeat(gs, GRID_W, axis=0)
    cc, cs = jnp.tile(gc, (SEQ // GRID_W, 1)), jnp.tile(gs, (SEQ // GRID_W, 1))
    pc, ps = group(jnp.arange(SEQ))
    ones = jnp.ones((SEQ, 2 * ROPE_HALF), F32)
    zeros = jnp.zeros((SEQ, 2 * ROPE_HALF), F32)
    ctab = jnp.stack([jnp.concatenate([rc, cc, rc, cc], axis=-1),
                      jnp.concatenate([rs, cs, rs, cs], axis=-1)])
    dtab = jnp.stack([jnp.concatenate([ones, ones, pc, ones], axis=-1),
                      jnp.concatenate([zeros, zeros, ps, zeros], axis=-1)])
    return ctab, dtab


def _group_mean_matrix():
    i = np.arange(2 * LANES)
    m = (i[:, None] // HEAD_DIM) == (i[None, :] // HEAD_DIM)
    return jnp.asarray(m.astype(np.float32) / HEAD_DIM, dtype=BF16)


def _t5_bucket_index(rel):
    nb = T5_BUCKETS // 2
    max_exact = nb // 2
    ret = (rel > 0).astype(np.int32) * nb
    n = np.abs(rel)
    large = max_exact + (np.log(np.maximum(n, 1) / max_exact) / np.log(T5_MAX_DIST / max_exact)
                         * (nb - max_exact)).astype(np.int32)
    large = np.minimum(large, nb - 1)
    return ret + np.where(n < max_exact, n, large)


def _nbr_group_kinds():
    rows = SEQ // GRID_W
    ng = rows // A_QROWS
    kinds = []
    for g in (0, 1, ng - 1):
        ws = int(np.clip(A_QROWS * g - A_WIN_R // 2, 0, rows - A_KROWS))
        qr = A_QROWS * g + np.arange(A_QROWS)
        rs = np.clip(qr - A_WIN_R // 2, 0, rows - A_WIN_R)
        kr = ws + np.arange(A_KROWS)
        ok = (kr[None, :] >= rs[:, None]) & (kr[None, :] < rs[:, None] + A_WIN_R)
        kinds.append((ws - A_QROWS * g + A_WIN_R - 1, ok))
    return kinds


def _nbr_bias_kernel(r_ref, o_ref):
    lane = lax.broadcasted_iota(jnp.int32, (GRID_W, LANES), 1)
    qc = lax.broadcasted_iota(jnp.int32, (GRID_W, LANES), 0)
    kc = lane % GRID_W
    cs = jnp.clip(qc - A_WIN_C // 2, 0, GRID_W - A_WIN_C)
    col_ok = (kc >= cs) & (kc < cs + A_WIN_C)
    left = lane < GRID_W
    pairs = [jnp.where(col_ok, pltpu.roll(jnp.broadcast_to(r_ref[j:j + 1, :], (GRID_W, LANES)),
                                          LANES - (A_WIN_C - 1), 1, stride=1, stride_axis=0), NEG_INF)
             for j in range(2 * A_WIN_R)]
    masked = jnp.full((GRID_W, LANES), NEG_INF, F32)
    for t, (s, row_ok) in enumerate(_nbr_group_kinds()):
        for qr in range(A_QROWS):
            for p in range(A_KROWS // 2):
                j = 2 * p - qr + s + 1
                ok_a, ok_b = bool(row_ok[qr][2 * p]), bool(row_ok[qr][2 * p + 1])
                if not (ok_a or ok_b):
                    blk = masked
                elif ok_a and ok_b:
                    blk = pairs[j]
                else:
                    blk = jnp.where(left if ok_a else ~left, pairs[j], NEG_INF)
                o_ref[t, GRID_W * qr:GRID_W * (qr + 1), LANES * p:LANES * (p + 1)] = blk


def _nbr_bias(rpb):
    depth = rpb.shape[0]
    r = jnp.pad(rpb.astype(F32) * LOG2E, ((0, 0), (0, 0), (1, 1), (0, GRID_W - (2 * A_WIN_C - 1))))
    r = jnp.concatenate([r[:, :, :-1], r[:, :, 1:]], axis=-1)
    return pl.pallas_call(
        _nbr_bias_kernel,
        grid=(depth, N_HEADS),
        in_specs=[pl.BlockSpec((None, None, 2 * A_WIN_R, LANES), lambda l, h: (l, h, 0, 0))],
        out_specs=pl.BlockSpec((None, 3, None, A_TQ, A_TK), lambda l, h: (l, 0, h, 0, 0)),
        out_shape=jax.ShapeDtypeStruct((depth, 3, N_HEADS, A_TQ, A_TK), F32),
        compiler_params=_params("parallel", "parallel"),
        name="nbr_bias",
    )(r)


def _win_bias(t5_table):
    rel_values = np.arange(-B_WINDOW, B_WINDOW + 1)
    by_rel = t5_table.astype(F32)[_t5_bucket_index(rel_values)].T * LOG2E
    by_rel = jnp.pad(by_rel, ((0, 0), (LANES, LANES - 1)))[:, None, :]
    return pl.pallas_call(
        _win_bias_kernel,
        grid=(N_HEADS,),
        in_specs=[pl.BlockSpec((None, 1, 4 * LANES), lambda h: (h, 0, 0))],
        out_specs=pl.BlockSpec((3, None, B_TQ, B_TK), lambda h: (0, h, 0, 0)),
        out_shape=jax.ShapeDtypeStruct((3, N_HEADS, B_TQ, B_TK), F32),
        compiler_params=_params("parallel"),
        name="win_bias",
    )(by_rel)


def _win_bias_kernel(u_ref, o_ref):
    assert B_WINDOW == LANES
    nq = SEQ // B_TQ
    qr = lax.broadcasted_iota(jnp.int32, (LANES, LANES), 0)
    lane = lax.broadcasted_iota(jnp.int32, (LANES, LANES), 1)
    tiles = {}
    for d in (-LANES, 0, LANES):
        src = jnp.broadcast_to(u_ref[:, LANES + d:3 * LANES + d], (LANES, 2 * LANES))
        skew = pltpu.roll(src, 0, 1, stride=1, stride_axis=0)[:, LANES:]
        tiles[d] = jnp.where(jnp.abs(lane - qr + d) <= B_WINDOW, skew, NEG_INF)
    masked = jnp.full((LANES, LANES), NEG_INF, F32)
    for t, n in enumerate((0, 1, nq - 1)):
        start = int(np.clip(B_TQ * n - B_WINDOW, 0, SEQ - B_TK))
        for qb in range(B_TQ // LANES):
            for kb in range(B_TK // LANES):
                d = (start + LANES * kb) - (B_TQ * n + LANES * qb)
                o_ref[t, LANES * qb:LANES * (qb + 1), LANES * kb:LANES * (kb + 1)] = tiles.get(d, masked)


def _pack_w_kernel(w_ref, o_ref):
    n_kr = W_MIX - LANES
    gates = N_HEADS * D_MODEL
    o_ref[:, :gates] = w_ref[:, n_kr + D_ROPE:].astype(BF16)
    o_ref[:, gates:W_MIX_COL * W_MIX] = jnp.zeros((o_ref.shape[0], W_MIX_COL * W_MIX - gates), BF16)
    o_ref[:, W_MIX_COL * W_MIX:W_MIX_COL * W_MIX + n_kr] = w_ref[:, :n_kr].astype(BF16)
    tail = w_ref[:, n_kr:n_kr + LANES]
    lane = lax.broadcasted_iota(jnp.int32, tail.shape, 1)
    o_ref[:, W_MIX_COL * W_MIX + n_kr:] = jnp.where((lane >= D_NOPE) & (lane < D_NOPE + D_ROPE),
                                                    pltpu.roll(tail, D_NOPE, 1), 0.0).astype(BF16)


def _pack_w_in(w_in):
    depth, rows, width = w_in.shape
    assert width == W_MIX - LANES + D_ROPE + N_HEADS * D_MODEL
    tr = 256
    return pl.pallas_call(
        _pack_w_kernel,
        grid=(depth, rows // tr),
        in_specs=[pl.BlockSpec((None, tr, width), lambda l, i: (l, i, 0))],
        out_specs=pl.BlockSpec((None, tr, (W_MIX_COL + 1) * W_MIX), lambda l, i: (l, i, 0)),
        out_shape=jax.ShapeDtypeStruct((depth, rows, (W_MIX_COL + 1) * W_MIX), BF16),
        compiler_params=_params("parallel", "parallel"),
        name="pack_w",
    )(w_in)


def _stacked_weights(w_in, c_q_norm, c_k_norm, d_w_q_up, d_w_kv_up):
    depth = w_in.shape[0]
    qs = HEAD_DIM ** -0.5 * LOG2E
    w = _pack_w_in(w_in)
    cgain = jnp.concatenate([jnp.tile(c_q_norm * qs, (1, N_HEADS)), jnp.tile(c_k_norm, (1, KV_GROUPED))],
                            axis=1).astype(F32)[:, None, :]

    def pad_heads(a, dim):
        return jnp.pad(a, ((0, 0), (0, 0), (0, 0), (0, LANES - dim))).reshape(depth, a.shape[1], QK_W)

    dwq = pad_heads(d_w_q_up.reshape(depth, D_Q_LORA, N_HEADS, D_NOPE + D_ROPE), D_NOPE + D_ROPE)
    kvu = d_w_kv_up.reshape(depth, D_KV_LORA, N_HEADS, D_NOPE + D_V)
    dwkv = jnp.concatenate([pad_heads(kvu[..., :D_NOPE], D_NOPE), pad_heads(kvu[..., D_NOPE:], D_V)], axis=2)
    return w, cgain, dwq.astype(BF16), dwkv.astype(BF16)


def kernel(x, w_in, a_rpb, b_sink, t5_bias, c_q_norm, c_k_norm, d_q_norm, d_kv_norm, d_w_q_up, d_w_kv_up,
           w_branch, w_out, ln_pre_mix, ln_post_mix, ln_pre_ffn, ln_post_ffn, ffn_w_gu, ffn_w_down):
    b, s, d = x.shape
    assert (s, d) == (SEQ, D_MODEL)
    depth = w_in.shape[0]
    x2 = x.reshape(b * s, d)
    ctab, dtab = _rope_tables()
    gmat = _group_mean_matrix()
    wbias = _win_bias(t5_bias)
    abias = _nbr_bias(a_rpb)
    w, cgain, dwq, dwkv = _stacked_weights(w_in, c_q_norm, c_k_norm, d_w_q_up, d_w_kv_up)
    wbr, wo, wgu, wdn = (a.astype(BF16) for a in (w_branch, w_out, ffn_w_gu, ffn_w_down))
    g_pre_mix, g_post_mix, g_pre_ffn, g_post_ffn, dqn, dkvn = (
        v.astype(F32)[:, None, :] for v in (ln_pre_mix, ln_post_mix, ln_pre_ffn, ln_post_ffn, d_q_norm, d_kv_norm))
    sink = b_sink.astype(F32) * LOG2E
    for l in range(depth):
        (aq, ak, av, bq, bk, bv, cq, ck, cv, dq, dk, dv) = _proj(
            l, x2, g_pre_mix, w, gmat, cgain, ctab, dtab, dqn, dkvn, dwq, dwkv)
        oa, ob = _local_attn(l, sink[l], aq, ak, av, abias, bq, bk, bv, wbias)
        oc = _dense_attn(cq, ck, cv, KV_GROUPED, "dense_c")
        od = _dense_attn(dq, dk, dv, N_HEADS, "dense_d")
        x2 = _merge(l, x2, g_pre_mix, g_post_mix, oa, ob, oc, od, w, wbr, wo)
        x2 = _ffn(l, x2, g_pre_ffn, g_post_ffn, wgu, wdn)
    return x2.reshape(b, s, d)
```

```python
import functools

import jax
import jax.numpy as jnp
import numpy as np
from jax import lax
from jax.experimental import pallas as pl
from jax.experimental.pallas import tpu as pltpu

D_MODEL = 1024
SEQ = 4096
GRID_W = 64
HEAD_DIM = 64
N_HEADS = 4
KV_GROUPED = 2
EPS = 1e-6
NEG_INF = -1e30
ROPE_THETA = 10000.0
A_WIN_R = 8
A_WIN_C = 16
B_WINDOW = 128
T5_BUCKETS = 32
T5_MAX_DIST = 128
D_Q_LORA = 256
D_KV_LORA = 128
D_NOPE = 64
D_ROPE = 32
D_V = 64
D_FF = 2816
BRANCH_W = 256

LANES = 128
QK_W = N_HEADS * LANES
KG_W = KV_GROUPED * LANES
ROPE_HALF = 16
LOG2E = 1.4426950408889634

A_QROWS = 4
A_KROWS = 12
A_TQ = A_QROWS * GRID_W
A_TK = A_KROWS * GRID_W
B_TQ = 256
B_TK = B_TQ + 2 * B_WINDOW
DENSE_TQ = 1024
DENSE_SUB = 256
TOK_TILE = 1024
FF_CHUNK = 256
MERGE_CHUNK = 256
PROJ_SUBS = 2
TOK_SUBS = 2
W_MIX = 9 * BRANCH_W
W_MIX_COL = 2
VMEM_LIMIT = 56 * 1024 * 1024

BF16 = jnp.bfloat16
F32 = jnp.float32


def _resident(arr, layer=None, cols=None):
    if layer is None:
        return pl.BlockSpec(arr.shape, lambda *_: (0,) * arr.ndim, pipeline_mode=pl.Buffered(1))
    width, col = cols if cols is not None else (arr.shape[-1], 0)
    return pl.BlockSpec((None,) + arr.shape[1:-1] + (width,), lambda *_: (layer,) + (0,) * (arr.ndim - 2) + (col,),
                        pipeline_mode=pl.Buffered(1))


def _params(*sem):
    return pltpu.CompilerParams(dimension_semantics=sem, vmem_limit_bytes=VMEM_LIMIT)


def _rms(x):
    return x * lax.rsqrt(jnp.mean(x * x, axis=-1, keepdims=True) + EPS)


def _rope128(x, cos, sin_signed):
    lane = lax.broadcasted_iota(jnp.int32, x.shape, 1)
    first = (lane % (2 * ROPE_HALF)) < ROPE_HALF
    rot = jnp.where(first, pltpu.roll(x, LANES - ROPE_HALF, 1), pltpu.roll(x, ROPE_HALF, 1))
    return x * cos + rot * sin_signed


def _with_ones(v):
    lane = lax.broadcasted_iota(jnp.int32, v.shape, 1)
    return jnp.where(lane % LANES == HEAD_DIM, 1.0, v)


def _spread_heads(o_ref, x, ones=False):
    lane = lax.broadcasted_iota(jnp.int32, (x.shape[0], LANES), 1)
    low = lane < HEAD_DIM
    fill = jnp.where(lane == HEAD_DIM, 1.0, 0.0) if ones else 0.0
    for c in range(x.shape[1] // LANES):
        pair = x[:, LANES * c:LANES * (c + 1)]
        o_ref[:, 2 * LANES * c:2 * LANES * c + LANES] = jnp.where(low, pair, fill).astype(BF16)
        o_ref[:, 2 * LANES * c + LANES:2 * LANES * (c + 1)] = (
            jnp.where(low, pltpu.roll(pair, HEAD_DIM, 1), fill).astype(BF16))


def _proj_kernel(x_ref, g_ref, w_ref, gmat_ref, cgain_ref, ctab_ref, dtab_ref,
                 dqn_ref, dkvn_ref, dwq_ref, dwkv_ref, *out_refs):
    sub_rows = x_ref.shape[0] // PROJ_SUBS
    for sub in range(PROJ_SUBS):
        rows = slice(sub_rows * sub, sub_rows * (sub + 1))
        _proj_rows(x_ref.at[rows], g_ref, w_ref, gmat_ref, cgain_ref, ctab_ref.at[:, rows], dtab_ref.at[:, rows],
                   dqn_ref, dkvn_ref, dwq_ref, dwkv_ref, *(o.at[rows] for o in out_refs))


def _proj_rows(x_ref, g_ref, w_ref, gmat_ref, cgain_ref, ctab_ref, dtab_ref,
               dqn_ref, dkvn_ref, dwq_ref, dwkv_ref,
               aq_ref, ak_ref, av_ref, bq_ref, bk_ref, bv_ref, cq_ref, ck_ref, cv_ref,
               dq_ref, dk_ref, dv_ref):
    h = (_rms(x_ref[...]) * g_ref[...]).astype(BF16)
    qs = HEAD_DIM ** -0.5 * LOG2E
    a0, b0, c0, d0 = 0, 3 * BRANCH_W, 5 * BRANCH_W, 7 * BRANCH_W

    pa = jnp.dot(h, w_ref[:, a0:b0], preferred_element_type=F32)
    _spread_heads(aq_ref, pa[:, :BRANCH_W] * qs)
    _spread_heads(ak_ref, pa[:, BRANCH_W:2 * BRANCH_W])
    _spread_heads(av_ref, pa[:, 2 * BRANCH_W:], ones=True)

    pb = jnp.dot(h, w_ref[:, b0:c0], preferred_element_type=F32)
    _spread_heads(bq_ref, pb[:, :BRANCH_W] * qs)
    _spread_heads(bk_ref, pb[:, BRANCH_W:BRANCH_W + LANES])
    _spread_heads(bv_ref, pb[:, BRANCH_W + LANES:], ones=True)

    pc = jnp.dot(h, w_ref[:, c0:d0], preferred_element_type=F32)
    _spread_heads(cv_ref, pc[:, BRANCH_W + LANES:], ones=True)
    ccos = ctab_ref[0]
    csin = ctab_ref[1]

    def norm_rope(y, gmat, gain):
        sq = y * y
        hi = sq.astype(BF16)
        lo = (sq - hi.astype(F32)).astype(BF16)
        ms = jnp.dot(hi, gmat, preferred_element_type=F32) + jnp.dot(lo, gmat, preferred_element_type=F32)
        yn = y * lax.rsqrt(ms + EPS) * gain
        return jnp.concatenate([_rope128(yn[:, LANES * c:LANES * (c + 1)], ccos, csin)
                                for c in range(y.shape[1] // LANES)], axis=1)

    _spread_heads(cq_ref, norm_rope(pc[:, :BRANCH_W], gmat_ref[...], cgain_ref[:, :BRANCH_W]))
    _spread_heads(ck_ref, norm_rope(pc[:, BRANCH_W:BRANCH_W + LANES], gmat_ref[:LANES, :LANES],
                                    cgain_ref[:, BRANCH_W:]))

    pd = jnp.dot(h, w_ref[:, d0:], preferred_element_type=F32)
    dcos = dtab_ref[0]
    dsin = dtab_ref[1]
    cq = (_rms(pd[:, :D_Q_LORA]) * dqn_ref[...]).astype(BF16)
    qd = jnp.dot(cq, dwq_ref[...], preferred_element_type=F32)
    ckv = (_rms(pd[:, D_Q_LORA:D_Q_LORA + D_KV_LORA]) * dkvn_ref[...]).astype(BF16)
    kvd = jnp.dot(ckv, dwkv_ref[...], preferred_element_type=F32)
    kr = _rope128(pd[:, D_Q_LORA + D_KV_LORA:], dcos, dsin)
    scale = (D_NOPE + D_ROPE) ** -0.5 * LOG2E
    for hd in range(N_HEADS):
        sl = slice(LANES * hd, LANES * (hd + 1))
        dq_ref[:, sl] = (_rope128(qd[:, sl], dcos, dsin) * scale).astype(BF16)
        dk_ref[:, sl] = (kvd[:, sl] + kr).astype(BF16)
    dv_ref[...] = _with_ones(kvd[:, QK_W:]).astype(BF16)


def _proj(layer, x2, g, w, gmat, cgain, ctab, dtab, dqn, dkvn, dwq, dwkv):
    t = x2.shape[0]
    tm = TOK_TILE
    seq_tiles = SEQ // tm

    def tok(w):
        return pl.BlockSpec((tm, w), lambda i: (i, 0))

    def tab():
        return pl.BlockSpec((2, tm, LANES), lambda i: (0, i % seq_tiles, 0))

    out_w = (QK_W, QK_W, QK_W, QK_W, KG_W, KG_W, QK_W, KG_W, KG_W, QK_W, QK_W, QK_W)
    return pl.pallas_call(
        _proj_kernel,
        grid=(t // tm,),
        in_specs=[tok(D_MODEL), _resident(g, layer), _resident(w, layer, (W_MIX, W_MIX_COL)), _resident(gmat),
                  _resident(cgain, layer), tab(), tab(), _resident(dqn, layer), _resident(dkvn, layer),
                  _resident(dwq, layer), _resident(dwkv, layer)],
        out_specs=[tok(ow) for ow in out_w],
        out_shape=[jax.ShapeDtypeStruct((t, ow), BF16) for ow in out_w],
        compiler_params=_params("parallel"),
        name="proj",
    )(x2, g, w, gmat, cgain, ctab, dtab, dqn, dkvn, dwq, dwkv)


def _attend(q, k, v_ref, rows, kv_head, pv_heads, bias=None, sink=None):
    slab, blk = divmod(kv_head, pv_heads)
    v = v_ref[rows, pv_heads * LANES * slab:pv_heads * LANES * (slab + 1)]
    s = lax.dot_general(q, k, (((1,), (1,)), ((), ())), preferred_element_type=F32)
    if bias is not None:
        s = s + bias
    m = jnp.max(s, axis=-1, keepdims=True)
    if sink is not None:
        m = jnp.maximum(m, sink)
    pv = jnp.dot(jnp.exp2(s - m).astype(BF16), v, preferred_element_type=F32)[:, LANES * blk:LANES * (blk + 1)]
    total = pv[:, HEAD_DIM:HEAD_DIM + 1]
    if sink is not None:
        total = total + jnp.exp2(sink - m)
    return pv * (1.0 / total)


def _store_heads(o_ref, outs, rows=slice(None)):
    low = lax.broadcasted_iota(jnp.int32, outs[0].shape, 1) < HEAD_DIM
    for g in range(N_HEADS // 2):
        pair = jnp.where(low, outs[2 * g], pltpu.roll(outs[2 * g + 1], HEAD_DIM, 1))
        o_ref[rows, LANES * g:LANES * (g + 1)] = pair.astype(BF16)


def _local_kernel(sink_ref, aq_ref, ak_ref, av_ref, abias_ref, bq_ref, bk_ref, bv_ref, bbias_ref, oa_ref, ob_ref):
    j = pl.program_id(1)
    a_start = pl.multiple_of(jnp.clip(A_QROWS * j - A_WIN_R // 2, 0, GRID_W - A_KROWS) * GRID_W, A_TQ)
    a_rows = pl.ds(a_start, A_TK)
    b_start = pl.multiple_of(jnp.clip(B_TQ * j - B_WINDOW, 0, SEQ - B_TK), B_WINDOW)
    b_rows = pl.ds(b_start, B_TK)
    a_outs, b_outs = [], []
    for h in range(N_HEADS):
        sl = slice(LANES * h, LANES * (h + 1))
        kv = h // (N_HEADS // KV_GROUPED)
        a_outs.append(_attend(aq_ref[:, sl], ak_ref[a_rows, sl], av_ref, a_rows, h, 2, bias=abias_ref[0, h]))
        b_outs.append(_attend(bq_ref[:, sl], bk_ref[b_rows, LANES * kv:LANES * (kv + 1)], bv_ref, b_rows, kv, 2,
                              bias=bbias_ref[0, h], sink=sink_ref[h]))
    _store_heads(oa_ref, a_outs)
    _store_heads(ob_ref, b_outs)


def _local_attn(layer, sink, aq, ak, av, abias, bq, bk, bv, bbias):
    assert A_TQ == B_TQ
    t = aq.shape[0]
    nb = t // SEQ
    nj = SEQ // A_TQ

    def tok(w):
        return pl.BlockSpec((A_TQ, w), lambda b, j: (b * nj + j, 0))

    def seq(w):
        return pl.BlockSpec((SEQ, w), lambda b, j: (b, 0))

    def kind(j):
        return (j > 0).astype(jnp.int32) + (j == nj - 1).astype(jnp.int32)

    a_bias = pl.BlockSpec((None, 1, N_HEADS, A_TQ, A_TK), lambda b, j: (layer, kind(j), 0, 0, 0))
    b_bias = pl.BlockSpec((1, N_HEADS, B_TQ, B_TK), lambda b, j: (kind(j), 0, 0, 0))

    return pl.pallas_call(
        _local_kernel,
        grid=(nb, nj),
        in_specs=[pl.BlockSpec(memory_space=pltpu.SMEM),
                  tok(QK_W), seq(QK_W), seq(QK_W), a_bias,
                  tok(QK_W), seq(KG_W), seq(KG_W), b_bias],
        out_specs=[tok(BRANCH_W), tok(BRANCH_W)],
        out_shape=[jax.ShapeDtypeStruct((t, BRANCH_W), BF16)] * 2,
        compiler_params=_params("parallel", "arbitrary"),
        name="local_attn",
    )(sink, aq, ak, av, abias, bq, bk, bv, bbias)


def _dense_kernel(q_ref, k_ref, v_ref, o_ref, *, kv_heads):
    keys = slice(None)
    for sub in range(DENSE_TQ // DENSE_SUB):
        rows = slice(DENSE_SUB * sub, DENSE_SUB * (sub + 1))
        outs = []
        for h in range(N_HEADS):
            kv = h // (N_HEADS // kv_heads)
            outs.append(_attend(q_ref[rows, LANES * h:LANES * (h + 1)], k_ref[:, LANES * kv:LANES * (kv + 1)],
                                v_ref, keys, kv, 1))
        _store_heads(o_ref, outs, rows)


def _dense_attn(q, k, v, kv_heads, name):
    t = q.shape[0]
    nb = t // SEQ
    nq = SEQ // DENSE_TQ
    return pl.pallas_call(
        functools.partial(_dense_kernel, kv_heads=kv_heads),
        grid=(nb, nq),
        in_specs=[pl.BlockSpec((DENSE_TQ, QK_W), lambda b, n: (b * nq + n, 0)),
                  pl.BlockSpec((SEQ, kv_heads * LANES), lambda b, n: (b, 0)),
                  pl.BlockSpec((SEQ, kv_heads * LANES), lambda b, n: (b, 0))],
        out_specs=pl.BlockSpec((DENSE_TQ, BRANCH_W), lambda b, n: (b * nq + n, 0)),
        out_shape=jax.ShapeDtypeStruct((t, BRANCH_W), BF16),
        compiler_params=_params("parallel", "arbitrary"),
        name=name,
    )(q, k, v)


def _merge_kernel(x_ref, gpre_ref, gpost_ref, oa_ref, ob_ref, oc_ref, od_ref, wg_ref, wbr_ref, wo_ref, y_ref):
    x = x_ref[...]
    h = (_rms(x) * gpre_ref[...]).astype(BF16)
    m = None
    for c in range(D_MODEL // MERGE_CHUNK):
        cols = slice(MERGE_CHUNK * c, MERGE_CHUNK * (c + 1))
        merged = None
        for n, o_ref in enumerate((oa_ref, ob_ref, oc_ref, od_ref)):
            logits = jnp.dot(h, wg_ref[:, D_MODEL * n + MERGE_CHUNK * c:D_MODEL * n + MERGE_CHUNK * (c + 1)],
                             preferred_element_type=F32)
            y = jnp.dot(o_ref[...], wbr_ref[n, :, cols], preferred_element_type=F32)
            term = jax.nn.sigmoid(logits) * y
            merged = term if merged is None else merged + term
        part = jnp.dot(merged.astype(BF16), wo_ref[cols, :], preferred_element_type=F32)
        m = part if m is None else m + part
    y_ref[...] = x + _rms(m) * gpost_ref[...]


def _merge(layer, x2, gpre, gpost, oa, ob, oc, od, wg, wbr, wo):
    t = x2.shape[0]
    tm = TOK_TILE

    def tok(w):
        return pl.BlockSpec((tm, w), lambda i: (i, 0))

    return pl.pallas_call(
        _merge_kernel,
        grid=(t // tm,),
        in_specs=[tok(D_MODEL), _resident(gpre, layer), _resident(gpost, layer), tok(BRANCH_W), tok(BRANCH_W),
                  tok(BRANCH_W), tok(BRANCH_W), _resident(wg, layer, (N_HEADS * D_MODEL, 0)), _resident(wbr, layer),
                  _resident(wo, layer)],
        out_specs=tok(D_MODEL),
        out_shape=jax.ShapeDtypeStruct((t, D_MODEL), F32),
        compiler_params=_params("parallel"),
        name="merge",
    )(x2, gpre, gpost, oa, ob, oc, od, wg, wbr, wo)


def _ffn_kernel(x_ref, gpre_ref, gpost_ref, wgu_ref, wdn_ref, y_ref):
    sub_rows = x_ref.shape[0] // TOK_SUBS
    for sub in range(TOK_SUBS):
        rows = slice(sub_rows * sub, sub_rows * (sub + 1))
        _ffn_rows(x_ref.at[rows], gpre_ref, gpost_ref, wgu_ref, wdn_ref, y_ref.at[rows])


def _ffn_rows(x_ref, gpre_ref, gpost_ref, wgu_ref, wdn_ref, y_ref):
    x = x_ref[...]
    h = (_rms(x) * gpre_ref[...]).astype(BF16)
    acc = None
    for c in range(D_FF // FF_CHUNK):
        lo = FF_CHUNK * c
        gate = jnp.dot(h, wgu_ref[:, lo:lo + FF_CHUNK], preferred_element_type=F32)
        up = jnp.dot(h, wgu_ref[:, D_FF + lo:D_FF + lo + FF_CHUNK], preferred_element_type=F32)
        act = (jax.nn.silu(gate) * up).astype(BF16)
        part = jnp.dot(act, wdn_ref[lo:lo + FF_CHUNK, :], preferred_element_type=F32)
        acc = part if acc is None else acc + part
    y_ref[...] = x + _rms(acc) * gpost_ref[...]


def _ffn(layer, x2, gpre, gpost, wgu, wdn):
    t = x2.shape[0]
    tm = TOK_TILE
    tok = pl.BlockSpec((tm, D_MODEL), lambda i: (i, 0))
    return pl.pallas_call(
        _ffn_kernel,
        grid=(t // tm,),
        in_specs=[tok, _resident(gpre, layer), _resident(gpost, layer), _resident(wgu, layer),
                  _resident(wdn, layer)],
        out_specs=tok,
        out_shape=jax.ShapeDtypeStruct((t, D_MODEL), F32),
        compiler_params=_params("parallel"),
        name="ffn",
    )(x2, gpre, gpost, wgu, wdn)


def _rope_tables():
    inv_freq = 1.0 / (ROPE_THETA ** (jnp.arange(ROPE_HALF, dtype=F32) * (1.0 / ROPE_HALF)))

    def group(pos):
        ang = pos.astype(F32)[:, None] * inv_freq[None, :]
        c, s = jnp.cos(ang), jnp.sin(ang)
        return jnp.concatenate([c, c], axis=-1), jnp.concatenate([-s, s], axis=-1)

    gc, gs = group(jnp.arange(GRID_W))
    rc, rs = jnp.repeat(gc, GRID_W, axis=0), jnp.repeat(gs, GRID_W, axis=0)
    cc, cs = jnp.tile(gc, (SEQ // GRID_W, 1)), jnp.tile(gs, (SEQ // GRID_W, 1))
    pc, ps = group(jnp.arange(SEQ))
    ones = jnp.ones((SEQ, 2 * ROPE_HALF), F32)
    zeros = jnp.zeros((SEQ, 2 * ROPE_HALF), F32)
    ctab = jnp.stack([jnp.concatenate([rc, cc, rc, cc], axis=-1),
                      jnp.concatenate([rs, cs, rs, cs], axis=-1)])
    dtab = jnp.stack([jnp.concatenate([ones, ones, pc, ones], axis=-1),
                      jnp.concatenate([zeros, zeros, ps, zeros], axis=-1)])
    return ctab, dtab


def _group_mean_matrix():
    i = np.arange(2 * LANES)
    m = (i[:, None] // HEAD_DIM) == (i[None, :] // HEAD_DIM)
    return jnp.asarray(m.astype(np.float32) / HEAD_DIM, dtype=BF16)


def _t5_bucket_index(rel):
    nb = T5_BUCKETS // 2
    max_exact = nb // 2
    ret = (rel > 0).astype(np.int32) * nb
    n = np.abs(rel)
    large = max_exact + (np.log(np.maximum(n, 1) / max_exact) / np.log(T5_MAX_DIST / max_exact)
                         * (nb - max_exact)).astype(np.int32)
    large = np.minimum(large, nb - 1)
    return ret + np.where(n < max_exact, n, large)


def _nbr_group_kinds():
    rows = SEQ // GRID_W
    ng = rows // A_QROWS
    kinds = []
    for g in (0, 1, ng - 1):
        ws = int(np.clip(A_QROWS * g - A_WIN_R // 2, 0, rows - A_KROWS))
        qr = A_QROWS * g + np.arange(A_QROWS)
        rs = np.clip(qr - A_WIN_R // 2, 0, rows - A_WIN_R)
        kr = ws + np.arange(A_KROWS)
        ok = (kr[None, :] >= rs[:, None]) & (kr[None, :] < rs[:, None] + A_WIN_R)
        kinds.append((ws - A_QROWS * g + A_WIN_R - 1, ok))
    return kinds


def _nbr_bias_kernel(r_ref, o_ref):
    lane = lax.broadcasted_iota(jnp.int32, (GRID_W, LANES), 1)
    qc = lax.broadcasted_iota(jnp.int32, (GRID_W, LANES), 0)
    kc = lane % GRID_W
    cs = jnp.clip(qc - A_WIN_C // 2, 0, GRID_W - A_WIN_C)
    col_ok = (kc >= cs) & (kc < cs + A_WIN_C)
    left = lane < GRID_W
    pairs = [jnp.where(col_ok, pltpu.roll(jnp.broadcast_to(r_ref[j:j + 1, :], (GRID_W, LANES)),
                                          LANES - (A_WIN_C - 1), 1, stride=1, stride_axis=0), NEG_INF)
             for j in range(2 * A_WIN_R)]
    masked = jnp.full((GRID_W, LANES), NEG_INF, F32)
    for t, (s, row_ok) in enumerate(_nbr_group_kinds()):
        for qr in range(A_QROWS):
            for p in range(A_KROWS // 2):
                j = 2 * p - qr + s + 1
                ok_a, ok_b = bool(row_ok[qr][2 * p]), bool(row_ok[qr][2 * p + 1])
                if not (ok_a or ok_b):
                    blk = masked
                elif ok_a and ok_b:
                    blk = pairs[j]
                else:
                    blk = jnp.where(left if ok_a else ~left, pairs[j], NEG_INF)
                o_ref[t, GRID_W * qr:GRID_W * (qr + 1), LANES * p:LANES * (p + 1)] = blk


def _nbr_bias(rpb):
    depth = rpb.shape[0]
    r = jnp.pad(rpb.astype(F32) * LOG2E, ((0, 0), (0, 0), (1, 1), (0, GRID_W - (2 * A_WIN_C - 1))))
    r = jnp.concatenate([r[:, :, :-1], r[:, :, 1:]], axis=-1)
    return pl.pallas_call(
        _nbr_bias_kernel,
        grid=(depth, N_HEADS),
        in_specs=[pl.BlockSpec((None, None, 2 * A_WIN_R, LANES), lambda l, h: (l, h, 0, 0))],
        out_specs=pl.BlockSpec((None, 3, None, A_TQ, A_TK), lambda l, h: (l, 0, h, 0, 0)),
        out_shape=jax.ShapeDtypeStruct((depth, 3, N_HEADS, A_TQ, A_TK), F32),
        compiler_params=_params("parallel", "parallel"),
        name="nbr_bias",
    )(r)


def _win_bias(t5_table):
    rel_values = np.arange(-B_WINDOW, B_WINDOW + 1)
    by_rel = t5_table.astype(F32)[_t5_bucket_index(rel_values)].T * LOG2E
    by_rel = jnp.pad(by_rel, ((0, 0), (LANES, LANES - 1)))[:, None, :]
    return pl.pallas_call(
        _win_bias_kernel,
        grid=(N_HEADS,),
        in_specs=[pl.BlockSpec((None, 1, 4 * LANES), lambda h: (h, 0, 0))],
        out_specs=pl.BlockSpec((3, None, B_TQ, B_TK), lambda h: (0, h, 0, 0)),
        out_shape=jax.ShapeDtypeStruct((3, N_HEADS, B_TQ, B_TK), F32),
        compiler_params=_params("parallel"),
        name="win_bias",
    )(by_rel)


def _win_bias_kernel(u_ref, o_ref):
    assert B_WINDOW == LANES
    nq = SEQ // B_TQ
    qr = lax.broadcasted_iota(jnp.int32, (LANES, LANES), 0)
    lane = lax.broadcasted_iota(jnp.int32, (LANES, LANES), 1)
    tiles = {}
    for d in (-LANES, 0, LANES):
        src = jnp.broadcast_to(u_ref[:, LANES + d:3 * LANES + d], (LANES, 2 * LANES))
        skew = pltpu.roll(src, 0, 1, stride=1, stride_axis=0)[:, LANES:]
        tiles[d] = jnp.where(jnp.abs(lane - qr + d) <= B_WINDOW, skew, NEG_INF)
    masked = jnp.full((LANES, LANES), NEG_INF, F32)
    for t, n in enumerate((0, 1, nq - 1)):
        start = int(np.clip(B_TQ * n - B_WINDOW, 0, SEQ - B_TK))
        for qb in range(B_TQ // LANES):
            for kb in range(B_TK // LANES):
                d = (start + LANES * kb) - (B_TQ * n + LANES * qb)
                o_ref[t, LANES * qb:LANES * (qb + 1), LANES * kb:LANES * (kb + 1)] = tiles.get(d, masked)


def _stacked_weights(w_in, c_q_norm, c_k_norm, d_w_q_up, d_w_kv_up):
    depth = w_in.shape[0]
    n_kr = W_MIX - LANES
    n_mix = n_kr + D_ROPE
    qs = HEAD_DIM ** -0.5 * LOG2E
    gate_w = w_in.shape[2] - n_mix
    zeros = jnp.zeros((depth, D_MODEL, W_MIX_COL * W_MIX - gate_w), BF16)
    w_kr = jnp.pad(w_in[:, :, n_kr:n_mix], ((0, 0), (0, 0), (D_NOPE, LANES - D_NOPE - D_ROPE)))
    w = jnp.concatenate([w_in[:, :, n_mix:].astype(BF16), zeros, w_in[:, :, :n_kr].astype(BF16),
                         w_kr.astype(BF16)], axis=2)
    cgain = jnp.concatenate([jnp.tile(c_q_norm * qs, (1, N_HEADS)), jnp.tile(c_k_norm, (1, KV_GROUPED))],
                            axis=1).astype(F32)[:, None, :]

    def pad_heads(a, dim):
        return jnp.pad(a, ((0, 0), (0, 0), (0, 0), (0, LANES - dim))).reshape(depth, a.shape[1], QK_W)

    dwq = pad_heads(d_w_q_up.reshape(depth, D_Q_LORA, N_HEADS, D_NOPE + D_ROPE), D_NOPE + D_ROPE)
    kvu = d_w_kv_up.reshape(depth, D_KV_LORA, N_HEADS, D_NOPE + D_V)
    dwkv = jnp.concatenate([pad_heads(kvu[..., :D_NOPE], D_NOPE), pad_heads(kvu[..., D_NOPE:], D_V)], axis=2)
    return w, cgain, dwq.astype(BF16), dwkv.astype(BF16)


def kernel(x, w_in, a_rpb, b_sink, t5_bias, c_q_norm, c_k_norm, d_q_norm, d_kv_norm, d_w_q_up, d_w_kv_up,
           w_branch, w_out, ln_pre_mix, ln_post_mix, ln_pre_ffn, ln_post_ffn, ffn_w_gu, ffn_w_down):
    b, s, d = x.shape
    assert (s, d) == (SEQ, D_MODEL)
    depth = w_in.shape[0]
    x2 = x.reshape(b * s, d)
    ctab, dtab = _rope_tables()
    gmat = _group_mean_matrix()
    wbias = _win_bias(t5_bias)
    abias = _nbr_bias(a_rpb)
    w, cgain, dwq, dwkv = _stacked_weights(w_in, c_q_norm, c_k_norm, d_w_q_up, d_w_kv_up)
    wbr, wo, wgu, wdn = (a.astype(BF16) for a in (w_branch, w_out, ffn_w_gu, ffn_w_down))
    g_pre_mix, g_post_mix, g_pre_ffn, g_post_ffn, dqn, dkvn = (
        v.astype(F32)[:, None, :] for v in (ln_pre_mix, ln_post_mix, ln_pre_ffn, ln_post_ffn, d_q_norm, d_kv_norm))
    sink = b_sink.astype(F32) * LOG2E
    for l in range(depth):
        (aq, ak, av, bq, bk, bv, cq, ck, cv, dq, dk, dv) = _proj(
            l, x2, g_pre_mix, w, gmat, cgain, ctab, dtab, dqn, dkvn, dwq, dwkv)
        oa, ob = _local_attn(l, sink[l], aq, ak, av, abias, bq, bk, bv, wbias)
        oc = _dense_attn(cq, ck, cv, KV_GROUPED, "dense_c")
        od = _dense_attn(dq, dk, dv, N_HEADS, "dense_d")
        x2 = _merge(l, x2, g_pre_mix, g_post_mix, oa, ob, oc, od, w, wbr, wo)
        x2 = _ffn(l, x2, g_pre_ffn, g_post_ffn, wgu, wdn)
    return x2.reshape(b, s, d)
```

```python
import functools

import jax
import jax.numpy as jnp
import numpy as np
from jax import lax
from jax.experimental import pallas as pl
from jax.experimental.pallas import tpu as pltpu

D_MODEL = 1024
SEQ = 4096
GRID_W = 64
HEAD_DIM = 64
N_HEADS = 4
KV_GROUPED = 2
EPS = 1e-6
NEG_INF = -1e30
ROPE_THETA = 10000.0
A_WIN_R = 8
A_WIN_C = 16
B_WINDOW = 128
T5_BUCKETS = 32
T5_MAX_DIST = 128
D_Q_LORA = 256
D_KV_LORA = 128
D_NOPE = 64
D_ROPE = 32
D_V = 64
D_FF = 2816
BRANCH_W = 256

LANES = 128
QK_W = N_HEADS * LANES
KG_W = KV_GROUPED * LANES
ROPE_HALF = 16
LOG2E = 1.4426950408889634

A_QROWS = 4
A_KROWS = 12
A_TQ = A_QROWS * GRID_W
A_TK = A_KROWS * GRID_W
B_TQ = 256
B_TK = B_TQ + 2 * B_WINDOW
DENSE_TQ = 512
DENSE_SUB = 256
TOK_TILE = 1024
FF_CHUNK = 256
MERGE_CHUNK = 256
PROJ_SUBS = 2
CHAIN_LOOKAHEAD = 2
TOK_SUBS = 2
W_MIX = 9 * BRANCH_W
W_MIX_COL = 2
VMEM_LIMIT = 56 * 1024 * 1024

BF16 = jnp.bfloat16
F32 = jnp.float32


def _resident(arr, layer=None, cols=None):
    if layer is None:
        return pl.BlockSpec(arr.shape, lambda *_: (0,) * arr.ndim, pipeline_mode=pl.Buffered(1))
    width, col = cols if cols is not None else (arr.shape[-1], 0)
    return pl.BlockSpec((None,) + arr.shape[1:-1] + (width,), lambda *_: (layer,) + (0,) * (arr.ndim - 2) + (col,),
                        pipeline_mode=pl.Buffered(1))


def _params(*sem):
    return pltpu.CompilerParams(dimension_semantics=sem, vmem_limit_bytes=VMEM_LIMIT)


def _rms(x):
    return x * lax.rsqrt(jnp.mean(x * x, axis=-1, keepdims=True) + EPS)


def _rope128(x, cos, sin_signed):
    lane = lax.broadcasted_iota(jnp.int32, x.shape, 1)
    first = (lane % (2 * ROPE_HALF)) < ROPE_HALF
    rot = jnp.where(first, pltpu.roll(x, LANES - ROPE_HALF, 1), pltpu.roll(x, ROPE_HALF, 1))
    return x * cos + rot * sin_signed


def _with_ones(v):
    lane = lax.broadcasted_iota(jnp.int32, v.shape, 1)
    return jnp.where(lane % LANES == HEAD_DIM, 1.0, v)


def _spread_heads(o_ref, x, ones=False):
    lane = lax.broadcasted_iota(jnp.int32, (x.shape[0], LANES), 1)
    low = lane < HEAD_DIM
    fill = jnp.where(lane == HEAD_DIM, 1.0, 0.0) if ones else 0.0
    for c in range(x.shape[1] // LANES):
        pair = x[:, LANES * c:LANES * (c + 1)]
        o_ref[:, 2 * LANES * c:2 * LANES * c + LANES] = jnp.where(low, pair, fill).astype(BF16)
        o_ref[:, 2 * LANES * c + LANES:2 * LANES * (c + 1)] = (
            jnp.where(low, pltpu.roll(pair, HEAD_DIM, 1), fill).astype(BF16))


def _proj_kernel(x_ref, g_ref, w_ref, gmat_ref, cgain_ref, ctab_ref, dtab_ref,
                 dqn_ref, dkvn_ref, dwq_ref, dwkv_ref, *out_refs):
    sub_rows = x_ref.shape[0] // PROJ_SUBS
    for sub in range(PROJ_SUBS):
        rows = slice(sub_rows * sub, sub_rows * (sub + 1))
        _proj_rows(x_ref.at[rows], g_ref, w_ref, gmat_ref, cgain_ref, ctab_ref.at[:, rows], dtab_ref.at[:, rows],
                   dqn_ref, dkvn_ref, dwq_ref, dwkv_ref, *(o.at[rows] for o in out_refs))


def _proj_rows(x_ref, g_ref, w_ref, gmat_ref, cgain_ref, ctab_ref, dtab_ref,
               dqn_ref, dkvn_ref, dwq_ref, dwkv_ref,
               aq_ref, ak_ref, av_ref, bq_ref, bk_ref, bv_ref, cq_ref, ck_ref, cv_ref,
               dq_ref, dk_ref, dv_ref):
    h = (_rms(x_ref[...]) * g_ref[...]).astype(BF16)
    qs = HEAD_DIM ** -0.5 * LOG2E
    a0, b0, c0, d0 = 0, 3 * BRANCH_W, 5 * BRANCH_W, 7 * BRANCH_W

    pd = jnp.dot(h, w_ref[:, d0:], preferred_element_type=F32)
    dcos = dtab_ref[0]
    dsin = dtab_ref[1]
    cq = (_rms(pd[:, :D_Q_LORA]) * dqn_ref[...]).astype(BF16)
    qd = jnp.dot(cq, dwq_ref[...], preferred_element_type=F32)
    ckv = (_rms(pd[:, D_Q_LORA:D_Q_LORA + D_KV_LORA]) * dkvn_ref[...]).astype(BF16)
    kvd = jnp.dot(ckv, dwkv_ref[...], preferred_element_type=F32)
    kr = _rope128(pd[:, D_Q_LORA + D_KV_LORA:], dcos, dsin)
    scale = (D_NOPE + D_ROPE) ** -0.5 * LOG2E
    for hd in range(N_HEADS):
        sl = slice(LANES * hd, LANES * (hd + 1))
        dq_ref[:, sl] = (_rope128(qd[:, sl], dcos, dsin) * scale).astype(BF16)
        dk_ref[:, sl] = (kvd[:, sl] + kr).astype(BF16)
    dv_ref[...] = _with_ones(kvd[:, QK_W:]).astype(BF16)

    pa = jnp.dot(h, w_ref[:, a0:b0], preferred_element_type=F32)
    _spread_heads(aq_ref, pa[:, :BRANCH_W] * qs)
    _spread_heads(ak_ref, pa[:, BRANCH_W:2 * BRANCH_W])
    _spread_heads(av_ref, pa[:, 2 * BRANCH_W:], ones=True)

    pb = jnp.dot(h, w_ref[:, b0:c0], preferred_element_type=F32)
    _spread_heads(bq_ref, pb[:, :BRANCH_W] * qs)
    _spread_heads(bk_ref, pb[:, BRANCH_W:BRANCH_W + LANES])
    _spread_heads(bv_ref, pb[:, BRANCH_W + LANES:], ones=True)

    pc = jnp.dot(h, w_ref[:, c0:d0], preferred_element_type=F32)
    _spread_heads(cv_ref, pc[:, BRANCH_W + LANES:], ones=True)
    ccos = ctab_ref[0]
    csin = ctab_ref[1]

    def norm_rope(y, gmat, gain):
        sq = y * y
        hi = sq.astype(BF16)
        lo = (sq - hi.astype(F32)).astype(BF16)
        ms = jnp.dot(hi, gmat, preferred_element_type=F32) + jnp.dot(lo, gmat, preferred_element_type=F32)
        yn = y * lax.rsqrt(ms + EPS) * gain
        return jnp.concatenate([_rope128(yn[:, LANES * c:LANES * (c + 1)], ccos, csin)
                                for c in range(y.shape[1] // LANES)], axis=1)

    _spread_heads(cq_ref, norm_rope(pc[:, :BRANCH_W], gmat_ref[...], cgain_ref[:, :BRANCH_W]))
    _spread_heads(ck_ref, norm_rope(pc[:, BRANCH_W:BRANCH_W + LANES], gmat_ref[:LANES, :LANES],
                                    cgain_ref[:, BRANCH_W:]))


def _proj(layer, x2, g, w, gmat, cgain, ctab, dtab, dqn, dkvn, dwq, dwkv):
    t = x2.shape[0]
    tm = TOK_TILE
    seq_tiles = SEQ // tm

    def tok(w):
        return pl.BlockSpec((tm, w), lambda i: (i, 0))

    def tab():
        return pl.BlockSpec((2, tm, LANES), lambda i: (0, i % seq_tiles, 0))

    out_w = (QK_W, QK_W, QK_W, QK_W, KG_W, KG_W, QK_W, KG_W, KG_W, QK_W, QK_W, QK_W)
    return pl.pallas_call(
        _proj_kernel,
        grid=(t // tm,),
        in_specs=[tok(D_MODEL), _resident(g, layer), _resident(w, layer, (W_MIX, W_MIX_COL)), _resident(gmat),
                  _resident(cgain, layer), tab(), tab(), _resident(dqn, layer), _resident(dkvn, layer),
                  _resident(dwq, layer), _resident(dwkv, layer)],
        out_specs=[tok(ow) for ow in out_w],
        out_shape=[jax.ShapeDtypeStruct((t, ow), BF16) for ow in out_w],
        compiler_params=_params("parallel"),
        name="proj",
    )(x2, g, w, gmat, cgain, ctab, dtab, dqn, dkvn, dwq, dwkv)


def _scores(q, k, bias=None):
    s = lax.dot_general(q, k, (((1,), (1,)), ((), ())), preferred_element_type=F32)
    return s if bias is None else s + bias


def _softmax_pv(s, v_ref, rows, kv_head, pv_heads, sink=None):
    slab, blk = divmod(kv_head, pv_heads)
    v = v_ref[rows, pv_heads * LANES * slab:pv_heads * LANES * (slab + 1)]
    m = jnp.max(s, axis=-1, keepdims=True)
    if sink is not None:
        m = jnp.maximum(m, sink)
    pv = jnp.dot(jnp.exp2(s - m).astype(BF16), v, preferred_element_type=F32)[:, LANES * blk:LANES * (blk + 1)]
    total = pv[:, HEAD_DIM:HEAD_DIM + 1]
    if sink is not None:
        total = total + jnp.exp2(sink - m)
    return pv * (1.0 / total)


def _run_chains(chains):
    outs = []
    ready = [chains[i][0]() for i in range(min(CHAIN_LOOKAHEAD, len(chains)))]
    for i, (_, finish) in enumerate(chains):
        if i + CHAIN_LOOKAHEAD < len(chains):
            ready.append(chains[i + CHAIN_LOOKAHEAD][0]())
        outs.append(finish(ready.pop(0)))
    return outs


def _store_heads(o_ref, outs, rows=slice(None)):
    low = lax.broadcasted_iota(jnp.int32, outs[0].shape, 1) < HEAD_DIM
    for g in range(N_HEADS // 2):
        pair = jnp.where(low, outs[2 * g], pltpu.roll(outs[2 * g + 1], HEAD_DIM, 1))
        o_ref[rows, LANES * g:LANES * (g + 1)] = pair.astype(BF16)


def _local_kernel(sink_ref, aq_ref, ak_ref, av_ref, abias_ref, bq_ref, bk_ref, bv_ref, bbias_ref, oa_ref, ob_ref):
    j = pl.program_id(1)
    a_start = pl.multiple_of(jnp.clip(A_QROWS * j - A_WIN_R // 2, 0, GRID_W - A_KROWS) * GRID_W, A_TQ)
    a_rows = pl.ds(a_start, A_TK)
    b_start = pl.multiple_of(jnp.clip(B_TQ * j - B_WINDOW, 0, SEQ - B_TK), B_WINDOW)
    b_rows = pl.ds(b_start, B_TK)
    chains = []
    for h in range(N_HEADS):
        sl = slice(LANES * h, LANES * (h + 1))
        kv = h // (N_HEADS // KV_GROUPED)
        kvl = slice(LANES * kv, LANES * (kv + 1))
        chains.append((lambda h=h, sl=sl: _scores(aq_ref[:, sl], ak_ref[a_rows, sl], abias_ref[0, h]),
                       lambda s, h=h: _softmax_pv(s, av_ref, a_rows, h, 2)))
        chains.append((lambda h=h, sl=sl, kvl=kvl: _scores(bq_ref[:, sl], bk_ref[b_rows, kvl], bbias_ref[0, h]),
                       lambda s, h=h, kv=kv: _softmax_pv(s, bv_ref, b_rows, kv, 2, sink=sink_ref[h])))
    outs = _run_chains(chains)
    _store_heads(oa_ref, outs[0::2])
    _store_heads(ob_ref, outs[1::2])


def _local_attn(layer, sink, aq, ak, av, abias, bq, bk, bv, bbias):
    assert A_TQ == B_TQ
    t = aq.shape[0]
    nb = t // SEQ
    nj = SEQ // A_TQ

    def tok(w):
        return pl.BlockSpec((A_TQ, w), lambda b, j: (b * nj + j, 0))

    def seq(w):
        return pl.BlockSpec((SEQ, w), lambda b, j: (b, 0))

    def kind(j):
        return (j > 0).astype(jnp.int32) + (j == nj - 1).astype(jnp.int32)

    a_bias = pl.BlockSpec((None, 1, N_HEADS, A_TQ, A_TK), lambda b, j: (layer, kind(j), 0, 0, 0))
    b_bias = pl.BlockSpec((1, N_HEADS, B_TQ, B_TK), lambda b, j: (kind(j), 0, 0, 0))

    return pl.pallas_call(
        _local_kernel,
        grid=(nb, nj),
        in_specs=[pl.BlockSpec(memory_space=pltpu.SMEM),
                  tok(QK_W), seq(QK_W), seq(QK_W), a_bias,
                  tok(QK_W), seq(KG_W), seq(KG_W), b_bias],
        out_specs=[tok(BRANCH_W), tok(BRANCH_W)],
        out_shape=[jax.ShapeDtypeStruct((t, BRANCH_W), BF16)] * 2,
        compiler_params=_params("parallel", "arbitrary"),
        name="local_attn",
    )(sink, aq, ak, av, abias, bq, bk, bv, bbias)


def _dense_kernel(q_ref, k_ref, v_ref, o_ref, *, kv_heads):
    keys = slice(None)
    subs = [slice(DENSE_SUB * i, DENSE_SUB * (i + 1)) for i in range(DENSE_TQ // DENSE_SUB)]
    chains = []
    for rows in subs:
        for h in range(N_HEADS):
            kv = h // (N_HEADS // kv_heads)
            chains.append((lambda rows=rows, h=h, kv=kv: _scores(q_ref[rows, LANES * h:LANES * (h + 1)],
                                                                 k_ref[:, LANES * kv:LANES * (kv + 1)]),
                           lambda s, kv=kv: _softmax_pv(s, v_ref, keys, kv, 1)))
    outs = _run_chains(chains)
    for i, rows in enumerate(subs):
        _store_heads(o_ref, outs[N_HEADS * i:N_HEADS * (i + 1)], rows)


def _dense_attn(q, k, v, kv_heads, name):
    t = q.shape[0]
    nb = t // SEQ
    nq = SEQ // DENSE_TQ
    return pl.pallas_call(
        functools.partial(_dense_kernel, kv_heads=kv_heads),
        grid=(nb, nq),
        in_specs=[pl.BlockSpec((DENSE_TQ, QK_W), lambda b, n: (b * nq + n, 0)),
                  pl.BlockSpec((SEQ, kv_heads * LANES), lambda b, n: (b, 0)),
                  pl.BlockSpec((SEQ, kv_heads * LANES), lambda b, n: (b, 0))],
        out_specs=pl.BlockSpec((DENSE_TQ, BRANCH_W), lambda b, n: (b * nq + n, 0)),
        out_shape=jax.ShapeDtypeStruct((t, BRANCH_W), BF16),
        compiler_params=_params("parallel", "arbitrary"),
        name=name,
    )(q, k, v)


def _merge_kernel(x_ref, gpre_ref, gpost_ref, oa_ref, ob_ref, oc_ref, od_ref, wg_ref, wbr_ref, wo_ref, y_ref):
    x = x_ref[...]
    h = (_rms(x) * gpre_ref[...]).astype(BF16)
    m = None
    for c in range(D_MODEL // MERGE_CHUNK):
        cols = slice(MERGE_CHUNK * c, MERGE_CHUNK * (c + 1))
        merged = None
        for n, o_ref in enumerate((oa_ref, ob_ref, oc_ref, od_ref)):
            logits = jnp.dot(h, wg_ref[:, D_MODEL * n + MERGE_CHUNK * c:D_MODEL * n + MERGE_CHUNK * (c + 1)],
                             preferred_element_type=F32)
            y = jnp.dot(o_ref[...], wbr_ref[n, :, cols], preferred_element_type=F32)
            term = jax.nn.sigmoid(logits) * y
            merged = term if merged is None else merged + term
        part = jnp.dot(merged.astype(BF16), wo_ref[cols, :], preferred_element_type=F32)
        m = part if m is None else m + part
    y_ref[...] = x + _rms(m) * gpost_ref[...]


def _merge(layer, x2, gpre, gpost, oa, ob, oc, od, wg, wbr, wo):
    t = x2.shape[0]
    tm = TOK_TILE

    def tok(w):
        return pl.BlockSpec((tm, w), lambda i: (i, 0))

    return pl.pallas_call(
        _merge_kernel,
        grid=(t // tm,),
        in_specs=[tok(D_MODEL), _resident(gpre, layer), _resident(gpost, layer), tok(BRANCH_W), tok(BRANCH_W),
                  tok(BRANCH_W), tok(BRANCH_W), _resident(wg, layer, (N_HEADS * D_MODEL, 0)), _resident(wbr, layer),
                  _resident(wo, layer)],
        out_specs=tok(D_MODEL),
        out_shape=jax.ShapeDtypeStruct((t, D_MODEL), F32),
        compiler_params=_params("parallel"),
        name="merge",
    )(x2, gpre, gpost, oa, ob, oc, od, wg, wbr, wo)


def _ffn_kernel(x_ref, gpre_ref, gpost_ref, wgu_ref, wdn_ref, y_ref):
    sub_rows = x_ref.shape[0] // TOK_SUBS
    for sub in range(TOK_SUBS):
        rows = slice(sub_rows * sub, sub_rows * (sub + 1))
        _ffn_rows(x_ref.at[rows], gpre_ref, gpost_ref, wgu_ref, wdn_ref, y_ref.at[rows])


def _ffn_rows(x_ref, gpre_ref, gpost_ref, wgu_ref, wdn_ref, y_ref):
    x = x_ref[...]
    h = (_rms(x) * gpre_ref[...]).astype(BF16)
    acc = None
    for c in range(D_FF // FF_CHUNK):
        lo = FF_CHUNK * c
        gate = jnp.dot(h, wgu_ref[:, lo:lo + FF_CHUNK], preferred_element_type=F32)
        up = jnp.dot(h, wgu_ref[:, D_FF + lo:D_FF + lo + FF_CHUNK], preferred_element_type=F32)
        act = (jax.nn.silu(gate) * up).astype(BF16)
        part = jnp.dot(act, wdn_ref[lo:lo + FF_CHUNK, :], preferred_element_type=F32)
        acc = part if acc is None else acc + part
    y_ref[...] = x + _rms(acc) * gpost_ref[...]


def _ffn(layer, x2, gpre, gpost, wgu, wdn):
    t = x2.shape[0]
    tm = TOK_TILE
    tok = pl.BlockSpec((tm, D_MODEL), lambda i: (i, 0))
    return pl.pallas_call(
        _ffn_kernel,
        grid=(t // tm,),
        in_specs=[tok, _resident(gpre, layer), _resident(gpost, layer), _resident(wgu, layer),
                  _resident(wdn, layer)],
        out_specs=tok,
        out_shape=jax.ShapeDtypeStruct((t, D_MODEL), F32),
        compiler_params=_params("parallel"),
        name="ffn",
    )(x2, gpre, gpost, wgu, wdn)


def _rope_tables():
    inv_freq = 1.0 / (ROPE_THETA ** (jnp.arange(ROPE_HALF, dtype=F32) * (1.0 / ROPE_HALF)))

    def group(pos):
        ang = pos.astype(F32)[:, None] * inv_freq[None, :]
        c, s = jnp.cos(ang), jnp.sin(ang)
        return jnp.concatenate([c, c], axis=-1), jnp.concatenate([-s, s], axis=-1)

    gc, gs = group(jnp.arange(GRID_W))
    rc, rs = jnp.repeat(gc, GRID_W, axis=0), jnp.repeat(gs, GRID_W, axis=0)
    cc, cs = jnp.tile(gc, (SEQ // GRID_W, 1)), jnp.tile(gs, (SEQ // GRID_W, 1))
    pc, ps = group(jnp.arange(SEQ))
    ones = jnp.ones((SEQ, 2 * ROPE_HALF), F32)
    zeros = jnp.zeros((SEQ, 2 * ROPE_HALF), F32)
    ctab = jnp.stack([jnp.concatenate([rc, cc, rc, cc], axis=-1),
                      jnp.concatenate([rs, cs, rs, cs], axis=-1)])
    dtab = jnp.stack([jnp.concatenate([ones, ones, pc, ones], axis=-1),
                      jnp.concatenate([zeros, zeros, ps, zeros], axis=-1)])
    return ctab, dtab


def _group_mean_matrix():
    i = np.arange(2 * LANES)
    m = (i[:, None] // HEAD_DIM) == (i[None, :] // HEAD_DIM)
    return jnp.asarray(m.astype(np.float32) / HEAD_DIM, dtype=BF16)


def _t5_bucket_index(rel):
    nb = T5_BUCKETS // 2
    max_exact = nb // 2
    ret = (rel > 0).astype(np.int32) * nb
    n = np.abs(rel)
    large = max_exact + (np.log(np.maximum(n, 1) / max_exact) / np.log(T5_MAX_DIST / max_exact)
                         * (nb - max_exact)).astype(np.int32)
    large = np.minimum(large, nb - 1)
    return ret + np.where(n < max_exact, n, large)


def _nbr_group_kinds():
    rows = SEQ // GRID_W
    ng = rows // A_QROWS
    kinds = []
    for g in (0, 1, ng - 1):
        ws = int(np.clip(A_QROWS * g - A_WIN_R // 2, 0, rows - A_KROWS))
        qr = A_QROWS * g + np.arange(A_QROWS)
        rs = np.clip(qr - A_WIN_R // 2, 0, rows - A_WIN_R)
        kr = ws + np.arange(A_KROWS)
        ok = (kr[None, :] >= rs[:, None]) & (kr[None, :] < rs[:, None] + A_WIN_R)
        kinds.append((ws - A_QROWS * g + A_WIN_R - 1, ok))
    return kinds


def _nbr_bias_kernel(r_ref, o_ref):
    lane = lax.broadcasted_iota(jnp.int32, (GRID_W, LANES), 1)
    qc = lax.broadcasted_iota(jnp.int32, (GRID_W, LANES), 0)
    kc = lane % GRID_W
    cs = jnp.clip(qc - A_WIN_C // 2, 0, GRID_W - A_WIN_C)
    col_ok = (kc >= cs) & (kc < cs + A_WIN_C)
    left = lane < GRID_W
    pairs = [jnp.where(col_ok, pltpu.roll(jnp.broadcast_to(r_ref[j:j + 1, :], (GRID_W, LANES)),
                                          LANES - (A_WIN_C - 1), 1, stride=1, stride_axis=0), NEG_INF)
             for j in range(2 * A_WIN_R)]
    masked = jnp.full((GRID_W, LANES), NEG_INF, F32)
    for t, (s, row_ok) in enumerate(_nbr_group_kinds()):
        for qr in range(A_QROWS):
            for p in range(A_KROWS // 2):
                j = 2 * p - qr + s + 1
                ok_a, ok_b = bool(row_ok[qr][2 * p]), bool(row_ok[qr][2 * p + 1])
                if not (ok_a or ok_b):
                    blk = masked
                elif ok_a and ok_b:
                    blk = pairs[j]
                else:
                    blk = jnp.where(left if ok_a else ~left, pairs[j], NEG_INF)
                o_ref[t, GRID_W * qr:GRID_W * (qr + 1), LANES * p:LANES * (p + 1)] = blk


def _nbr_bias(rpb):
    depth = rpb.shape[0]
    r = jnp.pad(rpb.astype(F32) * LOG2E, ((0, 0), (0, 0), (1, 1), (0, GRID_W - (2 * A_WIN_C - 1))))
    r = jnp.concatenate([r[:, :, :-1], r[:, :, 1:]], axis=-1)
    return pl.pallas_call(
        _nbr_bias_kernel,
        grid=(depth, N_HEADS),
        in_specs=[pl.BlockSpec((None, None, 2 * A_WIN_R, LANES), lambda l, h: (l, h, 0, 0))],
        out_specs=pl.BlockSpec((None, 3, None, A_TQ, A_TK), lambda l, h: (l, 0, h, 0, 0)),
        out_shape=jax.ShapeDtypeStruct((depth, 3, N_HEADS, A_TQ, A_TK), F32),
        compiler_params=_params("parallel", "parallel"),
        name="nbr_bias",
    )(r)


def _win_bias(t5_table):
    rel_values = np.arange(-B_WINDOW, B_WINDOW + 1)
    by_rel = t5_table.astype(F32)[_t5_bucket_index(rel_values)].T * LOG2E
    by_rel = jnp.pad(by_rel, ((0, 0), (LANES, LANES - 1)))[:, None, :]
    return pl.pallas_call(
        _win_bias_kernel,
        grid=(N_HEADS,),
        in_specs=[pl.BlockSpec((None, 1, 4 * LANES), lambda h: (h, 0, 0))],
        out_specs=pl.BlockSpec((3, None, B_TQ, B_TK), lambda h: (0, h, 0, 0)),
        out_shape=jax.ShapeDtypeStruct((3, N_HEADS, B_TQ, B_TK), F32),
        compiler_params=_params("parallel"),
        name="win_bias",
    )(by_rel)


def _win_bias_kernel(u_ref, o_ref):
    assert B_WINDOW == LANES
    nq = SEQ // B_TQ
    qr = lax.broadcasted_iota(jnp.int32, (LANES, LANES), 0)
    lane = lax.broadcasted_iota(jnp.int32, (LANES, LANES), 1)
    tiles = {}
    for d in (-LANES, 0, LANES):
        src = jnp.broadcast_to(u_ref[:, LANES + d:3 * LANES + d], (LANES, 2 * LANES))
        skew = pltpu.roll(src, 0, 1, stride=1, stride_axis=0)[:, LANES:]
        tiles[d] = jnp.where(jnp.abs(lane - qr + d) <= B_WINDOW, skew, NEG_INF)
    masked = jnp.full((LANES, LANES), NEG_INF, F32)
    for t, n in enumerate((0, 1, nq - 1)):
        start = int(np.clip(B_TQ * n - B_WINDOW, 0, SEQ - B_TK))
        for qb in range(B_TQ // LANES):
            for kb in range(B_TK // LANES):
                d = (start + LANES * kb) - (B_TQ * n + LANES * qb)
                o_ref[t, LANES * qb:LANES * (qb + 1), LANES * kb:LANES * (kb + 1)] = tiles.get(d, masked)


def _stacked_weights(w_in, c_q_norm, c_k_norm, d_w_q_up, d_w_kv_up):
    depth = w_in.shape[0]
    n_kr = W_MIX - LANES
    n_mix = n_kr + D_ROPE
    qs = HEAD_DIM ** -0.5 * LOG2E
    gate_w = w_in.shape[2] - n_mix
    zeros = jnp.zeros((depth, D_MODEL, W_MIX_COL * W_MIX - gate_w), BF16)
    w_kr = jnp.pad(w_in[:, :, n_kr:n_mix], ((0, 0), (0, 0), (D_NOPE, LANES - D_NOPE - D_ROPE)))
    w = jnp.concatenate([w_in[:, :, n_mix:].astype(BF16), zeros, w_in[:, :, :n_kr].astype(BF16),
                         w_kr.astype(BF16)], axis=2)
    cgain = jnp.concatenate([jnp.tile(c_q_norm * qs, (1, N_HEADS)), jnp.tile(c_k_norm, (1, KV_GROUPED))],
                            axis=1).astype(F32)[:, None, :]

    def pad_heads(a, dim):
        return jnp.pad(a, ((0, 0), (0, 0), (0, 0), (0, LANES - dim))).reshape(depth, a.shape[1], QK_W)

    dwq = pad_heads(d_w_q_up.reshape(depth, D_Q_LORA, N_HEADS, D_NOPE + D_ROPE), D_NOPE + D_ROPE)
    kvu = d_w_kv_up.reshape(depth, D_KV_LORA, N_HEADS, D_NOPE + D_V)
    dwkv = jnp.concatenate([pad_heads(kvu[..., :D_NOPE], D_NOPE), pad_heads(kvu[..., D_NOPE:], D_V)], axis=2)
    return w, cgain, dwq.astype(BF16), dwkv.astype(BF16)


def kernel(x, w_in, a_rpb, b_sink, t5_bias, c_q_norm, c_k_norm, d_q_norm, d_kv_norm, d_w_q_up, d_w_kv_up,
           w_branch, w_out, ln_pre_mix, ln_post_mix, ln_pre_ffn, ln_post_ffn, ffn_w_gu, ffn_w_down):
    b, s, d = x.shape
    assert (s, d) == (SEQ, D_MODEL)
    depth = w_in.shape[0]
    x2 = x.reshape(b * s, d)
    ctab, dtab = _rope_tables()
    gmat = _group_mean_matrix()
    wbias = _win_bias(t5_bias)
    abias = _nbr_bias(a_rpb)
    w, cgain, dwq, dwkv = _stacked_weights(w_in, c_q_norm, c_k_norm, d_w_q_up, d_w_kv_up)
    wbr, wo, wgu, wdn = (a.astype(BF16) for a in (w_branch, w_out, ffn_w_gu, ffn_w_down))
    g_pre_mix, g_post_mix, g_pre_ffn, g_post_ffn, dqn, dkvn = (
        v.astype(F32)[:, None, :] for v in (ln_pre_mix, ln_post_mix, ln_pre_ffn, ln_post_ffn, d_q_norm, d_kv_norm))
    sink = b_sink.astype(F32) * LOG2E
    for l in range(depth):
        (aq, ak, av, bq, bk, bv, cq, ck, cv, dq, dk, dv) = _proj(
            l, x2, g_pre_mix, w, gmat, cgain, ctab, dtab, dqn, dkvn, dwq, dwkv)
        oa, ob = _local_attn(l, sink[l], aq, ak, av, abias, bq, bk, bv, wbias)
        oc = _dense_attn(cq, ck, cv, KV_GROUPED, "dense_c")
        od = _dense_attn(dq, dk, dv, N_HEADS, "dense_d")
        x2 = _merge(l, x2, g_pre_mix, g_post_mix, oa, ob, oc, od, w, wbr, wo)
        x2 = _ffn(l, x2, g_pre_ffn, g_post_ffn, wgu, wdn)
    return x2.reshape(b, s, d)
```

```python
import functools

import jax
import jax.numpy as jnp
import numpy as np
from jax import lax
from jax.experimental import pallas as pl
from jax.experimental.pallas import tpu as pltpu

D_MODEL = 1024
SEQ = 4096
GRID_W = 64
HEAD_DIM = 64
N_HEADS = 4
KV_GROUPED = 2
EPS = 1e-6
NEG_INF = -1e30
ROPE_THETA = 10000.0
A_WIN_R = 8
A_WIN_C = 16
B_WINDOW = 128
T5_BUCKETS = 32
T5_MAX_DIST = 128
D_Q_LORA = 256
D_KV_LORA = 128
D_NOPE = 64
D_ROPE = 32
D_V = 64
D_FF = 2816
BRANCH_W = 256

LANES = 128
QK_W = N_HEADS * LANES
KG_W = KV_GROUPED * LANES
ROPE_HALF = 16
LOG2E = 1.4426950408889634

A_QROWS = 4
A_KROWS = 12
A_TQ = A_QROWS * GRID_W
A_TK = A_KROWS * GRID_W
B_TQ = 256
B_TK = B_TQ + 2 * B_WINDOW
LOCAL_SUBS = 2
LOCAL_A_KEYS = A_TK + (LOCAL_SUBS - 1) * A_TQ
LOCAL_B_KEYS = B_TK + (LOCAL_SUBS - 1) * B_TQ
DENSE_TQ = 512
DENSE_SUB = 256
CHAIN_LOOKAHEAD = 2
TOK_TILE = 1024
FF_CHUNK = 256
MERGE_CHUNK = 256
PROJ_SUBS = 2
TOK_SUBS = 2
W_MIX = 9 * BRANCH_W
W_MIX_COL = 2
VMEM_LIMIT = 56 * 1024 * 1024

BF16 = jnp.bfloat16
F32 = jnp.float32


def _resident(arr, layer=None, cols=None):
    if layer is None:
        return pl.BlockSpec(arr.shape, lambda *_: (0,) * arr.ndim, pipeline_mode=pl.Buffered(1))
    width, col = cols if cols is not None else (arr.shape[-1], 0)
    return pl.BlockSpec((None,) + arr.shape[1:-1] + (width,), lambda *_: (layer,) + (0,) * (arr.ndim - 2) + (col,),
                        pipeline_mode=pl.Buffered(1))


def _params(*sem):
    return pltpu.CompilerParams(dimension_semantics=sem, vmem_limit_bytes=VMEM_LIMIT)


def _rms(x):
    return x * lax.rsqrt(jnp.mean(x * x, axis=-1, keepdims=True) + EPS)


def _rope128(x, cos, sin_signed):
    lane = lax.broadcasted_iota(jnp.int32, x.shape, 1)
    first = (lane % (2 * ROPE_HALF)) < ROPE_HALF
    rot = jnp.where(first, pltpu.roll(x, LANES - ROPE_HALF, 1), pltpu.roll(x, ROPE_HALF, 1))
    return x * cos + rot * sin_signed


def _with_ones(v):
    lane = lax.broadcasted_iota(jnp.int32, v.shape, 1)
    return jnp.where(lane % LANES == HEAD_DIM, 1.0, v)


def _spread_heads(o_ref, x, ones=False):
    lane = lax.broadcasted_iota(jnp.int32, (x.shape[0], LANES), 1)
    low = lane < HEAD_DIM
    fill = jnp.where(lane == HEAD_DIM, 1.0, 0.0) if ones else 0.0
    for c in range(x.shape[1] // LANES):
        pair = x[:, LANES * c:LANES * (c + 1)]
        o_ref[:, 2 * LANES * c:2 * LANES * c + LANES] = jnp.where(low, pair, fill).astype(BF16)
        o_ref[:, 2 * LANES * c + LANES:2 * LANES * (c + 1)] = (
            jnp.where(low, pltpu.roll(pair, HEAD_DIM, 1), fill).astype(BF16))


def _proj_kernel(x_ref, g_ref, w_ref, gmat_ref, cgain_ref, ctab_ref, dtab_ref,
                 dqn_ref, dkvn_ref, dwq_ref, dwkv_ref, *out_refs):
    sub_rows = x_ref.shape[0] // PROJ_SUBS
    for sub in range(PROJ_SUBS):
        rows = slice(sub_rows * sub, sub_rows * (sub + 1))
        _proj_rows(x_ref.at[rows], g_ref, w_ref, gmat_ref, cgain_ref, ctab_ref.at[:, rows], dtab_ref.at[:, rows],
                   dqn_ref, dkvn_ref, dwq_ref, dwkv_ref, *(o.at[rows] for o in out_refs))


def _proj_rows(x_ref, g_ref, w_ref, gmat_ref, cgain_ref, ctab_ref, dtab_ref,
               dqn_ref, dkvn_ref, dwq_ref, dwkv_ref,
               aq_ref, ak_ref, av_ref, bq_ref, bk_ref, bv_ref, cq_ref, ck_ref, cv_ref,
               dq_ref, dk_ref, dv_ref):
    h = (_rms(x_ref[...]) * g_ref[...]).astype(BF16)
    qs = HEAD_DIM ** -0.5 * LOG2E
    a0, b0, c0, d0 = 0, 3 * BRANCH_W, 5 * BRANCH_W, 7 * BRANCH_W

    pd = jnp.dot(h, w_ref[:, d0:], preferred_element_type=F32)
    dcos = dtab_ref[0]
    dsin = dtab_ref[1]
    cq = (_rms(pd[:, :D_Q_LORA]) * dqn_ref[...]).astype(BF16)
    qd = jnp.dot(cq, dwq_ref[...], preferred_element_type=F32)
    ckv = (_rms(pd[:, D_Q_LORA:D_Q_LORA + D_KV_LORA]) * dkvn_ref[...]).astype(BF16)
    kvd = jnp.dot(ckv, dwkv_ref[...], preferred_element_type=F32)
    kr = _rope128(pd[:, D_Q_LORA + D_KV_LORA:], dcos, dsin)
    scale = (D_NOPE + D_ROPE) ** -0.5 * LOG2E
    for hd in range(N_HEADS):
        sl = slice(LANES * hd, LANES * (hd + 1))
        dq_ref[:, sl] = (_rope128(qd[:, sl], dcos, dsin) * scale).astype(BF16)
        dk_ref[:, sl] = (kvd[:, sl] + kr).astype(BF16)
    dv_ref[...] = _with_ones(kvd[:, QK_W:]).astype(BF16)

    pa = jnp.dot(h, w_ref[:, a0:b0], preferred_element_type=F32)
    _spread_heads(aq_ref, pa[:, :BRANCH_W] * qs)
    _spread_heads(ak_ref, pa[:, BRANCH_W:2 * BRANCH_W])
    _spread_heads(av_ref, pa[:, 2 * BRANCH_W:], ones=True)

    pb = jnp.dot(h, w_ref[:, b0:c0], preferred_element_type=F32)
    _spread_heads(bq_ref, pb[:, :BRANCH_W] * qs)
    _spread_heads(bk_ref, pb[:, BRANCH_W:BRANCH_W + LANES])
    _spread_heads(bv_ref, pb[:, BRANCH_W + LANES:], ones=True)

    pc = jnp.dot(h, w_ref[:, c0:d0], preferred_element_type=F32)
    _spread_heads(cv_ref, pc[:, BRANCH_W + LANES:], ones=True)
    ccos = ctab_ref[0]
    csin = ctab_ref[1]

    def norm_rope(y, gmat, gain):
        sq = y * y
        hi = sq.astype(BF16)
        lo = (sq - hi.astype(F32)).astype(BF16)
        ms = jnp.dot(hi, gmat, preferred_element_type=F32) + jnp.dot(lo, gmat, preferred_element_type=F32)
        yn = y * lax.rsqrt(ms + EPS) * gain
        return jnp.concatenate([_rope128(yn[:, LANES * c:LANES * (c + 1)], ccos, csin)
                                for c in range(y.shape[1] // LANES)], axis=1)

    _spread_heads(cq_ref, norm_rope(pc[:, :BRANCH_W], gmat_ref[...], cgain_ref[:, :BRANCH_W]))
    _spread_heads(ck_ref, norm_rope(pc[:, BRANCH_W:BRANCH_W + LANES], gmat_ref[:LANES, :LANES],
                                    cgain_ref[:, BRANCH_W:]))


def _proj(layer, x2, g, w, gmat, cgain, ctab, dtab, dqn, dkvn, dwq, dwkv):
    t = x2.shape[0]
    tm = TOK_TILE
    seq_tiles = SEQ // tm

    def tok(w):
        return pl.BlockSpec((tm, w), lambda i: (i, 0))

    def tab():
        return pl.BlockSpec((2, tm, LANES), lambda i: (0, i % seq_tiles, 0))

    out_w = (QK_W, QK_W, QK_W, QK_W, KG_W, KG_W, QK_W, KG_W, KG_W, QK_W, QK_W, QK_W)
    return pl.pallas_call(
        _proj_kernel,
        grid=(t // tm,),
        in_specs=[tok(D_MODEL), _resident(g, layer), _resident(w, layer, (W_MIX, W_MIX_COL)), _resident(gmat),
                  _resident(cgain, layer), tab(), tab(), _resident(dqn, layer), _resident(dkvn, layer),
                  _resident(dwq, layer), _resident(dwkv, layer)],
        out_specs=[tok(ow) for ow in out_w],
        out_shape=[jax.ShapeDtypeStruct((t, ow), BF16) for ow in out_w],
        compiler_params=_params("parallel"),
        name="proj",
    )(x2, g, w, gmat, cgain, ctab, dtab, dqn, dkvn, dwq, dwkv)


def _scores(q, k, bias=None):
    s = lax.dot_general(q, k, (((1,), (1,)), ((), ())), preferred_element_type=F32)
    return s if bias is None else s + bias


def _softmax_pv(s, v_ref, rows, kv_head, pv_heads, sink=None):
    slab, blk = divmod(kv_head, pv_heads)
    v = v_ref[rows, pv_heads * LANES * slab:pv_heads * LANES * (slab + 1)]
    m = jnp.max(s, axis=-1, keepdims=True)
    if sink is not None:
        m = jnp.maximum(m, sink)
    pv = jnp.dot(jnp.exp2(s - m).astype(BF16), v, preferred_element_type=F32)[:, LANES * blk:LANES * (blk + 1)]
    total = pv[:, HEAD_DIM:HEAD_DIM + 1]
    if sink is not None:
        total = total + jnp.exp2(sink - m)
    return pv * (1.0 / total)


def _run_chains(chains):
    outs = []
    ready = [chains[i][0]() for i in range(min(CHAIN_LOOKAHEAD, len(chains)))]
    for i, (_, finish) in enumerate(chains):
        if i + CHAIN_LOOKAHEAD < len(chains):
            ready.append(chains[i + CHAIN_LOOKAHEAD][0]())
        outs.append(finish(ready.pop(0)))
    return outs


def _store_heads(o_ref, outs, rows=slice(None)):
    low = lax.broadcasted_iota(jnp.int32, outs[0].shape, 1) < HEAD_DIM
    for g in range(N_HEADS // 2):
        pair = jnp.where(low, outs[2 * g], pltpu.roll(outs[2 * g + 1], HEAD_DIM, 1))
        o_ref[rows, LANES * g:LANES * (g + 1)] = pair.astype(BF16)


def _a_window(blk):
    return jnp.clip(A_QROWS * blk - A_WIN_R // 2, 0, GRID_W - A_KROWS) * GRID_W


def _b_window(blk):
    return jnp.clip(B_TQ * blk - B_WINDOW, 0, SEQ - B_TK)


def _local_kernel(sink_ref, aq_ref, ak_ref, av_ref, abias0_ref, abias1_ref, bq_ref, bk_ref, bv_ref,
                  bbias0_ref, bbias1_ref, oa_ref, ob_ref):
    first = LOCAL_SUBS * pl.program_id(1)
    a_base = jnp.minimum(_a_window(first), SEQ - LOCAL_A_KEYS)
    b_base = jnp.minimum(_b_window(first), SEQ - LOCAL_B_KEYS)
    chains = []
    for sub, (abias_ref, bbias_ref) in enumerate(((abias0_ref, bbias0_ref), (abias1_ref, bbias1_ref))):
        q_rows = slice(A_TQ * sub, A_TQ * (sub + 1))
        a_rows = pl.ds(pl.multiple_of(_a_window(first + sub) - a_base, A_TQ), A_TK)
        b_rows = pl.ds(pl.multiple_of(_b_window(first + sub) - b_base, B_WINDOW), B_TK)
        for h in range(N_HEADS):
            sl = slice(LANES * h, LANES * (h + 1))
            kv = h // (N_HEADS // KV_GROUPED)
            kvl = slice(LANES * kv, LANES * (kv + 1))
            chains.append((
                lambda h=h, sl=sl, q_rows=q_rows, a_rows=a_rows, abias_ref=abias_ref:
                    _scores(aq_ref[q_rows, sl], ak_ref[a_rows, sl], abias_ref[0, h]),
                lambda s, h=h, a_rows=a_rows: _softmax_pv(s, av_ref, a_rows, h, 2)))
            chains.append((
                lambda h=h, sl=sl, kvl=kvl, q_rows=q_rows, b_rows=b_rows, bbias_ref=bbias_ref:
                    _scores(bq_ref[q_rows, sl], bk_ref[b_rows, kvl], bbias_ref[0, h]),
                lambda s, h=h, kv=kv, b_rows=b_rows: _softmax_pv(s, bv_ref, b_rows, kv, 2, sink=sink_ref[h])))
    outs = _run_chains(chains)
    for sub in range(LOCAL_SUBS):
        q_rows = slice(A_TQ * sub, A_TQ * (sub + 1))
        mine = outs[2 * N_HEADS * sub:2 * N_HEADS * (sub + 1)]
        _store_heads(oa_ref, mine[0::2], q_rows)
        _store_heads(ob_ref, mine[1::2], q_rows)


def _local_attn(layer, sink, aq, ak, av, abias, bq, bk, bv, bbias):
    assert A_TQ == B_TQ and LOCAL_SUBS == 2
    t = aq.shape[0]
    nb = t // SEQ
    nblk = SEQ // A_TQ
    nj = nblk // LOCAL_SUBS

    def tok(w):
        return pl.BlockSpec((LOCAL_SUBS * A_TQ, w), lambda b, j: (b * nj + j, 0))

    def keys(n, w, window):
        return pl.BlockSpec((pl.Element(n), pl.Element(w)),
                            lambda b, j: (pl.multiple_of(b * SEQ + jnp.minimum(window(LOCAL_SUBS * j), SEQ - n),
                                                         B_WINDOW), 0))

    def kind(j, sub):
        blk = LOCAL_SUBS * j + sub
        return (blk > 0).astype(jnp.int32) + (blk == nblk - 1).astype(jnp.int32)

    def a_bias(sub):
        return pl.BlockSpec((None, 1, N_HEADS, A_TQ, A_TK), lambda b, j: (layer, kind(j, sub), 0, 0, 0))

    def b_bias(sub):
        return pl.BlockSpec((1, N_HEADS, B_TQ, B_TK), lambda b, j: (kind(j, sub), 0, 0, 0))

    return pl.pallas_call(
        _local_kernel,
        grid=(nb, nj),
        in_specs=[pl.BlockSpec(memory_space=pltpu.SMEM),
                  tok(QK_W), keys(LOCAL_A_KEYS, QK_W, _a_window), keys(LOCAL_A_KEYS, QK_W, _a_window),
                  a_bias(0), a_bias(1),
                  tok(QK_W), keys(LOCAL_B_KEYS, KG_W, _b_window), keys(LOCAL_B_KEYS, KG_W, _b_window),
                  b_bias(0), b_bias(1)],
        out_specs=[tok(BRANCH_W), tok(BRANCH_W)],
        out_shape=[jax.ShapeDtypeStruct((t, BRANCH_W), BF16)] * 2,
        compiler_params=_params("parallel", "arbitrary"),
        name="local_attn",
    )(sink, aq, ak, av, abias, abias, bq, bk, bv, bbias, bbias)


def _dense_kernel(q_ref, k_ref, v_ref, o_ref, *, kv_heads):
    keys = slice(None)
    subs = [slice(DENSE_SUB * i, DENSE_SUB * (i + 1)) for i in range(DENSE_TQ // DENSE_SUB)]
    chains = []
    for rows in subs:
        for h in range(N_HEADS):
            kv = h // (N_HEADS // kv_heads)
            chains.append((lambda rows=rows, h=h, kv=kv: _scores(q_ref[rows, LANES * h:LANES * (h + 1)],
                                                                 k_ref[:, LANES * kv:LANES * (kv + 1)]),
                           lambda s, kv=kv: _softmax_pv(s, v_ref, keys, kv, 1)))
    outs = _run_chains(chains)
    for i, rows in enumerate(subs):
        _store_heads(o_ref, outs[N_HEADS * i:N_HEADS * (i + 1)], rows)


def _dense_attn(q, k, v, kv_heads, name):
    t = q.shape[0]
    nb = t // SEQ
    nq = SEQ // DENSE_TQ
    return pl.pallas_call(
        functools.partial(_dense_kernel, kv_heads=kv_heads),
        grid=(nb, nq),
        in_specs=[pl.BlockSpec((DENSE_TQ, QK_W), lambda b, n: (b * nq + n, 0)),
                  pl.BlockSpec((SEQ, kv_heads * LANES), lambda b, n: (b, 0)),
                  pl.BlockSpec((SEQ, kv_heads * LANES), lambda b, n: (b, 0))],
        out_specs=pl.BlockSpec((DENSE_TQ, BRANCH_W), lambda b, n: (b * nq + n, 0)),
        out_shape=jax.ShapeDtypeStruct((t, BRANCH_W), BF16),
        compiler_params=_params("parallel", "arbitrary"),
        name=name,
    )(q, k, v)


def _merge_kernel(x_ref, gpre_ref, gpost_ref, oa_ref, ob_ref, oc_ref, od_ref, wg_ref, wbr_ref, wo_ref, y_ref):
    x = x_ref[...]
    h = (_rms(x) * gpre_ref[...]).astype(BF16)
    m = None
    for c in range(D_MODEL // MERGE_CHUNK):
        cols = slice(MERGE_CHUNK * c, MERGE_CHUNK * (c + 1))
        merged = None
        for n, o_ref in enumerate((oa_ref, ob_ref, oc_ref, od_ref)):
            logits = jnp.dot(h, wg_ref[:, D_MODEL * n + MERGE_CHUNK * c:D_MODEL * n + MERGE_CHUNK * (c + 1)],
                             preferred_element_type=F32)
            y = jnp.dot(o_ref[...], wbr_ref[n, :, cols], preferred_element_type=F32)
            term = jax.nn.sigmoid(logits) * y
            merged = term if merged is None else merged + term
        part = jnp.dot(merged.astype(BF16), wo_ref[cols, :], preferred_element_type=F32)
        m = part if m is None else m + part
    y_ref[...] = x + _rms(m) * gpost_ref[...]


def _merge(layer, x2, gpre, gpost, oa, ob, oc, od, wg, wbr, wo):
    t = x2.shape[0]
    tm = TOK_TILE

    def tok(w):
        return pl.BlockSpec((tm, w), lambda i: (i, 0))

    return pl.pallas_call(
        _merge_kernel,
        grid=(t // tm,),
        in_specs=[tok(D_MODEL), _resident(gpre, layer), _resident(gpost, layer), tok(BRANCH_W), tok(BRANCH_W),
                  tok(BRANCH_W), tok(BRANCH_W), _resident(wg, layer, (N_HEADS * D_MODEL, 0)), _resident(wbr, layer),
                  _resident(wo, layer)],
        out_specs=tok(D_MODEL),
        out_shape=jax.ShapeDtypeStruct((t, D_MODEL), F32),
        compiler_params=_params("parallel"),
        name="merge",
    )(x2, gpre, gpost, oa, ob, oc, od, wg, wbr, wo)


def _ffn_kernel(x_ref, gpre_ref, gpost_ref, wgu_ref, wdn_ref, y_ref):
    sub_rows = x_ref.shape[0] // TOK_SUBS
    for sub in range(TOK_SUBS):
        rows = slice(sub_rows * sub, sub_rows * (sub + 1))
        _ffn_rows(x_ref.at[rows], gpre_ref, gpost_ref, wgu_ref, wdn_ref, y_ref.at[rows])


def _ffn_rows(x_ref, gpre_ref, gpost_ref, wgu_ref, wdn_ref, y_ref):
    x = x_ref[...]
    h = (_rms(x) * gpre_ref[...]).astype(BF16)
    acc = None
    for c in range(D_FF // FF_CHUNK):
        lo = FF_CHUNK * c
        gate = jnp.dot(h, wgu_ref[:, lo:lo + FF_CHUNK], preferred_element_type=F32)
        up = jnp.dot(h, wgu_ref[:, D_FF + lo:D_FF + lo + FF_CHUNK], preferred_element_type=F32)
        act = (jax.nn.silu(gate) * up).astype(BF16)
        part = jnp.dot(act, wdn_ref[lo:lo + FF_CHUNK, :], preferred_element_type=F32)
        acc = part if acc is None else acc + part
    y_ref[...] = x + _rms(acc) * gpost_ref[...]


def _ffn(layer, x2, gpre, gpost, wgu, wdn):
    t = x2.shape[0]
    tm = TOK_TILE
    tok = pl.BlockSpec((tm, D_MODEL), lambda i: (i, 0))
    return pl.pallas_call(
        _ffn_kernel,
        grid=(t // tm,),
        in_specs=[tok, _resident(gpre, layer), _resident(gpost, layer), _resident(wgu, layer),
                  _resident(wdn, layer)],
        out_specs=tok,
        out_shape=jax.ShapeDtypeStruct((t, D_MODEL), F32),
        compiler_params=_params("parallel"),
        name="ffn",
    )(x2, gpre, gpost, wgu, wdn)


def _rope_tables():
    inv_freq = 1.0 / (ROPE_THETA ** (jnp.arange(ROPE_HALF, dtype=F32) * (1.0 / ROPE_HALF)))

    def group(pos):
        ang = pos.astype(F32)[:, None] * inv_freq[None, :]
        c, s = jnp.cos(ang), jnp.sin(ang)
        return jnp.concatenate([c, c], axis=-1), jnp.concatenate([-s, s], axis=-1)

    gc, gs = group(jnp.arange(GRID_W))
    rc, rs = jnp.repeat(gc, GRID_W, axis=0), jnp.repeat(gs, GRID_W, axis=0)
    cc, cs = jnp.tile(gc, (SEQ // GRID_W, 1)), jnp.tile(gs, (SEQ // GRID_W, 1))
    pc, ps = group(jnp.arange(SEQ))
    ones = jnp.ones((SEQ, 2 * ROPE_HALF), F32)
    zeros = jnp.zeros((SEQ, 2 * ROPE_HALF), F32)
    ctab = jnp.stack([jnp.concatenate([rc, cc, rc, cc], axis=-1),
                      jnp.concatenate([rs, cs, rs, cs], axis=-1)])
    dtab = jnp.stack([jnp.concatenate([ones, ones, pc, ones], axis=-1),
                      jnp.concatenate([zeros, zeros, ps, zeros], axis=-1)])
    return ctab, dtab


def _group_mean_matrix():
    i = np.arange(2 * LANES)
    m = (i[:, None] // HEAD_DIM) == (i[None, :] // HEAD_DIM)
    return jnp.asarray(m.astype(np.float32) / HEAD_DIM, dtype=BF16)


def _t5_bucket_index(rel):
    nb = T5_BUCKETS // 2
    max_exact = nb // 2
    ret = (rel > 0).astype(np.int32) * nb
    n = np.abs(rel)
    large = max_exact + (np.log(np.maximum(n, 1) / max_exact) / np.log(T5_MAX_DIST / max_exact)
                         * (nb - max_exact)).astype(np.int32)
    large = np.minimum(large, nb - 1)
    return ret + np.where(n < max_exact, n, large)


def _nbr_group_kinds():
    rows = SEQ // GRID_W
    ng = rows // A_QROWS
    kinds = []
    for g in (0, 1, ng - 1):
        ws = int(np.clip(A_QROWS * g - A_WIN_R // 2, 0, rows - A_KROWS))
        qr = A_QROWS * g + np.arange(A_QROWS)
        rs = np.clip(qr - A_WIN_R // 2, 0, rows - A_WIN_R)
        kr = ws + np.arange(A_KROWS)
        ok = (kr[None, :] >= rs[:, None]) & (kr[None, :] < rs[:, None] + A_WIN_R)
        kinds.append((ws - A_QROWS * g + A_WIN_R - 1, ok))
    return kinds


def _nbr_bias_kernel(r_ref, o_ref):
    lane = lax.broadcasted_iota(jnp.int32, (GRID_W, LANES), 1)
    qc = lax.broadcasted_iota(jnp.int32, (GRID_W, LANES), 0)
    kc = lane % GRID_W
    cs = jnp.clip(qc - A_WIN_C // 2, 0, GRID_W - A_WIN_C)
    col_ok = (kc >= cs) & (kc < cs + A_WIN_C)
    left = lane < GRID_W
    pairs = [jnp.where(col_ok, pltpu.roll(jnp.broadcast_to(r_ref[j:j + 1, :], (GRID_W, LANES)),
                                          LANES - (A_WIN_C - 1), 1, stride=1, stride_axis=0), NEG_INF)
             for j in range(2 * A_WIN_R)]
    masked = jnp.full((GRID_W, LANES), NEG_INF, F32)
    for t, (s, row_ok) in enumerate(_nbr_group_kinds()):
        for qr in range(A_QROWS):
            for p in range(A_KROWS // 2):
                j = 2 * p - qr + s + 1
                ok_a, ok_b = bool(row_ok[qr][2 * p]), bool(row_ok[qr][2 * p + 1])
                if not (ok_a or ok_b):
                    blk = masked
                elif ok_a and ok_b:
                    blk = pairs[j]
                else:
                    blk = jnp.where(left if ok_a else ~left, pairs[j], NEG_INF)
                o_ref[t, GRID_W * qr:GRID_W * (qr + 1), LANES * p:LANES * (p + 1)] = blk


def _nbr_bias(rpb):
    depth = rpb.shape[0]
    r = jnp.pad(rpb.astype(F32) * LOG2E, ((0, 0), (0, 0), (1, 1), (0, GRID_W - (2 * A_WIN_C - 1))))
    r = jnp.concatenate([r[:, :, :-1], r[:, :, 1:]], axis=-1)
    return pl.pallas_call(
        _nbr_bias_kernel,
        grid=(depth, N_HEADS),
        in_specs=[pl.BlockSpec((None, None, 2 * A_WIN_R, LANES), lambda l, h: (l, h, 0, 0))],
        out_specs=pl.BlockSpec((None, 3, None, A_TQ, A_TK), lambda l, h: (l, 0, h, 0, 0)),
        out_shape=jax.ShapeDtypeStruct((depth, 3, N_HEADS, A_TQ, A_TK), F32),
        compiler_params=_params("parallel", "parallel"),
        name="nbr_bias",
    )(r)


def _win_bias(t5_table):
    rel_values = np.arange(-B_WINDOW, B_WINDOW + 1)
    by_rel = t5_table.astype(F32)[_t5_bucket_index(rel_values)].T * LOG2E
    by_rel = jnp.pad(by_rel, ((0, 0), (LANES, LANES - 1)))[:, None, :]
    return pl.pallas_call(
        _win_bias_kernel,
        grid=(N_HEADS,),
        in_specs=[pl.BlockSpec((None, 1, 4 * LANES), lambda h: (h, 0, 0))],
        out_specs=pl.BlockSpec((3, None, B_TQ, B_TK), lambda h: (0, h, 0, 0)),
        out_shape=jax.ShapeDtypeStruct((3, N_HEADS, B_TQ, B_TK), F32),
        compiler_params=_params("parallel"),
        name="win_bias",
    )(by_rel)


def _win_bias_kernel(u_ref, o_ref):
    assert B_WINDOW == LANES
    nq = SEQ // B_TQ
    qr = lax.broadcasted_iota(jnp.int32, (LANES, LANES), 0)
    lane = lax.broadcasted_iota(jnp.int32, (LANES, LANES), 1)
    tiles = {}
    for d in (-LANES, 0, LANES):
        src = jnp.broadcast_to(u_ref[:, LANES + d:3 * LANES + d], (LANES, 2 * LANES))
        skew = pltpu.roll(src, 0, 1, stride=1, stride_axis=0)[:, LANES:]
        tiles[d] = jnp.where(jnp.abs(lane - qr + d) <= B_WINDOW, skew, NEG_INF)
    masked = jnp.full((LANES, LANES), NEG_INF, F32)
    for t, n in enumerate((0, 1, nq - 1)):
        start = int(np.clip(B_TQ * n - B_WINDOW, 0, SEQ - B_TK))
        for qb in range(B_TQ // LANES):
            for kb in range(B_TK // LANES):
                d = (start + LANES * kb) - (B_TQ * n + LANES * qb)
                o_ref[t, LANES * qb:LANES * (qb + 1), LANES * kb:LANES * (kb + 1)] = tiles.get(d, masked)


def _stacked_weights(w_in, c_q_norm, c_k_norm, d_w_q_up, d_w_kv_up):
    depth = w_in.shape[0]
    n_kr = W_MIX - LANES
    n_mix = n_kr + D_ROPE
    qs = HEAD_DIM ** -0.5 * LOG2E
    gate_w = w_in.shape[2] - n_mix
    zeros = jnp.zeros((depth, D_MODEL, W_MIX_COL * W_MIX - gate_w), BF16)
    w_kr = jnp.pad(w_in[:, :, n_kr:n_mix], ((0, 0), (0, 0), (D_NOPE, LANES - D_NOPE - D_ROPE)))
    w = jnp.concatenate([w_in[:, :, n_mix:].astype(BF16), zeros, w_in[:, :, :n_kr].astype(BF16),
                         w_kr.astype(BF16)], axis=2)
    cgain = jnp.concatenate([jnp.tile(c_q_norm * qs, (1, N_HEADS)), jnp.tile(c_k_norm, (1, KV_GROUPED))],
                            axis=1).astype(F32)[:, None, :]

    def pad_heads(a, dim):
        return jnp.pad(a, ((0, 0), (0, 0), (0, 0), (0, LANES - dim))).reshape(depth, a.shape[1], QK_W)

    dwq = pad_heads(d_w_q_up.reshape(depth, D_Q_LORA, N_HEADS, D_NOPE + D_ROPE), D_NOPE + D_ROPE)
    kvu = d_w_kv_up.reshape(depth, D_KV_LORA, N_HEADS, D_NOPE + D_V)
    dwkv = jnp.concatenate([pad_heads(kvu[..., :D_NOPE], D_NOPE), pad_heads(kvu[..., D_NOPE:], D_V)], axis=2)
    return w, cgain, dwq.astype(BF16), dwkv.astype(BF16)


def kernel(x, w_in, a_rpb, b_sink, t5_bias, c_q_norm, c_k_norm, d_q_norm, d_kv_norm, d_w_q_up, d_w_kv_up,
           w_branch, w_out, ln_pre_mix, ln_post_mix, ln_pre_ffn, ln_post_ffn, ffn_w_gu, ffn_w_down):
    b, s, d = x.shape
    assert (s, d) == (SEQ, D_MODEL)
    depth = w_in.shape[0]
    x2 = x.reshape(b * s, d)
    ctab, dtab = _rope_tables()
    gmat = _group_mean_matrix()
    wbias = _win_bias(t5_bias)
    abias = _nbr_bias(a_rpb)
    w, cgain, dwq, dwkv = _stacked_weights(w_in, c_q_norm, c_k_norm, d_w_q_up, d_w_kv_up)
    wbr, wo, wgu, wdn = (a.astype(BF16) for a in (w_branch, w_out, ffn_w_gu, ffn_w_down))
    g_pre_mix, g_post_mix, g_pre_ffn, g_post_ffn, dqn, dkvn = (
        v.astype(F32)[:, None, :] for v in (ln_pre_mix, ln_post_mix, ln_pre_ffn, ln_post_ffn, d_q_norm, d_kv_norm))
    sink = b_sink.astype(F32) * LOG2E
    for l in range(depth):
        (aq, ak, av, bq, bk, bv, cq, ck, cv, dq, dk, dv) = _proj(
            l, x2, g_pre_mix, w, gmat, cgain, ctab, dtab, dqn, dkvn, dwq, dwkv)
        oa, ob = _local_attn(l, sink[l], aq, ak, av, abias, bq, bk, bv, wbias)
        oc = _dense_attn(cq, ck, cv, KV_GROUPED, "dense_c")
        od = _dense_attn(dq, dk, dv, N_HEADS, "dense_d")
        x2 = _merge(l, x2, g_pre_mix, g_post_mix, oa, ob, oc, od, w, wbr, wo)
        x2 = _ffn(l, x2, g_pre_ffn, g_post_ffn, wgu, wdn)
    return x2.reshape(b, s, d)
```

```python
import functools

import jax
import jax.numpy as jnp
import numpy as np
from jax import lax
from jax.experimental import pallas as pl
from jax.experimental.pallas import tpu as pltpu

D_MODEL = 1024
SEQ = 4096
GRID_W = 64
HEAD_DIM = 64
N_HEADS = 4
KV_GROUPED = 2
EPS = 1e-6
NEG_INF = -1e30
ROPE_THETA = 10000.0
A_WIN_R = 8
A_WIN_C = 16
B_WINDOW = 128
T5_BUCKETS = 32
T5_MAX_DIST = 128
D_Q_LORA = 256
D_KV_LORA = 128
D_NOPE = 64
D_ROPE = 32
D_V = 64
D_FF = 2816
BRANCH_W = 256

LANES = 128
QK_W = N_HEADS * LANES
KG_W = KV_GROUPED * LANES
ROPE_HALF = 16
LOG2E = 1.4426950408889634

A_QROWS = 4
A_KROWS = 12
A_TQ = A_QROWS * GRID_W
A_TK = A_KROWS * GRID_W
B_TQ = 256
B_TK = B_TQ + 2 * B_WINDOW
LOCAL_SUBS = 4
LOCAL_A_KEYS = A_TK + (LOCAL_SUBS - 1) * A_TQ
LOCAL_B_KEYS = B_TK + (LOCAL_SUBS - 1) * B_TQ
DENSE_TQ = 512
DENSE_SUB = 256
CHAIN_LOOKAHEAD = 2
TOK_TILE = 1024
FF_CHUNK = 256
MERGE_CHUNK = 256
PROJ_SUBS = 2
TOK_SUBS = 2
W_MIX = 9 * BRANCH_W
W_MIX_COL = 2
VMEM_LIMIT = 56 * 1024 * 1024

BF16 = jnp.bfloat16
F32 = jnp.float32


def _resident(arr, layer=None, cols=None):
    if layer is None:
        return pl.BlockSpec(arr.shape, lambda *_: (0,) * arr.ndim, pipeline_mode=pl.Buffered(1))
    width, col = cols if cols is not None else (arr.shape[-1], 0)
    return pl.BlockSpec((None,) + arr.shape[1:-1] + (width,), lambda *_: (layer,) + (0,) * (arr.ndim - 2) + (col,),
                        pipeline_mode=pl.Buffered(1))


def _params(*sem):
    return pltpu.CompilerParams(dimension_semantics=sem, vmem_limit_bytes=VMEM_LIMIT)


def _rms(x):
    return x * lax.rsqrt(jnp.mean(x * x, axis=-1, keepdims=True) + EPS)


def _rope128(x, cos, sin_signed):
    lane = lax.broadcasted_iota(jnp.int32, x.shape, 1)
    first = (lane % (2 * ROPE_HALF)) < ROPE_HALF
    rot = jnp.where(first, pltpu.roll(x, LANES - ROPE_HALF, 1), pltpu.roll(x, ROPE_HALF, 1))
    return x * cos + rot * sin_signed


def _with_ones(v):
    lane = lax.broadcasted_iota(jnp.int32, v.shape, 1)
    return jnp.where(lane % LANES == HEAD_DIM, 1.0, v)


def _spread_heads(o_ref, x, ones=False):
    lane = lax.broadcasted_iota(jnp.int32, (x.shape[0], LANES), 1)
    low = lane < HEAD_DIM
    fill = jnp.where(lane == HEAD_DIM, 1.0, 0.0) if ones else 0.0
    for c in range(x.shape[1] // LANES):
        pair = x[:, LANES * c:LANES * (c + 1)]
        o_ref[:, 2 * LANES * c:2 * LANES * c + LANES] = jnp.where(low, pair, fill).astype(BF16)
        o_ref[:, 2 * LANES * c + LANES:2 * LANES * (c + 1)] = (
            jnp.where(low, pltpu.roll(pair, HEAD_DIM, 1), fill).astype(BF16))


def _proj_kernel(x_ref, g_ref, w_ref, gmat_ref, cgain_ref, ctab_ref, dtab_ref,
                 dqn_ref, dkvn_ref, dwq_ref, dwkv_ref, *out_refs):
    sub_rows = x_ref.shape[0] // PROJ_SUBS
    for sub in range(PROJ_SUBS):
        rows = slice(sub_rows * sub, sub_rows * (sub + 1))
        _proj_rows(x_ref.at[rows], g_ref, w_ref, gmat_ref, cgain_ref, ctab_ref.at[:, rows], dtab_ref.at[:, rows],
                   dqn_ref, dkvn_ref, dwq_ref, dwkv_ref, *(o.at[rows] for o in out_refs))


def _proj_rows(x_ref, g_ref, w_ref, gmat_ref, cgain_ref, ctab_ref, dtab_ref,
               dqn_ref, dkvn_ref, dwq_ref, dwkv_ref,
               aq_ref, ak_ref, av_ref, bq_ref, bk_ref, bv_ref, cq_ref, ck_ref, cv_ref,
               dq_ref, dk_ref, dv_ref):
    h = (_rms(x_ref[...]) * g_ref[...]).astype(BF16)
    qs = HEAD_DIM ** -0.5 * LOG2E
    a0, b0, c0, d0 = 0, 3 * BRANCH_W, 5 * BRANCH_W, 7 * BRANCH_W

    pd = jnp.dot(h, w_ref[:, d0:], preferred_element_type=F32)
    dcos = dtab_ref[0]
    dsin = dtab_ref[1]
    cq = (_rms(pd[:, :D_Q_LORA]) * dqn_ref[...]).astype(BF16)
    qd = jnp.dot(cq, dwq_ref[...], preferred_element_type=F32)
    ckv = (_rms(pd[:, D_Q_LORA:D_Q_LORA + D_KV_LORA]) * dkvn_ref[...]).astype(BF16)
    kvd = jnp.dot(ckv, dwkv_ref[...], preferred_element_type=F32)
    kr = _rope128(pd[:, D_Q_LORA + D_KV_LORA:], dcos, dsin)
    scale = (D_NOPE + D_ROPE) ** -0.5 * LOG2E
    for hd in range(N_HEADS):
        sl = slice(LANES * hd, LANES * (hd + 1))
        dq_ref[:, sl] = (_rope128(qd[:, sl], dcos, dsin) * scale).astype(BF16)
        dk_ref[:, sl] = (kvd[:, sl] + kr).astype(BF16)
    dv_ref[...] = _with_ones(kvd[:, QK_W:]).astype(BF16)

    pa = jnp.dot(h, w_ref[:, a0:b0], preferred_element_type=F32)
    _spread_heads(aq_ref, pa[:, :BRANCH_W] * qs)
    _spread_heads(ak_ref, pa[:, BRANCH_W:2 * BRANCH_W])
    _spread_heads(av_ref, pa[:, 2 * BRANCH_W:], ones=True)

    pb = jnp.dot(h, w_ref[:, b0:c0], preferred_element_type=F32)
    _spread_heads(bq_ref, pb[:, :BRANCH_W] * qs)
    _spread_heads(bk_ref, pb[:, BRANCH_W:BRANCH_W + LANES])
    _spread_heads(bv_ref, pb[:, BRANCH_W + LANES:], ones=True)

    pc = jnp.dot(h, w_ref[:, c0:d0], preferred_element_type=F32)
    _spread_heads(cv_ref, pc[:, BRANCH_W + LANES:], ones=True)
    ccos = ctab_ref[0]
    csin = ctab_ref[1]

    def norm_rope(y, gmat, gain):
        sq = y * y
        hi = sq.astype(BF16)
        lo = (sq - hi.astype(F32)).astype(BF16)
        ms = jnp.dot(hi, gmat, preferred_element_type=F32) + jnp.dot(lo, gmat, preferred_element_type=F32)
        yn = y * lax.rsqrt(ms + EPS) * gain
        return jnp.concatenate([_rope128(yn[:, LANES * c:LANES * (c + 1)], ccos, csin)
                                for c in range(y.shape[1] // LANES)], axis=1)

    _spread_heads(cq_ref, norm_rope(pc[:, :BRANCH_W], gmat_ref[...], cgain_ref[:, :BRANCH_W]))
    _spread_heads(ck_ref, norm_rope(pc[:, BRANCH_W:BRANCH_W + LANES], gmat_ref[:LANES, :LANES],
                                    cgain_ref[:, BRANCH_W:]))


def _proj(layer, x2, g, w, gmat, cgain, ctab, dtab, dqn, dkvn, dwq, dwkv):
    t = x2.shape[0]
    tm = TOK_TILE
    seq_tiles = SEQ // tm

    def tok(w):
        return pl.BlockSpec((tm, w), lambda i: (i, 0))

    def tab():
        return pl.BlockSpec((2, tm, LANES), lambda i: (0, i % seq_tiles, 0))

    out_w = (QK_W, QK_W, QK_W, QK_W, KG_W, KG_W, QK_W, KG_W, KG_W, QK_W, QK_W, QK_W)
    return pl.pallas_call(
        _proj_kernel,
        grid=(t // tm,),
        in_specs=[tok(D_MODEL), _resident(g, layer), _resident(w, layer, (W_MIX, W_MIX_COL)), _resident(gmat),
                  _resident(cgain, layer), tab(), tab(), _resident(dqn, layer), _resident(dkvn, layer),
                  _resident(dwq, layer), _resident(dwkv, layer)],
        out_specs=[tok(ow) for ow in out_w],
        out_shape=[jax.ShapeDtypeStruct((t, ow), BF16) for ow in out_w],
        compiler_params=_params("parallel"),
        name="proj",
    )(x2, g, w, gmat, cgain, ctab, dtab, dqn, dkvn, dwq, dwkv)


def _scores(q, k, bias=None):
    s = lax.dot_general(q, k, (((1,), (1,)), ((), ())), preferred_element_type=F32)
    return s if bias is None else s + bias


def _softmax_pv(s, v_ref, rows, kv_head, pv_heads, sink=None):
    slab, blk = divmod(kv_head, pv_heads)
    v = v_ref[rows, pv_heads * LANES * slab:pv_heads * LANES * (slab + 1)]
    m = jnp.max(s, axis=-1, keepdims=True)
    if sink is not None:
        m = jnp.maximum(m, sink)
    pv = jnp.dot(jnp.exp2(s - m).astype(BF16), v, preferred_element_type=F32)[:, LANES * blk:LANES * (blk + 1)]
    total = pv[:, HEAD_DIM:HEAD_DIM + 1]
    if sink is not None:
        total = total + jnp.exp2(sink - m)
    return pv * (1.0 / total)


def _run_chains(chains):
    outs = []
    ready = [chains[i][0]() for i in range(min(CHAIN_LOOKAHEAD, len(chains)))]
    for i, (_, finish) in enumerate(chains):
        if i + CHAIN_LOOKAHEAD < len(chains):
            ready.append(chains[i + CHAIN_LOOKAHEAD][0]())
        outs.append(finish(ready.pop(0)))
    return outs


def _store_heads(o_ref, outs, rows=slice(None)):
    low = lax.broadcasted_iota(jnp.int32, outs[0].shape, 1) < HEAD_DIM
    for g in range(N_HEADS // 2):
        pair = jnp.where(low, outs[2 * g], pltpu.roll(outs[2 * g + 1], HEAD_DIM, 1))
        o_ref[rows, LANES * g:LANES * (g + 1)] = pair.astype(BF16)


def _a_window(blk):
    return jnp.clip(A_QROWS * blk - A_WIN_R // 2, 0, GRID_W - A_KROWS) * GRID_W


def _b_window(blk):
    return jnp.clip(B_TQ * blk - B_WINDOW, 0, SEQ - B_TK)


def _local_kernel(sink_ref, aq_ref, ak_ref, av_ref, abias_ref, bq_ref, bk_ref, bv_ref, bbias_ref, oa_ref, ob_ref):
    nblk = SEQ // A_TQ
    first = LOCAL_SUBS * pl.program_id(1)
    a_base = jnp.minimum(_a_window(first), SEQ - LOCAL_A_KEYS)
    b_base = jnp.minimum(_b_window(first), SEQ - LOCAL_B_KEYS)
    chains = []
    for sub in range(LOCAL_SUBS):
        blk = first + sub
        kind = (blk > 0).astype(jnp.int32) + (blk == nblk - 1).astype(jnp.int32)
        q_rows = slice(A_TQ * sub, A_TQ * (sub + 1))
        a_rows = pl.ds(pl.multiple_of(_a_window(blk) - a_base, A_TQ), A_TK)
        b_rows = pl.ds(pl.multiple_of(_b_window(blk) - b_base, B_WINDOW), B_TK)
        for h in range(N_HEADS):
            sl = slice(LANES * h, LANES * (h + 1))
            kv = h // (N_HEADS // KV_GROUPED)
            kvl = slice(LANES * kv, LANES * (kv + 1))
            chains.append((
                lambda h=h, sl=sl, q_rows=q_rows, a_rows=a_rows, kind=kind:
                    _scores(aq_ref[q_rows, sl], ak_ref[a_rows, sl], abias_ref[kind, h]),
                lambda s, h=h, a_rows=a_rows: _softmax_pv(s, av_ref, a_rows, h, 2)))
            chains.append((
                lambda h=h, sl=sl, kvl=kvl, q_rows=q_rows, b_rows=b_rows, kind=kind:
                    _scores(bq_ref[q_rows, sl], bk_ref[b_rows, kvl], bbias_ref[kind, h]),
                lambda s, h=h, kv=kv, b_rows=b_rows: _softmax_pv(s, bv_ref, b_rows, kv, 2, sink=sink_ref[h])))
    outs = _run_chains(chains)
    for sub in range(LOCAL_SUBS):
        q_rows = slice(A_TQ * sub, A_TQ * (sub + 1))
        mine = outs[2 * N_HEADS * sub:2 * N_HEADS * (sub + 1)]
        _store_heads(oa_ref, mine[0::2], q_rows)
        _store_heads(ob_ref, mine[1::2], q_rows)


def _local_attn(layer, sink, aq, ak, av, abias, bq, bk, bv, bbias):
    assert A_TQ == B_TQ
    t = aq.shape[0]
    nb = t // SEQ
    nj = SEQ // A_TQ // LOCAL_SUBS

    def tok(w):
        return pl.BlockSpec((LOCAL_SUBS * A_TQ, w), lambda b, j: (b * nj + j, 0))

    def keys(n, w, window):
        return pl.BlockSpec((pl.Element(n), pl.Element(w)),
                            lambda b, j: (pl.multiple_of(b * SEQ + jnp.minimum(window(LOCAL_SUBS * j), SEQ - n),
                                                         B_WINDOW), 0))

    return pl.pallas_call(
        _local_kernel,
        grid=(nb, nj),
        in_specs=[pl.BlockSpec(memory_space=pltpu.SMEM),
                  tok(QK_W), keys(LOCAL_A_KEYS, QK_W, _a_window), keys(LOCAL_A_KEYS, QK_W, _a_window),
                  _resident(abias, layer),
                  tok(QK_W), keys(LOCAL_B_KEYS, KG_W, _b_window), keys(LOCAL_B_KEYS, KG_W, _b_window),
                  _resident(bbias)],
        out_specs=[tok(BRANCH_W), tok(BRANCH_W)],
        out_shape=[jax.ShapeDtypeStruct((t, BRANCH_W), BF16)] * 2,
        compiler_params=_params("parallel", "arbitrary"),
        name="local_attn",
    )(sink, aq, ak, av, abias, bq, bk, bv, bbias)


def _dense_kernel(q_ref, k_ref, v_ref, o_ref, *, kv_heads):
    keys = slice(None)
    subs = [slice(DENSE_SUB * i, DENSE_SUB * (i + 1)) for i in range(DENSE_TQ // DENSE_SUB)]
    chains = []
    for rows in subs:
        for h in range(N_HEADS):
            kv = h // (N_HEADS // kv_heads)
            chains.append((lambda rows=rows, h=h, kv=kv: _scores(q_ref[rows, LANES * h:LANES * (h + 1)],
                                                                 k_ref[:, LANES * kv:LANES * (kv + 1)]),
                           lambda s, kv=kv: _softmax_pv(s, v_ref, keys, kv, 1)))
    outs = _run_chains(chains)
    for i, rows in enumerate(subs):
        _store_heads(o_ref, outs[N_HEADS * i:N_HEADS * (i + 1)], rows)


def _dense_attn(q, k, v, kv_heads, name):
    t = q.shape[0]
    nb = t // SEQ
    nq = SEQ // DENSE_TQ
    return pl.pallas_call(
        functools.partial(_dense_kernel, kv_heads=kv_heads),
        grid=(nb, nq),
        in_specs=[pl.BlockSpec((DENSE_TQ, QK_W), lambda b, n: (b * nq + n, 0)),
                  pl.BlockSpec((SEQ, kv_heads * LANES), lambda b, n: (b, 0)),
                  pl.BlockSpec((SEQ, kv_heads * LANES), lambda b, n: (b, 0))],
        out_specs=pl.BlockSpec((DENSE_TQ, BRANCH_W), lambda b, n: (b * nq + n, 0)),
        out_shape=jax.ShapeDtypeStruct((t, BRANCH_W), BF16),
        compiler_params=_params("parallel", "arbitrary"),
        name=name,
    )(q, k, v)


def _merge_kernel(x_ref, gpre_ref, gpost_ref, oa_ref, ob_ref, oc_ref, od_ref, wg_ref, wbr_ref, wo_ref, y_ref):
    x = x_ref[...]
    h = (_rms(x) * gpre_ref[...]).astype(BF16)
    m = None
    for c in range(D_MODEL // MERGE_CHUNK):
        cols = slice(MERGE_CHUNK * c, MERGE_CHUNK * (c + 1))
        merged = None
        for n, o_ref in enumerate((oa_ref, ob_ref, oc_ref, od_ref)):
            logits = jnp.dot(h, wg_ref[:, D_MODEL * n + MERGE_CHUNK * c:D_MODEL * n + MERGE_CHUNK * (c + 1)],
                             preferred_element_type=F32)
            y = jnp.dot(o_ref[...], wbr_ref[n, :, cols], preferred_element_type=F32)
            term = jax.nn.sigmoid(logits) * y
            merged = term if merged is None else merged + term
        part = jnp.dot(merged.astype(BF16), wo_ref[cols, :], preferred_element_type=F32)
        m = part if m is None else m + part
    y_ref[...] = x + _rms(m) * gpost_ref[...]


def _merge(layer, x2, gpre, gpost, oa, ob, oc, od, wg, wbr, wo):
    t = x2.shape[0]
    tm = TOK_TILE

    def tok(w):
        return pl.BlockSpec((tm, w), lambda i: (i, 0))

    return pl.pallas_call(
        _merge_kernel,
        grid=(t // tm,),
        in_specs=[tok(D_MODEL), _resident(gpre, layer), _resident(gpost, layer), tok(BRANCH_W), tok(BRANCH_W),
                  tok(BRANCH_W), tok(BRANCH_W), _resident(wg, layer, (N_HEADS * D_MODEL, 0)), _resident(wbr, layer),
                  _resident(wo, layer)],
        out_specs=tok(D_MODEL),
        out_shape=jax.ShapeDtypeStruct((t, D_MODEL), F32),
        compiler_params=_params("parallel"),
        name="merge",
    )(x2, gpre, gpost, oa, ob, oc, od, wg, wbr, wo)


def _ffn_kernel(x_ref, gpre_ref, gpost_ref, wgu_ref, wdn_ref, y_ref):
    sub_rows = x_ref.shape[0] // TOK_SUBS
    for sub in range(TOK_SUBS):
        rows = slice(sub_rows * sub, sub_rows * (sub + 1))
        _ffn_rows(x_ref.at[rows], gpre_ref, gpost_ref, wgu_ref, wdn_ref, y_ref.at[rows])


def _ffn_rows(x_ref, gpre_ref, gpost_ref, wgu_ref, wdn_ref, y_ref):
    x = x_ref[...]
    h = (_rms(x) * gpre_ref[...]).astype(BF16)
    acc = None
    for c in range(D_FF // FF_CHUNK):
        lo = FF_CHUNK * c
        gate = jnp.dot(h, wgu_ref[:, lo:lo + FF_CHUNK], preferred_element_type=F32)
        up = jnp.dot(h, wgu_ref[:, D_FF + lo:D_FF + lo + FF_CHUNK], preferred_element_type=F32)
        act = (jax.nn.silu(gate) * up).astype(BF16)
        part = jnp.dot(act, wdn_ref[lo:lo + FF_CHUNK, :], preferred_element_type=F32)
        acc = part if acc is None else acc + part
    y_ref[...] = x + _rms(acc) * gpost_ref[...]


def _ffn(layer, x2, gpre, gpost, wgu, wdn):
    t = x2.shape[0]
    tm = TOK_TILE
    tok = pl.BlockSpec((tm, D_MODEL), lambda i: (i, 0))
    return pl.pallas_call(
        _ffn_kernel,
        grid=(t // tm,),
        in_specs=[tok, _resident(gpre, layer), _resident(gpost, layer), _resident(wgu, layer),
                  _resident(wdn, layer)],
        out_specs=tok,
        out_shape=jax.ShapeDtypeStruct((t, D_MODEL), F32),
        compiler_params=_params("parallel"),
        name="ffn",
    )(x2, gpre, gpost, wgu, wdn)


def _rope_tables():
    inv_freq = 1.0 / (ROPE_THETA ** (jnp.arange(ROPE_HALF, dtype=F32) * (1.0 / ROPE_HALF)))

    def group(pos):
        ang = pos.astype(F32)[:, None] * inv_freq[None, :]
        c, s = jnp.cos(ang), jnp.sin(ang)
        return jnp.concatenate([c, c], axis=-1), jnp.concatenate([-s, s], axis=-1)

    gc, gs = group(jnp.arange(GRID_W))
    rc, rs = jnp.repeat(gc, GRID_W, axis=0), jnp.repeat(gs, GRID_W, axis=0)
    cc, cs = jnp.tile(gc, (SEQ // GRID_W, 1)), jnp.tile(gs, (SEQ // GRID_W, 1))
    pc, ps = group(jnp.arange(SEQ))
    ones = jnp.ones((SEQ, 2 * ROPE_HALF), F32)
    zeros = jnp.zeros((SEQ, 2 * ROPE_HALF), F32)
    ctab = jnp.stack([jnp.concatenate([rc, cc, rc, cc], axis=-1),
                      jnp.concatenate([rs, cs, rs, cs], axis=-1)])
    dtab = jnp.stack([jnp.concatenate([ones, ones, pc, ones], axis=-1),
                      jnp.concatenate([zeros, zeros, ps, zeros], axis=-1)])
    return ctab, dtab


def _group_mean_matrix():
    i = np.arange(2 * LANES)
    m = (i[:, None] // HEAD_DIM) == (i[None, :] // HEAD_DIM)
    return jnp.asarray(m.astype(np.float32) / HEAD_DIM, dtype=BF16)


def _t5_bucket_index(rel):
    nb = T5_BUCKETS // 2
    max_exact = nb // 2
    ret = (rel > 0).astype(np.int32) * nb
    n = np.abs(rel)
    large = max_exact + (np.log(np.maximum(n, 1) / max_exact) / np.log(T5_MAX_DIST / max_exact)
                         * (nb - max_exact)).astype(np.int32)
    large = np.minimum(large, nb - 1)
    return ret + np.where(n < max_exact, n, large)


def _nbr_group_kinds():
    rows = SEQ // GRID_W
    ng = rows // A_QROWS
    kinds = []
    for g in (0, 1, ng - 1):
        ws = int(np.clip(A_QROWS * g - A_WIN_R // 2, 0, rows - A_KROWS))
        qr = A_QROWS * g + np.arange(A_QROWS)
        rs = np.clip(qr - A_WIN_R // 2, 0, rows - A_WIN_R)
        kr = ws + np.arange(A_KROWS)
        ok = (kr[None, :] >= rs[:, None]) & (kr[None, :] < rs[:, None] + A_WIN_R)
        kinds.append((ws - A_QROWS * g + A_WIN_R - 1, ok))
    return kinds


def _nbr_bias_kernel(r_ref, o_ref):
    lane = lax.broadcasted_iota(jnp.int32, (GRID_W, LANES), 1)
    qc = lax.broadcasted_iota(jnp.int32, (GRID_W, LANES), 0)
    kc = lane % GRID_W
    cs = jnp.clip(qc - A_WIN_C // 2, 0, GRID_W - A_WIN_C)
    col_ok = (kc >= cs) & (kc < cs + A_WIN_C)
    left = lane < GRID_W
    pairs = [jnp.where(col_ok, pltpu.roll(jnp.broadcast_to(r_ref[j:j + 1, :], (GRID_W, LANES)),
                                          LANES - (A_WIN_C - 1), 1, stride=1, stride_axis=0), NEG_INF)
             for j in range(2 * A_WIN_R)]
    masked = jnp.full((GRID_W, LANES), NEG_INF, F32)
    for t, (s, row_ok) in enumerate(_nbr_group_kinds()):
        for qr in range(A_QROWS):
            for p in range(A_KROWS // 2):
                j = 2 * p - qr + s + 1
                ok_a, ok_b = bool(row_ok[qr][2 * p]), bool(row_ok[qr][2 * p + 1])
                if not (ok_a or ok_b):
                    blk = masked
                elif ok_a and ok_b:
                    blk = pairs[j]
                else:
                    blk = jnp.where(left if ok_a else ~left, pairs[j], NEG_INF)
                o_ref[t, GRID_W * qr:GRID_W * (qr + 1), LANES * p:LANES * (p + 1)] = blk


def _nbr_bias(rpb):
    depth = rpb.shape[0]
    r = jnp.pad(rpb.astype(F32) * LOG2E, ((0, 0), (0, 0), (1, 1), (0, GRID_W - (2 * A_WIN_C - 1))))
    r = jnp.concatenate([r[:, :, :-1], r[:, :, 1:]], axis=-1)
    return pl.pallas_call(
        _nbr_bias_kernel,
        grid=(depth, N_HEADS),
        in_specs=[pl.BlockSpec((None, None, 2 * A_WIN_R, LANES), lambda l, h: (l, h, 0, 0))],
        out_specs=pl.BlockSpec((None, 3, None, A_TQ, A_TK), lambda l, h: (l, 0, h, 0, 0)),
        out_shape=jax.ShapeDtypeStruct((depth, 3, N_HEADS, A_TQ, A_TK), F32),
        compiler_params=_params("parallel", "parallel"),
        name="nbr_bias",
    )(r)


def _win_bias(t5_table):
    rel_values = np.arange(-B_WINDOW, B_WINDOW + 1)
    by_rel = t5_table.astype(F32)[_t5_bucket_index(rel_values)].T * LOG2E
    by_rel = jnp.pad(by_rel, ((0, 0), (LANES, LANES - 1)))[:, None, :]
    return pl.pallas_call(
        _win_bias_kernel,
        grid=(N_HEADS,),
        in_specs=[pl.BlockSpec((None, 1, 4 * LANES), lambda h: (h, 0, 0))],
        out_specs=pl.BlockSpec((3, None, B_TQ, B_TK), lambda h: (0, h, 0, 0)),
        out_shape=jax.ShapeDtypeStruct((3, N_HEADS, B_TQ, B_TK), F32),
        compiler_params=_params("parallel"),
        name="win_bias",
    )(by_rel)


def _win_bias_kernel(u_ref, o_ref):
    assert B_WINDOW == LANES
    nq = SEQ // B_TQ
    qr = lax.broadcasted_iota(jnp.int32, (LANES, LANES), 0)
    lane = lax.broadcasted_iota(jnp.int32, (LANES, LANES), 1)
    tiles = {}
    for d in (-LANES, 0, LANES):
        src = jnp.broadcast_to(u_ref[:, LANES + d:3 * LANES + d], (LANES, 2 * LANES))
        skew = pltpu.roll(src, 0, 1, stride=1, stride_axis=0)[:, LANES:]
        tiles[d] = jnp.where(jnp.abs(lane - qr + d) <= B_WINDOW, skew, NEG_INF)
    masked = jnp.full((LANES, LANES), NEG_INF, F32)
    for t, n in enumerate((0, 1, nq - 1)):
        start = int(np.clip(B_TQ * n - B_WINDOW, 0, SEQ - B_TK))
        for qb in range(B_TQ // LANES):
            for kb in range(B_TK // LANES):
                d = (start + LANES * kb) - (B_TQ * n + LANES * qb)
                o_ref[t, LANES * qb:LANES * (qb + 1), LANES * kb:LANES * (kb + 1)] = tiles.get(d, masked)


def _stacked_weights(w_in, c_q_norm, c_k_norm, d_w_q_up, d_w_kv_up):
    depth = w_in.shape[0]
    n_kr = W_MIX - LANES
    n_mix = n_kr + D_ROPE
    qs = HEAD_DIM ** -0.5 * LOG2E
    gate_w = w_in.shape[2] - n_mix
    zeros = jnp.zeros((depth, D_MODEL, W_MIX_COL * W_MIX - gate_w), BF16)
    w_kr = jnp.pad(w_in[:, :, n_kr:n_mix], ((0, 0), (0, 0), (D_NOPE, LANES - D_NOPE - D_ROPE)))
    w = jnp.concatenate([w_in[:, :, n_mix:].astype(BF16), zeros, w_in[:, :, :n_kr].astype(BF16),
                         w_kr.astype(BF16)], axis=2)
    cgain = jnp.concatenate([jnp.tile(c_q_norm * qs, (1, N_HEADS)), jnp.tile(c_k_norm, (1, KV_GROUPED))],
                            axis=1).astype(F32)[:, None, :]

    def pad_heads(a, dim):
        return jnp.pad(a, ((0, 0), (0, 0), (0, 0), (0, LANES - dim))).reshape(depth, a.shape[1], QK_W)

    dwq = pad_heads(d_w_q_up.reshape(depth, D_Q_LORA, N_HEADS, D_NOPE + D_ROPE), D_NOPE + D_ROPE)
    kvu = d_w_kv_up.reshape(depth, D_KV_LORA, N_HEADS, D_NOPE + D_V)
    dwkv = jnp.concatenate([pad_heads(kvu[..., :D_NOPE], D_NOPE), pad_heads(kvu[..., D_NOPE:], D_V)], axis=2)
    return w, cgain, dwq.astype(BF16), dwkv.astype(BF16)


def kernel(x, w_in, a_rpb, b_sink, t5_bias, c_q_norm, c_k_norm, d_q_norm, d_kv_norm, d_w_q_up, d_w_kv_up,
           w_branch, w_out, ln_pre_mix, ln_post_mix, ln_pre_ffn, ln_post_ffn, ffn_w_gu, ffn_w_down):
    b, s, d = x.shape
    assert (s, d) == (SEQ, D_MODEL)
    depth = w_in.shape[0]
    x2 = x.reshape(b * s, d)
    ctab, dtab = _rope_tables()
    gmat = _group_mean_matrix()
    wbias = _win_bias(t5_bias)
    abias = _nbr_bias(a_rpb)
    w, cgain, dwq, dwkv = _stacked_weights(w_in, c_q_norm, c_k_norm, d_w_q_up, d_w_kv_up)
    wbr, wo, wgu, wdn = (a.astype(BF16) for a in (w_branch, w_out, ffn_w_gu, ffn_w_down))
    g_pre_mix, g_post_mix, g_pre_ffn, g_post_ffn, dqn, dkvn = (
        v.astype(F32)[:, None, :] for v in (ln_pre_mix, ln_post_mix, ln_pre_ffn, ln_post_ffn, d_q_norm, d_kv_norm))
    sink = b_sink.astype(F32) * LOG2E
    for l in range(depth):
        (aq, ak, av, bq, bk, bv, cq, ck, cv, dq, dk, dv) = _proj(
            l, x2, g_pre_mix, w, gmat, cgain, ctab, dtab, dqn, dkvn, dwq, dwkv)
        oa, ob = _local_attn(l, sink[l], aq, ak, av, abias, bq, bk, bv, wbias)
        oc = _dense_attn(cq, ck, cv, KV_GROUPED, "dense_c")
        od = _dense_attn(dq, dk, dv, N_HEADS, "dense_d")
        x2 = _merge(l, x2, g_pre_mix, g_post_mix, oa, ob, oc, od, w, wbr, wo)
        x2 = _ffn(l, x2, g_pre_ffn, g_post_ffn, wgu, wdn)
    return x2.reshape(b, s, d)
```

```python
import functools

import jax
import jax.numpy as jnp
import numpy as np
from jax import lax
from jax.experimental import pallas as pl
from jax.experimental.pallas import tpu as pltpu

D_MODEL = 1024
SEQ = 4096
GRID_W = 64
HEAD_DIM = 64
N_HEADS = 4
KV_GROUPED = 2
EPS = 1e-6
NEG_INF = -1e30
ROPE_THETA = 10000.0
A_WIN_R = 8
A_WIN_C = 16
B_WINDOW = 128
T5_BUCKETS = 32
T5_MAX_DIST = 128
D_Q_LORA = 256
D_KV_LORA = 128
D_NOPE = 64
D_ROPE = 32
D_V = 64
D_FF = 2816
BRANCH_W = 256

LANES = 128
QK_W = N_HEADS * LANES
KG_W = KV_GROUPED * LANES
ROPE_HALF = 16
LOG2E = 1.4426950408889634

A_QROWS = 4
A_KROWS = 12
A_TQ = A_QROWS * GRID_W
A_TK = A_KROWS * GRID_W
B_TQ = 256
B_TK = B_TQ + 2 * B_WINDOW
LOCAL_SUBS = 4
LOCAL_A_KEYS = A_TK + (LOCAL_SUBS - 1) * A_TQ
LOCAL_B_KEYS = B_TK + (LOCAL_SUBS - 1) * B_TQ
DENSE_TQ = 512
DENSE_SUB = 256
CHAIN_LOOKAHEAD = 2
TOK_TILE = 1024
FF_CHUNK = 256
MERGE_CHUNK = 256
PROJ_SUBS = 2
TOK_SUBS = 2
W_MIX = 9 * BRANCH_W
W_MIX_COL = 2
VMEM_LIMIT = 56 * 1024 * 1024

BF16 = jnp.bfloat16
F32 = jnp.float32


def _resident(arr, layer=None, cols=None):
    if layer is None:
        return pl.BlockSpec(arr.shape, lambda *_: (0,) * arr.ndim, pipeline_mode=pl.Buffered(1))
    width, col = cols if cols is not None else (arr.shape[-1], 0)
    return pl.BlockSpec((None,) + arr.shape[1:-1] + (width,), lambda *_: (layer,) + (0,) * (arr.ndim - 2) + (col,),
                        pipeline_mode=pl.Buffered(1))


def _params(*sem):
    return pltpu.CompilerParams(dimension_semantics=sem, vmem_limit_bytes=VMEM_LIMIT)


def _rms(x):
    return x * lax.rsqrt(jnp.mean(x * x, axis=-1, keepdims=True) + EPS)


def _rope128(x, cos, sin_signed):
    lane = lax.broadcasted_iota(jnp.int32, x.shape, 1)
    first = (lane % (2 * ROPE_HALF)) < ROPE_HALF
    rot = jnp.where(first, pltpu.roll(x, LANES - ROPE_HALF, 1), pltpu.roll(x, ROPE_HALF, 1))
    return x * cos + rot * sin_signed


def _with_ones(v):
    lane = lax.broadcasted_iota(jnp.int32, v.shape, 1)
    return jnp.where(lane % LANES == HEAD_DIM, 1.0, v)


def _spread_heads(o_ref, x, ones=False):
    lane = lax.broadcasted_iota(jnp.int32, (x.shape[0], LANES), 1)
    low = lane < HEAD_DIM
    fill = jnp.where(lane == HEAD_DIM, 1.0, 0.0) if ones else 0.0
    for c in range(x.shape[1] // LANES):
        pair = x[:, LANES * c:LANES * (c + 1)]
        o_ref[:, 2 * LANES * c:2 * LANES * c + LANES] = jnp.where(low, pair, fill).astype(BF16)
        o_ref[:, 2 * LANES * c + LANES:2 * LANES * (c + 1)] = (
            jnp.where(low, pltpu.roll(pair, HEAD_DIM, 1), fill).astype(BF16))


def _proj_kernel(x_ref, g_ref, w_ref, gmat_ref, cgain_ref, ctab_ref, dtab_ref,
                 dqn_ref, dkvn_ref, dwq_ref, dwkv_ref, *out_refs):
    sub_rows = x_ref.shape[0] // PROJ_SUBS
    for sub in range(PROJ_SUBS):
        rows = slice(sub_rows * sub, sub_rows * (sub + 1))
        _proj_rows(x_ref.at[rows], g_ref, w_ref, gmat_ref, cgain_ref, ctab_ref.at[:, rows], dtab_ref.at[:, rows],
                   dqn_ref, dkvn_ref, dwq_ref, dwkv_ref, *(o.at[rows] for o in out_refs))


def _proj_rows(x_ref, g_ref, w_ref, gmat_ref, cgain_ref, ctab_ref, dtab_ref,
               dqn_ref, dkvn_ref, dwq_ref, dwkv_ref,
               aq_ref, ak_ref, av_ref, bq_ref, bk_ref, bv_ref, cq_ref, ck_ref, cv_ref,
               dq_ref, dk_ref, dv_ref):
    h = (_rms(x_ref[...]) * g_ref[...]).astype(BF16)
    qs = HEAD_DIM ** -0.5 * LOG2E
    a0, b0, c0, d0 = 0, 3 * BRANCH_W, 5 * BRANCH_W, 7 * BRANCH_W

    pd = jnp.dot(h, w_ref[:, d0:], preferred_element_type=F32)
    dcos = dtab_ref[0]
    dsin = dtab_ref[1]
    cq = (_rms(pd[:, :D_Q_LORA]) * dqn_ref[...]).astype(BF16)
    qd = jnp.dot(cq, dwq_ref[...], preferred_element_type=F32)
    ckv = (_rms(pd[:, D_Q_LORA:D_Q_LORA + D_KV_LORA]) * dkvn_ref[...]).astype(BF16)
    kvd = jnp.dot(ckv, dwkv_ref[...], preferred_element_type=F32)
    kr = _rope128(pd[:, D_Q_LORA + D_KV_LORA:], dcos, dsin)
    scale = (D_NOPE + D_ROPE) ** -0.5 * LOG2E
    for hd in range(N_HEADS):
        sl = slice(LANES * hd, LANES * (hd + 1))
        dq_ref[:, sl] = (_rope128(qd[:, sl], dcos, dsin) * scale).astype(BF16)
        dk_ref[:, sl] = (kvd[:, sl] + kr).astype(BF16)
    dv_ref[...] = _with_ones(kvd[:, QK_W:]).astype(BF16)

    pa = jnp.dot(h, w_ref[:, a0:b0], preferred_element_type=F32)
    _spread_heads(aq_ref, pa[:, :BRANCH_W] * qs)
    _spread_heads(ak_ref, pa[:, BRANCH_W:2 * BRANCH_W])
    _spread_heads(av_ref, pa[:, 2 * BRANCH_W:], ones=True)

    pb = jnp.dot(h, w_ref[:, b0:c0], preferred_element_type=F32)
    _spread_heads(bq_ref, pb[:, :BRANCH_W] * qs)
    _spread_heads(bk_ref, pb[:, BRANCH_W:BRANCH_W + LANES])
    _spread_heads(bv_ref, pb[:, BRANCH_W + LANES:], ones=True)

    pc = jnp.dot(h, w_ref[:, c0:d0], preferred_element_type=F32)
    _spread_heads(cv_ref, pc[:, BRANCH_W + LANES:], ones=True)
    ccos = ctab_ref[0]
    csin = ctab_ref[1]

    def norm_rope(y, gmat, gain):
        sq = y * y
        hi = sq.astype(BF16)
        lo = (sq - hi.astype(F32)).astype(BF16)
        ms = jnp.dot(hi, gmat, preferred_element_type=F32) + jnp.dot(lo, gmat, preferred_element_type=F32)
        yn = y * lax.rsqrt(ms + EPS) * gain
        return jnp.concatenate([_rope128(yn[:, LANES * c:LANES * (c + 1)], ccos, csin)
                                for c in range(y.shape[1] // LANES)], axis=1)

    _spread_heads(cq_ref, norm_rope(pc[:, :BRANCH_W], gmat_ref[...], cgain_ref[:, :BRANCH_W]))
    _spread_heads(ck_ref, norm_rope(pc[:, BRANCH_W:BRANCH_W + LANES], gmat_ref[:LANES, :LANES],
                                    cgain_ref[:, BRANCH_W:]))


def _proj(layer, x2, g, w, gmat, cgain, ctab, dtab, dqn, dkvn, dwq, dwkv):
    t = x2.shape[0]
    tm = TOK_TILE
    seq_tiles = SEQ // tm

    def tok(w):
        return pl.BlockSpec((tm, w), lambda i: (i, 0))

    def tab():
        return pl.BlockSpec((2, tm, LANES), lambda i: (0, i % seq_tiles, 0))

    out_w = (QK_W, QK_W, QK_W, QK_W, KG_W, KG_W, QK_W, KG_W, KG_W, QK_W, QK_W, QK_W)
    return pl.pallas_call(
        _proj_kernel,
        grid=(t // tm,),
        in_specs=[tok(D_MODEL), _resident(g, layer), _resident(w, layer, (W_MIX, W_MIX_COL)), _resident(gmat),
                  _resident(cgain, layer), tab(), tab(), _resident(dqn, layer), _resident(dkvn, layer),
                  _resident(dwq, layer), _resident(dwkv, layer)],
        out_specs=[tok(ow) for ow in out_w],
        out_shape=[jax.ShapeDtypeStruct((t, ow), BF16) for ow in out_w],
        compiler_params=_params("parallel"),
        name="proj",
    )(x2, g, w, gmat, cgain, ctab, dtab, dqn, dkvn, dwq, dwkv)


def _scores(q, k, bias=None):
    s = lax.dot_general(q, k, (((1,), (1,)), ((), ())), preferred_element_type=F32)
    return s if bias is None else s + bias


def _softmax_pv(s, v_ref, rows, kv_head, pv_heads, sink=None):
    slab, blk = divmod(kv_head, pv_heads)
    v = v_ref[rows, pv_heads * LANES * slab:pv_heads * LANES * (slab + 1)]
    m = jnp.max(s, axis=-1, keepdims=True)
    if sink is not None:
        m = jnp.maximum(m, sink)
    pv = jnp.dot(jnp.exp2(s - m).astype(BF16), v, preferred_element_type=F32)[:, LANES * blk:LANES * (blk + 1)]
    total = pv[:, HEAD_DIM:HEAD_DIM + 1]
    if sink is not None:
        total = total + jnp.exp2(sink - m)
    return pv * (1.0 / total)


def _run_chains(chains):
    outs = []
    ready = [chains[i][0]() for i in range(min(CHAIN_LOOKAHEAD, len(chains)))]
    for i, (_, finish) in enumerate(chains):
        if i + CHAIN_LOOKAHEAD < len(chains):
            ready.append(chains[i + CHAIN_LOOKAHEAD][0]())
        outs.append(finish(ready.pop(0)))
    return outs


def _store_heads(o_ref, outs, rows=slice(None)):
    low = lax.broadcasted_iota(jnp.int32, outs[0].shape, 1) < HEAD_DIM
    for g in range(N_HEADS // 2):
        pair = jnp.where(low, outs[2 * g], pltpu.roll(outs[2 * g + 1], HEAD_DIM, 1))
        o_ref[rows, LANES * g:LANES * (g + 1)] = pair.astype(BF16)


def _a_window(blk):
    return jnp.clip(A_QROWS * blk - A_WIN_R // 2, 0, GRID_W - A_KROWS) * GRID_W


def _b_window(blk):
    return jnp.clip(B_TQ * blk - B_WINDOW, 0, SEQ - B_TK)


def _local_kernel(sink_ref, aq_ref, ak_ref, av_ref, abias_ref, bq_ref, bk_ref, bv_ref, bbias_ref, oa_ref, ob_ref):
    nblk = SEQ // A_TQ
    first = LOCAL_SUBS * pl.program_id(1)
    a_base = jnp.minimum(_a_window(first), SEQ - LOCAL_A_KEYS)
    b_base = jnp.minimum(_b_window(first), SEQ - LOCAL_B_KEYS)
    chains = []
    for sub in range(LOCAL_SUBS):
        blk = first + sub
        kind = (blk > 0).astype(jnp.int32) + (blk == nblk - 1).astype(jnp.int32)
        q_rows = slice(A_TQ * sub, A_TQ * (sub + 1))
        a_rows = pl.ds(pl.multiple_of(_a_window(blk) - a_base, A_TQ), A_TK)
        b_rows = pl.ds(pl.multiple_of(_b_window(blk) - b_base, B_WINDOW), B_TK)
        for h in range(N_HEADS):
            sl = slice(LANES * h, LANES * (h + 1))
            kv = h // (N_HEADS // KV_GROUPED)
            kvl = slice(LANES * kv, LANES * (kv + 1))
            chains.append((
                lambda h=h, sl=sl, q_rows=q_rows, a_rows=a_rows, kind=kind:
                    _scores(aq_ref[q_rows, sl], ak_ref[a_rows, sl], abias_ref[kind, h]),
                lambda s, h=h, a_rows=a_rows: _softmax_pv(s, av_ref, a_rows, h, 2)))
            chains.append((
                lambda h=h, sl=sl, kvl=kvl, q_rows=q_rows, b_rows=b_rows, kind=kind:
                    _scores(bq_ref[q_rows, sl], bk_ref[b_rows, kvl], bbias_ref[kind, h]),
                lambda s, h=h, kv=kv, b_rows=b_rows: _softmax_pv(s, bv_ref, b_rows, kv, 2, sink=sink_ref[h])))
    outs = _run_chains(chains)
    for sub in range(LOCAL_SUBS):
        q_rows = slice(A_TQ * sub, A_TQ * (sub + 1))
        mine = outs[2 * N_HEADS * sub:2 * N_HEADS * (sub + 1)]
        _store_heads(oa_ref, mine[0::2], q_rows)
        _store_heads(ob_ref, mine[1::2], q_rows)


def _local_attn(layer, sink, aq, ak, av, abias, bq, bk, bv, bbias):
    assert A_TQ == B_TQ
    t = aq.shape[0]
    nb = t // SEQ
    nj = SEQ // A_TQ // LOCAL_SUBS

    def tok(w):
        return pl.BlockSpec((LOCAL_SUBS * A_TQ, w), lambda b, j: (b * nj + j, 0))

    def keys(n, w, window):
        return pl.BlockSpec((pl.Element(n), pl.Element(w)),
                            lambda b, j: (pl.multiple_of(b * SEQ + jnp.minimum(window(LOCAL_SUBS * j), SEQ - n),
                                                         B_WINDOW), 0))

    return pl.pallas_call(
        _local_kernel,
        grid=(nb, nj),
        in_specs=[pl.BlockSpec(memory_space=pltpu.SMEM),
                  tok(QK_W), keys(LOCAL_A_KEYS, QK_W, _a_window), keys(LOCAL_A_KEYS, QK_W, _a_window),
                  _resident(abias, layer),
                  tok(QK_W), keys(LOCAL_B_KEYS, KG_W, _b_window), keys(LOCAL_B_KEYS, KG_W, _b_window),
                  _resident(bbias)],
        out_specs=[tok(BRANCH_W), tok(BRANCH_W)],
        out_shape=[jax.ShapeDtypeStruct((t, BRANCH_W), BF16)] * 2,
        compiler_params=_params("parallel", "arbitrary"),
        name="local_attn",
    )(sink, aq, ak, av, abias, bq, bk, bv, bbias)


def _dense_kernel(q_ref, k_ref, v_ref, o_ref, *, kv_heads):
    keys = slice(None)
    subs = [slice(DENSE_SUB * i, DENSE_SUB * (i + 1)) for i in range(DENSE_TQ // DENSE_SUB)]
    chains = []
    for rows in subs:
        for h in range(N_HEADS):
            kv = h // (N_HEADS // kv_heads)
            chains.append((lambda rows=rows, h=h, kv=kv: _scores(q_ref[rows, LANES * h:LANES * (h + 1)],
                                                                 k_ref[:, LANES * kv:LANES * (kv + 1)]),
                           lambda s, kv=kv: _softmax_pv(s, v_ref, keys, kv, 1)))
    outs = _run_chains(chains)
    for i, rows in enumerate(subs):
        _store_heads(o_ref, outs[N_HEADS * i:N_HEADS * (i + 1)], rows)


def _dense_attn(q, k, v, kv_heads, name):
    t = q.shape[0]
    nb = t // SEQ
    nq = SEQ // DENSE_TQ
    return pl.pallas_call(
        functools.partial(_dense_kernel, kv_heads=kv_heads),
        grid=(nb, nq),
        in_specs=[pl.BlockSpec((DENSE_TQ, QK_W), lambda b, n: (b * nq + n, 0)),
                  pl.BlockSpec((SEQ, kv_heads * LANES), lambda b, n: (b, 0)),
                  pl.BlockSpec((SEQ, kv_heads * LANES), lambda b, n: (b, 0))],
        out_specs=pl.BlockSpec((DENSE_TQ, BRANCH_W), lambda b, n: (b * nq + n, 0)),
        out_shape=jax.ShapeDtypeStruct((t, BRANCH_W), BF16),
        compiler_params=_params("parallel", "arbitrary"),
        name=name,
    )(q, k, v)


def _merge_kernel(x_ref, gpre_ref, gpost_ref, oa_ref, ob_ref, oc_ref, od_ref, wg_ref, wbr_ref, wo_ref, y_ref):
    x = x_ref[...]
    h = (_rms(x) * gpre_ref[...]).astype(BF16)
    m = None
    for c in range(D_MODEL // MERGE_CHUNK):
        cols = slice(MERGE_CHUNK * c, MERGE_CHUNK * (c + 1))
        merged = None
        for n, o_ref in enumerate((oa_ref, ob_ref, oc_ref, od_ref)):
            logits = jnp.dot(h, wg_ref[:, D_MODEL * n + MERGE_CHUNK * c:D_MODEL * n + MERGE_CHUNK * (c + 1)],
                             preferred_element_type=F32)
            y = jnp.dot(o_ref[...], wbr_ref[n, :, cols], preferred_element_type=F32)
            term = jax.nn.sigmoid(logits) * y
            merged = term if merged is None else merged + term
        part = jnp.dot(merged.astype(BF16), wo_ref[cols, :], preferred_element_type=F32)
        m = part if m is None else m + part
    y_ref[...] = x + _rms(m) * gpost_ref[...]


def _merge(layer, x2, gpre, gpost, oa, ob, oc, od, wg, wbr, wo):
    t = x2.shape[0]
    tm = TOK_TILE

    def tok(w):
        return pl.BlockSpec((tm, w), lambda i: (i, 0))

    return pl.pallas_call(
        _merge_kernel,
        grid=(t // tm,),
        in_specs=[tok(D_MODEL), _resident(gpre, layer), _resident(gpost, layer), tok(BRANCH_W), tok(BRANCH_W),
                  tok(BRANCH_W), tok(BRANCH_W), _resident(wg, layer, (N_HEADS * D_MODEL, 0)), _resident(wbr, layer),
                  _resident(wo, layer)],
        out_specs=tok(D_MODEL),
        out_shape=jax.ShapeDtypeStruct((t, D_MODEL), F32),
        compiler_params=_params("parallel"),
        name="merge",
    )(x2, gpre, gpost, oa, ob, oc, od, wg, wbr, wo)


def _ffn_kernel(x_ref, gpre_ref, gpost_ref, wgu_ref, wdn_ref, y_ref):
    sub_rows = x_ref.shape[0] // TOK_SUBS
    for sub in range(TOK_SUBS):
        rows = slice(sub_rows * sub, sub_rows * (sub + 1))
        _ffn_rows(x_ref.at[rows], gpre_ref, gpost_ref, wgu_ref, wdn_ref, y_ref.at[rows])


def _ffn_rows(x_ref, gpre_ref, gpost_ref, wgu_ref, wdn_ref, y_ref):
    x = x_ref[...]
    h = (_rms(x) * gpre_ref[...]).astype(BF16)
    acc = None
    for c in range(D_FF // FF_CHUNK):
        lo = FF_CHUNK * c
        gate = jnp.dot(h, wgu_ref[:, lo:lo + FF_CHUNK], preferred_element_type=F32)
        up = jnp.dot(h, wgu_ref[:, D_FF + lo:D_FF + lo + FF_CHUNK], preferred_element_type=F32)
        act = (jax.nn.silu(gate) * up).astype(BF16)
        part = jnp.dot(act, wdn_ref[lo:lo + FF_CHUNK, :], preferred_element_type=F32)
        acc = part if acc is None else acc + part
    y_ref[...] = x + _rms(acc) * gpost_ref[...]


def _ffn(layer, x2, gpre, gpost, wgu, wdn):
    t = x2.shape[0]
    tm = TOK_TILE
    tok = pl.BlockSpec((tm, D_MODEL), lambda i: (i, 0))
    return pl.pallas_call(
        _ffn_kernel,
        grid=(t // tm,),
        in_specs=[tok, _resident(gpre, layer), _resident(gpost, layer), _resident(wgu, layer),
                  _resident(wdn, layer)],
        out_specs=tok,
        out_shape=jax.ShapeDtypeStruct((t, D_MODEL), F32),
        compiler_params=_params("parallel"),
        name="ffn",
    )(x2, gpre, gpost, wgu, wdn)


def _rope_tables():
    f32 = np.float32
    inv_freq = (f32(1.0) / (f32(ROPE_THETA) ** (np.arange(ROPE_HALF, dtype=f32) * f32(1.0 / ROPE_HALF)))).astype(f32)

    def group(pos):
        ang = pos.astype(f32)[:, None] * inv_freq[None, :]
        c, s = np.cos(ang), np.sin(ang)
        return np.concatenate([c, c], axis=-1), np.concatenate([-s, s], axis=-1)

    gc, gs = group(np.arange(GRID_W))
    rc, rs = np.repeat(gc, GRID_W, axis=0), np.repeat(gs, GRID_W, axis=0)
    cc, cs = np.tile(gc, (SEQ // GRID_W, 1)), np.tile(gs, (SEQ // GRID_W, 1))
    pc, ps = group(np.arange(SEQ))
    ones = np.ones((SEQ, 2 * ROPE_HALF), f32)
    zeros = np.zeros((SEQ, 2 * ROPE_HALF), f32)
    ctab = np.stack([np.concatenate([rc, cc, rc, cc], axis=-1),
                     np.concatenate([rs, cs, rs, cs], axis=-1)])
    dtab = np.stack([np.concatenate([ones, ones, pc, ones], axis=-1),
                     np.concatenate([zeros, zeros, ps, zeros], axis=-1)])
    return jnp.asarray(ctab, F32), jnp.asarray(dtab, F32)


def _group_mean_matrix():
    i = np.arange(2 * LANES)
    m = (i[:, None] // HEAD_DIM) == (i[None, :] // HEAD_DIM)
    return jnp.asarray(m.astype(np.float32) / HEAD_DIM, dtype=BF16)


def _t5_bucket_index(rel):
    nb = T5_BUCKETS // 2
    max_exact = nb // 2
    ret = (rel > 0).astype(np.int32) * nb
    n = np.abs(rel)
    large = max_exact + (np.log(np.maximum(n, 1) / max_exact) / np.log(T5_MAX_DIST / max_exact)
                         * (nb - max_exact)).astype(np.int32)
    large = np.minimum(large, nb - 1)
    return ret + np.where(n < max_exact, n, large)


def _nbr_group_kinds():
    rows = SEQ // GRID_W
    ng = rows // A_QROWS
    kinds = []
    for g in (0, 1, ng - 1):
        ws = int(np.clip(A_QROWS * g - A_WIN_R // 2, 0, rows - A_KROWS))
        qr = A_QROWS * g + np.arange(A_QROWS)
        rs = np.clip(qr - A_WIN_R // 2, 0, rows - A_WIN_R)
        kr = ws + np.arange(A_KROWS)
        ok = (kr[None, :] >= rs[:, None]) & (kr[None, :] < rs[:, None] + A_WIN_R)
        kinds.append((ws - A_QROWS * g + A_WIN_R - 1, ok))
    return kinds


def _nbr_bias_kernel(r_ref, o_ref):
    lane = lax.broadcasted_iota(jnp.int32, (GRID_W, LANES), 1)
    qc = lax.broadcasted_iota(jnp.int32, (GRID_W, LANES), 0)
    kc = lane % GRID_W
    cs = jnp.clip(qc - A_WIN_C // 2, 0, GRID_W - A_WIN_C)
    col_ok = (kc >= cs) & (kc < cs + A_WIN_C)
    left = lane < GRID_W
    pairs = [jnp.where(col_ok, pltpu.roll(jnp.broadcast_to(r_ref[j:j + 1, :], (GRID_W, LANES)),
                                          LANES - (A_WIN_C - 1), 1, stride=1, stride_axis=0), NEG_INF)
             for j in range(2 * A_WIN_R)]
    masked = jnp.full((GRID_W, LANES), NEG_INF, F32)
    for t, (s, row_ok) in enumerate(_nbr_group_kinds()):
        for qr in range(A_QROWS):
            for p in range(A_KROWS // 2):
                j = 2 * p - qr + s + 1
                ok_a, ok_b = bool(row_ok[qr][2 * p]), bool(row_ok[qr][2 * p + 1])
                if not (ok_a or ok_b):
                    blk = masked
                elif ok_a and ok_b:
                    blk = pairs[j]
                else:
                    blk = jnp.where(left if ok_a else ~left, pairs[j], NEG_INF)
                o_ref[t, GRID_W * qr:GRID_W * (qr + 1), LANES * p:LANES * (p + 1)] = blk


def _nbr_bias(rpb):
    depth = rpb.shape[0]
    r = jnp.pad(rpb.astype(F32) * LOG2E, ((0, 0), (0, 0), (1, 1), (0, GRID_W - (2 * A_WIN_C - 1))))
    r = jnp.concatenate([r[:, :, :-1], r[:, :, 1:]], axis=-1)
    return pl.pallas_call(
        _nbr_bias_kernel,
        grid=(depth, N_HEADS),
        in_specs=[pl.BlockSpec((None, None, 2 * A_WIN_R, LANES), lambda l, h: (l, h, 0, 0))],
        out_specs=pl.BlockSpec((None, 3, None, A_TQ, A_TK), lambda l, h: (l, 0, h, 0, 0)),
        out_shape=jax.ShapeDtypeStruct((depth, 3, N_HEADS, A_TQ, A_TK), F32),
        compiler_params=_params("parallel", "parallel"),
        name="nbr_bias",
    )(r)


def _win_bias(t5_table):
    rel_values = np.arange(-B_WINDOW, B_WINDOW + 1)
    by_rel = t5_table.astype(F32)[_t5_bucket_index(rel_values)].T * LOG2E
    by_rel = jnp.pad(by_rel, ((0, 0), (LANES, LANES - 1)))[:, None, :]
    return pl.pallas_call(
        _win_bias_kernel,
        grid=(N_HEADS,),
        in_specs=[pl.BlockSpec((None, 1, 4 * LANES), lambda h: (h, 0, 0))],
        out_specs=pl.BlockSpec((3, None, B_TQ, B_TK), lambda h: (0, h, 0, 0)),
        out_shape=jax.ShapeDtypeStruct((3, N_HEADS, B_TQ, B_TK), F32),
        compiler_params=_params("parallel"),
        name="win_bias",
    )(by_rel)


def _win_bias_kernel(u_ref, o_ref):
    assert B_WINDOW == LANES
    nq = SEQ // B_TQ
    qr = lax.broadcasted_iota(jnp.int32, (LANES, LANES), 0)
    lane = lax.broadcasted_iota(jnp.int32, (LANES, LANES), 1)
    tiles = {}
    for d in (-LANES, 0, LANES):
        src = jnp.broadcast_to(u_ref[:, LANES + d:3 * LANES + d], (LANES, 2 * LANES))
        skew = pltpu.roll(src, 0, 1, stride=1, stride_axis=0)[:, LANES:]
        tiles[d] = jnp.where(jnp.abs(lane - qr + d) <= B_WINDOW, skew, NEG_INF)
    masked = jnp.full((LANES, LANES), NEG_INF, F32)
    for t, n in enumerate((0, 1, nq - 1)):
        start = int(np.clip(B_TQ * n - B_WINDOW, 0, SEQ - B_TK))
        for qb in range(B_TQ // LANES):
            for kb in range(B_TK // LANES):
                d = (start + LANES * kb) - (B_TQ * n + LANES * qb)
                o_ref[t, LANES * qb:LANES * (qb + 1), LANES * kb:LANES * (kb + 1)] = tiles.get(d, masked)


def _stacked_weights(w_in, c_q_norm, c_k_norm, d_w_q_up, d_w_kv_up):
    depth = w_in.shape[0]
    n_kr = W_MIX - LANES
    n_mix = n_kr + D_ROPE
    qs = HEAD_DIM ** -0.5 * LOG2E
    gate_w = w_in.shape[2] - n_mix
    zeros = jnp.zeros((depth, D_MODEL, W_MIX_COL * W_MIX - gate_w), BF16)
    w_kr = jnp.pad(w_in[:, :, n_kr:n_mix], ((0, 0), (0, 0), (D_NOPE, LANES - D_NOPE - D_ROPE)))
    w = jnp.concatenate([w_in[:, :, n_mix:].astype(BF16), zeros, w_in[:, :, :n_kr].astype(BF16),
                         w_kr.astype(BF16)], axis=2)
    cgain = jnp.concatenate([jnp.tile(c_q_norm * qs, (1, N_HEADS)), jnp.tile(c_k_norm, (1, KV_GROUPED))],
                            axis=1).astype(F32)[:, None, :]

    def pad_heads(a, dim):
        return jnp.pad(a, ((0, 0), (0, 0), (0, 0), (0, LANES - dim))).reshape(depth, a.shape[1], QK_W)

    dwq = pad_heads(d_w_q_up.reshape(depth, D_Q_LORA, N_HEADS, D_NOPE + D_ROPE), D_NOPE + D_ROPE)
    kvu = d_w_kv_up.reshape(depth, D_KV_LORA, N_HEADS, D_NOPE + D_V)
    dwkv = jnp.concatenate([pad_heads(kvu[..., :D_NOPE], D_NOPE), pad_heads(kvu[..., D_NOPE:], D_V)], axis=2)
    return w, cgain, dwq.astype(BF16), dwkv.astype(BF16)


def kernel(x, w_in, a_rpb, b_sink, t5_bias, c_q_norm, c_k_norm, d_q_norm, d_kv_norm, d_w_q_up, d_w_kv_up,
           w_branch, w_out, ln_pre_mix, ln_post_mix, ln_pre_ffn, ln_post_ffn, ffn_w_gu, ffn_w_down):
    b, s, d = x.shape
    assert (s, d) == (SEQ, D_MODEL)
    depth = w_in.shape[0]
    x2 = x.reshape(b * s, d)
    ctab, dtab = _rope_tables()
    gmat = _group_mean_matrix()
    wbias = _win_bias(t5_bias)
    abias = _nbr_bias(a_rpb)
    w, cgain, dwq, dwkv = _stacked_weights(w_in, c_q_norm, c_k_norm, d_w_q_up, d_w_kv_up)
    wbr, wo, wgu, wdn = (a.astype(BF16) for a in (w_branch, w_out, ffn_w_gu, ffn_w_down))
    g_pre_mix, g_post_mix, g_pre_ffn, g_post_ffn, dqn, dkvn = (
        v.astype(F32)[:, None, :] for v in (ln_pre_mix, ln_post_mix, ln_pre_ffn, ln_post_ffn, d_q_norm, d_kv_norm))
    sink = b_sink.astype(F32) * LOG2E
    for l in range(depth):
        (aq, ak, av, bq, bk, bv, cq, ck, cv, dq, dk, dv) = _proj(
            l, x2, g_pre_mix, w, gmat, cgain, ctab, dtab, dqn, dkvn, dwq, dwkv)
        oa, ob = _local_attn(l, sink[l], aq, ak, av, abias, bq, bk, bv, wbias)
        oc = _dense_attn(cq, ck, cv, KV_GROUPED, "dense_c")
        od = _dense_attn(dq, dk, dv, N_HEADS, "dense_d")
        x2 = _merge(l, x2, g_pre_mix, g_post_mix, oa, ob, oc, od, w, wbr, wo)
        x2 = _ffn(l, x2, g_pre_ffn, g_post_ffn, wgu, wdn)
    return x2.reshape(b, s, d)
```

```python
import functools

import jax
import jax.numpy as jnp
import numpy as np
from jax import lax
from jax.experimental import pallas as pl
from jax.experimental.pallas import tpu as pltpu

D_MODEL = 1024
SEQ = 4096
GRID_W = 64
HEAD_DIM = 64
N_HEADS = 4
KV_GROUPED = 2
EPS = 1e-6
NEG_INF = -1e30
ROPE_THETA = 10000.0
A_WIN_R = 8
A_WIN_C = 16
B_WINDOW = 128
T5_BUCKETS = 32
T5_MAX_DIST = 128
D_Q_LORA = 256
D_KV_LORA = 128
D_NOPE = 64
D_ROPE = 32
D_V = 64
D_FF = 2816
BRANCH_W = 256

LANES = 128
QK_W = N_HEADS * LANES
KG_W = KV_GROUPED * LANES
ROPE_HALF = 16
LOG2E = 1.4426950408889634

A_QROWS = 4
A_KROWS = 12
A_TQ = A_QROWS * GRID_W
A_TK = A_KROWS * GRID_W
B_TQ = 256
B_TK = B_TQ + 2 * B_WINDOW
LOCAL_SUBS = 4
LOCAL_A_KEYS = A_TK + (LOCAL_SUBS - 1) * A_TQ
LOCAL_B_KEYS = B_TK + (LOCAL_SUBS - 1) * B_TQ
DENSE_TQ = 512
DENSE_SUB = 256
DENSE_LOOKAHEAD = 3
LOCAL_LOOKAHEAD = 2
TOK_TILE = 1024
FF_CHUNK = 256
MERGE_CHUNK = 256
PROJ_SUBS = 2
TOK_SUBS = 2
W_MIX = 9 * BRANCH_W
W_MIX_COL = 2
VMEM_LIMIT = 56 * 1024 * 1024

BF16 = jnp.bfloat16
F32 = jnp.float32


def _resident(arr, layer=None, cols=None):
    if layer is None:
        return pl.BlockSpec(arr.shape, lambda *_: (0,) * arr.ndim, pipeline_mode=pl.Buffered(1))
    width, col = cols if cols is not None else (arr.shape[-1], 0)
    return pl.BlockSpec((None,) + arr.shape[1:-1] + (width,), lambda *_: (layer,) + (0,) * (arr.ndim - 2) + (col,),
                        pipeline_mode=pl.Buffered(1))


def _params(*sem):
    return pltpu.CompilerParams(dimension_semantics=sem, vmem_limit_bytes=VMEM_LIMIT)


def _rms(x):
    return x * lax.rsqrt(jnp.mean(x * x, axis=-1, keepdims=True) + EPS)


def _rope128(x, cos, sin_signed):
    lane = lax.broadcasted_iota(jnp.int32, x.shape, 1)
    first = (lane % (2 * ROPE_HALF)) < ROPE_HALF
    rot = jnp.where(first, pltpu.roll(x, LANES - ROPE_HALF, 1), pltpu.roll(x, ROPE_HALF, 1))
    return x * cos + rot * sin_signed


def _with_ones(v):
    lane = lax.broadcasted_iota(jnp.int32, v.shape, 1)
    return jnp.where(lane % LANES == HEAD_DIM, 1.0, v)


def _spread_heads(o_ref, x, ones=False):
    lane = lax.broadcasted_iota(jnp.int32, (x.shape[0], LANES), 1)
    low = lane < HEAD_DIM
    fill = jnp.where(lane == HEAD_DIM, 1.0, 0.0) if ones else 0.0
    for c in range(x.shape[1] // LANES):
        pair = x[:, LANES * c:LANES * (c + 1)]
        o_ref[:, 2 * LANES * c:2 * LANES * c + LANES] = jnp.where(low, pair, fill).astype(BF16)
        o_ref[:, 2 * LANES * c + LANES:2 * LANES * (c + 1)] = (
            jnp.where(low, pltpu.roll(pair, HEAD_DIM, 1), fill).astype(BF16))


def _proj_kernel(x_ref, g_ref, w_ref, gmat_ref, cgain_ref, ctab_ref, dtab_ref,
                 dqn_ref, dkvn_ref, dwq_ref, dwkv_ref, *out_refs):
    sub_rows = x_ref.shape[0] // PROJ_SUBS
    for sub in range(PROJ_SUBS):
        rows = slice(sub_rows * sub, sub_rows * (sub + 1))
        _proj_rows(x_ref.at[rows], g_ref, w_ref, gmat_ref, cgain_ref, ctab_ref.at[:, rows], dtab_ref.at[:, rows],
                   dqn_ref, dkvn_ref, dwq_ref, dwkv_ref, *(o.at[rows] for o in out_refs))


def _proj_rows(x_ref, g_ref, w_ref, gmat_ref, cgain_ref, ctab_ref, dtab_ref,
               dqn_ref, dkvn_ref, dwq_ref, dwkv_ref,
               aq_ref, ak_ref, av_ref, bq_ref, bk_ref, bv_ref, cq_ref, ck_ref, cv_ref,
               dq_ref, dk_ref, dv_ref):
    h = (_rms(x_ref[...]) * g_ref[...]).astype(BF16)
    qs = HEAD_DIM ** -0.5 * LOG2E
    a0, b0, c0, d0 = 0, 3 * BRANCH_W, 5 * BRANCH_W, 7 * BRANCH_W

    pd = jnp.dot(h, w_ref[:, d0:], preferred_element_type=F32)
    dcos = dtab_ref[0]
    dsin = dtab_ref[1]
    cq = (_rms(pd[:, :D_Q_LORA]) * dqn_ref[...]).astype(BF16)
    qd = jnp.dot(cq, dwq_ref[...], preferred_element_type=F32)
    ckv = (_rms(pd[:, D_Q_LORA:D_Q_LORA + D_KV_LORA]) * dkvn_ref[...]).astype(BF16)
    kvd = jnp.dot(ckv, dwkv_ref[...], preferred_element_type=F32)
    kr = _rope128(pd[:, D_Q_LORA + D_KV_LORA:], dcos, dsin)
    scale = (D_NOPE + D_ROPE) ** -0.5 * LOG2E
    for hd in range(N_HEADS):
        sl = slice(LANES * hd, LANES * (hd + 1))
        dq_ref[:, sl] = (_rope128(qd[:, sl], dcos, dsin) * scale).astype(BF16)
        dk_ref[:, sl] = (kvd[:, sl] + kr).astype(BF16)
    dv_ref[...] = _with_ones(kvd[:, QK_W:]).astype(BF16)

    pa = jnp.dot(h, w_ref[:, a0:b0], preferred_element_type=F32)
    _spread_heads(aq_ref, pa[:, :BRANCH_W] * qs)
    _spread_heads(ak_ref, pa[:, BRANCH_W:2 * BRANCH_W])
    _spread_heads(av_ref, pa[:, 2 * BRANCH_W:], ones=True)

    pb = jnp.dot(h, w_ref[:, b0:c0], preferred_element_type=F32)
    _spread_heads(bq_ref, pb[:, :BRANCH_W] * qs)
    _spread_heads(bk_ref, pb[:, BRANCH_W:BRANCH_W + LANES])
    _spread_heads(bv_ref, pb[:, BRANCH_W + LANES:], ones=True)

    pc = jnp.dot(h, w_ref[:, c0:d0], preferred_element_type=F32)
    _spread_heads(cv_ref, pc[:, BRANCH_W + LANES:], ones=True)
    ccos = ctab_ref[0]
    csin = ctab_ref[1]

    def norm_rope(y, gmat, gain):
        sq = y * y
        hi = sq.astype(BF16)
        lo = (sq - hi.astype(F32)).astype(BF16)
        ms = jnp.dot(hi, gmat, preferred_element_type=F32) + jnp.dot(lo, gmat, preferred_element_type=F32)
        yn = y * lax.rsqrt(ms + EPS) * gain
        return jnp.concatenate([_rope128(yn[:, LANES * c:LANES * (c + 1)], ccos, csin)
                                for c in range(y.shape[1] // LANES)], axis=1)

    _spread_heads(cq_ref, norm_rope(pc[:, :BRANCH_W], gmat_ref[...], cgain_ref[:, :BRANCH_W]))
    _spread_heads(ck_ref, norm_rope(pc[:, BRANCH_W:BRANCH_W + LANES], gmat_ref[:LANES, :LANES],
                                    cgain_ref[:, BRANCH_W:]))


def _proj(layer, x2, g, w, gmat, cgain, ctab, dtab, dqn, dkvn, dwq, dwkv):
    t = x2.shape[0]
    tm = TOK_TILE
    seq_tiles = SEQ // tm

    def tok(w):
        return pl.BlockSpec((tm, w), lambda i: (i, 0))

    def tab():
        return pl.BlockSpec((2, tm, LANES), lambda i: (0, i % seq_tiles, 0))

    out_w = (QK_W, QK_W, QK_W, QK_W, KG_W, KG_W, QK_W, KG_W, KG_W, QK_W, QK_W, QK_W)
    return pl.pallas_call(
        _proj_kernel,
        grid=(t // tm,),
        in_specs=[tok(D_MODEL), _resident(g, layer), _resident(w, layer, (W_MIX, W_MIX_COL)), _resident(gmat),
                  _resident(cgain, layer), tab(), tab(), _resident(dqn, layer), _resident(dkvn, layer),
                  _resident(dwq, layer), _resident(dwkv, layer)],
        out_specs=[tok(ow) for ow in out_w],
        out_shape=[jax.ShapeDtypeStruct((t, ow), BF16) for ow in out_w],
        compiler_params=_params("parallel"),
        name="proj",
    )(x2, g, w, gmat, cgain, ctab, dtab, dqn, dkvn, dwq, dwkv)


def _scores(q, k, bias=None):
    s = lax.dot_general(q, k, (((1,), (1,)), ((), ())), preferred_element_type=F32)
    return s if bias is None else s + bias


def _softmax_pv(s, v_ref, rows, kv_head, pv_heads, sink=None):
    slab, blk = divmod(kv_head, pv_heads)
    v = v_ref[rows, pv_heads * LANES * slab:pv_heads * LANES * (slab + 1)]
    m = jnp.max(s, axis=-1, keepdims=True)
    if sink is not None:
        m = jnp.maximum(m, sink)
    pv = jnp.dot(jnp.exp2(s - m).astype(BF16), v, preferred_element_type=F32)[:, LANES * blk:LANES * (blk + 1)]
    total = pv[:, HEAD_DIM:HEAD_DIM + 1]
    if sink is not None:
        total = total + jnp.exp2(sink - m)
    return pv * (1.0 / total)


def _run_chains(chains, lookahead):
    outs = []
    ready = [chains[i][0]() for i in range(min(lookahead, len(chains)))]
    for i, (_, finish) in enumerate(chains):
        if i + lookahead < len(chains):
            ready.append(chains[i + lookahead][0]())
        outs.append(finish(ready.pop(0)))
    return outs


def _store_heads(o_ref, outs, rows=slice(None)):
    low = lax.broadcasted_iota(jnp.int32, outs[0].shape, 1) < HEAD_DIM
    for g in range(N_HEADS // 2):
        pair = jnp.where(low, outs[2 * g], pltpu.roll(outs[2 * g + 1], HEAD_DIM, 1))
        o_ref[rows, LANES * g:LANES * (g + 1)] = pair.astype(BF16)


def _a_window(blk):
    return jnp.clip(A_QROWS * blk - A_WIN_R // 2, 0, GRID_W - A_KROWS) * GRID_W


def _b_window(blk):
    return jnp.clip(B_TQ * blk - B_WINDOW, 0, SEQ - B_TK)


def _local_kernel(sink_ref, aq_ref, ak_ref, av_ref, abias_ref, bq_ref, bk_ref, bv_ref, bbias_ref, oa_ref, ob_ref):
    nblk = SEQ // A_TQ
    first = LOCAL_SUBS * pl.program_id(1)
    a_base = jnp.minimum(_a_window(first), SEQ - LOCAL_A_KEYS)
    b_base = jnp.minimum(_b_window(first), SEQ - LOCAL_B_KEYS)
    chains = []
    for sub in range(LOCAL_SUBS):
        blk = first + sub
        kind = (blk > 0).astype(jnp.int32) + (blk == nblk - 1).astype(jnp.int32)
        q_rows = slice(A_TQ * sub, A_TQ * (sub + 1))
        a_rows = pl.ds(pl.multiple_of(_a_window(blk) - a_base, A_TQ), A_TK)
        b_rows = pl.ds(pl.multiple_of(_b_window(blk) - b_base, B_WINDOW), B_TK)
        for h in range(N_HEADS):
            sl = slice(LANES * h, LANES * (h + 1))
            kv = h // (N_HEADS // KV_GROUPED)
            kvl = slice(LANES * kv, LANES * (kv + 1))
            chains.append((
                lambda h=h, sl=sl, q_rows=q_rows, a_rows=a_rows, kind=kind:
                    _scores(aq_ref[q_rows, sl], ak_ref[a_rows, sl], abias_ref[kind, h]),
                lambda s, h=h, a_rows=a_rows: _softmax_pv(s, av_ref, a_rows, h, 2)))
            chains.append((
                lambda h=h, sl=sl, kvl=kvl, q_rows=q_rows, b_rows=b_rows, kind=kind:
                    _scores(bq_ref[q_rows, sl], bk_ref[b_rows, kvl], bbias_ref[kind, h]),
                lambda s, h=h, kv=kv, b_rows=b_rows: _softmax_pv(s, bv_ref, b_rows, kv, 2, sink=sink_ref[h])))
    outs = _run_chains(chains, LOCAL_LOOKAHEAD)
    for sub in range(LOCAL_SUBS):
        q_rows = slice(A_TQ * sub, A_TQ * (sub + 1))
        mine = outs[2 * N_HEADS * sub:2 * N_HEADS * (sub + 1)]
        _store_heads(oa_ref, mine[0::2], q_rows)
        _store_heads(ob_ref, mine[1::2], q_rows)


def _local_attn(layer, sink, aq, ak, av, abias, bq, bk, bv, bbias):
    assert A_TQ == B_TQ
    t = aq.shape[0]
    nb = t // SEQ
    nj = SEQ // A_TQ // LOCAL_SUBS

    def tok(w):
        return pl.BlockSpec((LOCAL_SUBS * A_TQ, w), lambda b, j: (b * nj + j, 0))

    def keys(n, w, window):
        return pl.BlockSpec((pl.Element(n), pl.Element(w)),
                            lambda b, j: (pl.multiple_of(b * SEQ + jnp.minimum(window(LOCAL_SUBS * j), SEQ - n),
                                                         B_WINDOW), 0))

    return pl.pallas_call(
        _local_kernel,
        grid=(nb, nj),
        in_specs=[pl.BlockSpec(memory_space=pltpu.SMEM),
                  tok(QK_W), keys(LOCAL_A_KEYS, QK_W, _a_window), keys(LOCAL_A_KEYS, QK_W, _a_window),
                  _resident(abias, layer),
                  tok(QK_W), keys(LOCAL_B_KEYS, KG_W, _b_window), keys(LOCAL_B_KEYS, KG_W, _b_window),
                  _resident(bbias)],
        out_specs=[tok(BRANCH_W), tok(BRANCH_W)],
        out_shape=[jax.ShapeDtypeStruct((t, BRANCH_W), BF16)] * 2,
        compiler_params=_params("parallel", "arbitrary"),
        name="local_attn",
    )(sink, aq, ak, av, abias, bq, bk, bv, bbias)


def _dense_kernel(q_ref, k_ref, v_ref, o_ref, *, kv_heads):
    keys = slice(None)
    subs = [slice(DENSE_SUB * i, DENSE_SUB * (i + 1)) for i in range(DENSE_TQ // DENSE_SUB)]
    chains = []
    for rows in subs:
        for h in range(N_HEADS):
            kv = h // (N_HEADS // kv_heads)
            chains.append((lambda rows=rows, h=h, kv=kv: _scores(q_ref[rows, LANES * h:LANES * (h + 1)],
                                                                 k_ref[:, LANES * kv:LANES * (kv + 1)]),
                           lambda s, kv=kv: _softmax_pv(s, v_ref, keys, kv, 1)))
    outs = _run_chains(chains, DENSE_LOOKAHEAD)
    for i, rows in enumerate(subs):
        _store_heads(o_ref, outs[N_HEADS * i:N_HEADS * (i + 1)], rows)


def _dense_attn(q, k, v, kv_heads, name):
    t = q.shape[0]
    nb = t // SEQ
    nq = SEQ // DENSE_TQ
    return pl.pallas_call(
        functools.partial(_dense_kernel, kv_heads=kv_heads),
        grid=(nb, nq),
        in_specs=[pl.BlockSpec((DENSE_TQ, QK_W), lambda b, n: (b * nq + n, 0)),
                  pl.BlockSpec((SEQ, kv_heads * LANES), lambda b, n: (b, 0)),
                  pl.BlockSpec((SEQ, kv_heads * LANES), lambda b, n: (b, 0))],
        out_specs=pl.BlockSpec((DENSE_TQ, BRANCH_W), lambda b, n: (b * nq + n, 0)),
        out_shape=jax.ShapeDtypeStruct((t, BRANCH_W), BF16),
        compiler_params=_params("parallel", "arbitrary"),
        name=name,
    )(q, k, v)


def _merge_kernel(x_ref, gpre_ref, gpost_ref, oa_ref, ob_ref, oc_ref, od_ref, wg_ref, wbr_ref, wo_ref, y_ref):
    x = x_ref[...]
    h = (_rms(x) * gpre_ref[...]).astype(BF16)
    m = None
    for c in range(D_MODEL // MERGE_CHUNK):
        cols = slice(MERGE_CHUNK * c, MERGE_CHUNK * (c + 1))
        merged = None
        for n, o_ref in enumerate((oa_ref, ob_ref, oc_ref, od_ref)):
            logits = jnp.dot(h, wg_ref[:, D_MODEL * n + MERGE_CHUNK * c:D_MODEL * n + MERGE_CHUNK * (c + 1)],
                             preferred_element_type=F32)
            y = jnp.dot(o_ref[...], wbr_ref[n, :, cols], preferred_element_type=F32)
            term = jax.nn.sigmoid(logits) * y
            merged = term if merged is None else merged + term
        part = jnp.dot(merged.astype(BF16), wo_ref[cols, :], preferred_element_type=F32)
        m = part if m is None else m + part
    y_ref[...] = x + _rms(m) * gpost_ref[...]


def _merge(layer, x2, gpre, gpost, oa, ob, oc, od, wg, wbr, wo):
    t = x2.shape[0]
    tm = TOK_TILE

    def tok(w):
        return pl.BlockSpec((tm, w), lambda i: (i, 0))

    return pl.pallas_call(
        _merge_kernel,
        grid=(t // tm,),
        in_specs=[tok(D_MODEL), _resident(gpre, layer), _resident(gpost, layer), tok(BRANCH_W), tok(BRANCH_W),
                  tok(BRANCH_W), tok(BRANCH_W), _resident(wg, layer, (N_HEADS * D_MODEL, 0)), _resident(wbr, layer),
                  _resident(wo, layer)],
        out_specs=tok(D_MODEL),
        out_shape=jax.ShapeDtypeStruct((t, D_MODEL), F32),
        compiler_params=_params("parallel"),
        name="merge",
    )(x2, gpre, gpost, oa, ob, oc, od, wg, wbr, wo)


def _ffn_kernel(x_ref, gpre_ref, gpost_ref, wgu_ref, wdn_ref, y_ref):
    sub_rows = x_ref.shape[0] // TOK_SUBS
    for sub in range(TOK_SUBS):
        rows = slice(sub_rows * sub, sub_rows * (sub + 1))
        _ffn_rows(x_ref.at[rows], gpre_ref, gpost_ref, wgu_ref, wdn_ref, y_ref.at[rows])


def _ffn_rows(x_ref, gpre_ref, gpost_ref, wgu_ref, wdn_ref, y_ref):
    x = x_ref[...]
    h = (_rms(x) * gpre_ref[...]).astype(BF16)
    acc = None
    for c in range(D_FF // FF_CHUNK):
        lo = FF_CHUNK * c
        gate = jnp.dot(h, wgu_ref[:, lo:lo + FF_CHUNK], preferred_element_type=F32)
        up = jnp.dot(h, wgu_ref[:, D_FF + lo:D_FF + lo + FF_CHUNK], preferred_element_type=F32)
        act = (jax.nn.silu(gate) * up).astype(BF16)
        part = jnp.dot(act, wdn_ref[lo:lo + FF_CHUNK, :], preferred_element_type=F32)
        acc = part if acc is None else acc + part
    y_ref[...] = x + _rms(acc) * gpost_ref[...]


def _ffn(layer, x2, gpre, gpost, wgu, wdn):
    t = x2.shape[0]
    tm = TOK_TILE
    tok = pl.BlockSpec((tm, D_MODEL), lambda i: (i, 0))
    return pl.pallas_call(
        _ffn_kernel,
        grid=(t // tm,),
        in_specs=[tok, _resident(gpre, layer), _resident(gpost, layer), _resident(wgu, layer),
                  _resident(wdn, layer)],
        out_specs=tok,
        out_shape=jax.ShapeDtypeStruct((t, D_MODEL), F32),
        compiler_params=_params("parallel"),
        name="ffn",
    )(x2, gpre, gpost, wgu, wdn)


def _rope_tables():
    f32 = np.float32
    inv_freq = (f32(1.0) / (f32(ROPE_THETA) ** (np.arange(ROPE_HALF, dtype=f32) * f32(1.0 / ROPE_HALF)))).astype(f32)

    def group(pos):
        ang = pos.astype(f32)[:, None] * inv_freq[None, :]
        c, s = np.cos(ang), np.sin(ang)
        return np.concatenate([c, c], axis=-1), np.concatenate([-s, s], axis=-1)

    gc, gs = group(np.arange(GRID_W))
    rc, rs = np.repeat(gc, GRID_W, axis=0), np.repeat(gs, GRID_W, axis=0)
    cc, cs = np.tile(gc, (SEQ // GRID_W, 1)), np.tile(gs, (SEQ // GRID_W, 1))
    pc, ps = group(np.arange(SEQ))
    ones = np.ones((SEQ, 2 * ROPE_HALF), f32)
    zeros = np.zeros((SEQ, 2 * ROPE_HALF), f32)
    ctab = np.stack([np.concatenate([rc, cc, rc, cc], axis=-1),
                     np.concatenate([rs, cs, rs, cs], axis=-1)])
    dtab = np.stack([np.concatenate([ones, ones, pc, ones], axis=-1),
                     np.concatenate([zeros, zeros, ps, zeros], axis=-1)])
    return jnp.asarray(ctab, F32), jnp.asarray(dtab, F32)


def _group_mean_matrix():
    i = np.arange(2 * LANES)
    m = (i[:, None] // HEAD_DIM) == (i[None, :] // HEAD_DIM)
    return jnp.asarray(m.astype(np.float32) / HEAD_DIM, dtype=BF16)


def _t5_bucket_index(rel):
    nb = T5_BUCKETS // 2
    max_exact = nb // 2
    ret = (rel > 0).astype(np.int32) * nb
    n = np.abs(rel)
    large = max_exact + (np.log(np.maximum(n, 1) / max_exact) / np.log(T5_MAX_DIST / max_exact)
                         * (nb - max_exact)).astype(np.int32)
    large = np.minimum(large, nb - 1)
    return ret + np.where(n < max_exact, n, large)


def _nbr_group_kinds():
    rows = SEQ // GRID_W
    ng = rows // A_QROWS
    kinds = []
    for g in (0, 1, ng - 1):
        ws = int(np.clip(A_QROWS * g - A_WIN_R // 2, 0, rows - A_KROWS))
        qr = A_QROWS * g + np.arange(A_QROWS)
        rs = np.clip(qr - A_WIN_R // 2, 0, rows - A_WIN_R)
        kr = ws + np.arange(A_KROWS)
        ok = (kr[None, :] >= rs[:, None]) & (kr[None, :] < rs[:, None] + A_WIN_R)
        kinds.append((ws - A_QROWS * g + A_WIN_R - 1, ok))
    return kinds


def _nbr_bias_kernel(r_ref, o_ref):
    lane = lax.broadcasted_iota(jnp.int32, (GRID_W, LANES), 1)
    qc = lax.broadcasted_iota(jnp.int32, (GRID_W, LANES), 0)
    kc = lane % GRID_W
    cs = jnp.clip(qc - A_WIN_C // 2, 0, GRID_W - A_WIN_C)
    col_ok = (kc >= cs) & (kc < cs + A_WIN_C)
    left = lane < GRID_W
    pairs = [jnp.where(col_ok, pltpu.roll(jnp.broadcast_to(r_ref[j:j + 1, :], (GRID_W, LANES)),
                                          LANES - (A_WIN_C - 1), 1, stride=1, stride_axis=0), NEG_INF)
             for j in range(2 * A_WIN_R)]
    masked = jnp.full((GRID_W, LANES), NEG_INF, F32)
    for t, (s, row_ok) in enumerate(_nbr_group_kinds()):
        for qr in range(A_QROWS):
            for p in range(A_KROWS // 2):
                j = 2 * p - qr + s + 1
                ok_a, ok_b = bool(row_ok[qr][2 * p]), bool(row_ok[qr][2 * p + 1])
                if not (ok_a or ok_b):
                    blk = masked
                elif ok_a and ok_b:
                    blk = pairs[j]
                else:
                    blk = jnp.where(left if ok_a else ~left, pairs[j], NEG_INF)
                o_ref[t, GRID_W * qr:GRID_W * (qr + 1), LANES * p:LANES * (p + 1)] = blk


def _nbr_bias(rpb):
    depth = rpb.shape[0]
    r = jnp.pad(rpb.astype(F32) * LOG2E, ((0, 0), (0, 0), (1, 1), (0, GRID_W - (2 * A_WIN_C - 1))))
    r = jnp.concatenate([r[:, :, :-1], r[:, :, 1:]], axis=-1)
    return pl.pallas_call(
        _nbr_bias_kernel,
        grid=(depth, N_HEADS),
        in_specs=[pl.BlockSpec((None, None, 2 * A_WIN_R, LANES), lambda l, h: (l, h, 0, 0))],
        out_specs=pl.BlockSpec((None, 3, None, A_TQ, A_TK), lambda l, h: (l, 0, h, 0, 0)),
        out_shape=jax.ShapeDtypeStruct((depth, 3, N_HEADS, A_TQ, A_TK), F32),
        compiler_params=_params("parallel", "parallel"),
        name="nbr_bias",
    )(r)


def _win_bias(t5_table):
    rel_values = np.arange(-B_WINDOW, B_WINDOW + 1)
    by_rel = t5_table.astype(F32)[_t5_bucket_index(rel_values)].T * LOG2E
    by_rel = jnp.pad(by_rel, ((0, 0), (LANES, LANES - 1)))[:, None, :]
    return pl.pallas_call(
        _win_bias_kernel,
        grid=(N_HEADS,),
        in_specs=[pl.BlockSpec((None, 1, 4 * LANES), lambda h: (h, 0, 0))],
        out_specs=pl.BlockSpec((3, None, B_TQ, B_TK), lambda h: (0, h, 0, 0)),
        out_shape=jax.ShapeDtypeStruct((3, N_HEADS, B_TQ, B_TK), F32),
        compiler_params=_params("parallel"),
        name="win_bias",
    )(by_rel)


def _win_bias_kernel(u_ref, o_ref):
    assert B_WINDOW == LANES
    nq = SEQ // B_TQ
    qr = lax.broadcasted_iota(jnp.int32, (LANES, LANES), 0)
    lane = lax.broadcasted_iota(jnp.int32, (LANES, LANES), 1)
    tiles = {}
    for d in (-LANES, 0, LANES):
        src = jnp.broadcast_to(u_ref[:, LANES + d:3 * LANES + d], (LANES, 2 * LANES))
        skew = pltpu.roll(src, 0, 1, stride=1, stride_axis=0)[:, LANES:]
        tiles[d] = jnp.where(jnp.abs(lane - qr + d) <= B_WINDOW, skew, NEG_INF)
    masked = jnp.full((LANES, LANES), NEG_INF, F32)
    for t, n in enumerate((0, 1, nq - 1)):
        start = int(np.clip(B_TQ * n - B_WINDOW, 0, SEQ - B_TK))
        for qb in range(B_TQ // LANES):
            for kb in range(B_TK // LANES):
                d = (start + LANES * kb) - (B_TQ * n + LANES * qb)
                o_ref[t, LANES * qb:LANES * (qb + 1), LANES * kb:LANES * (kb + 1)] = tiles.get(d, masked)


def _stacked_weights(w_in, c_q_norm, c_k_norm, d_w_q_up, d_w_kv_up):
    depth = w_in.shape[0]
    n_kr = W_MIX - LANES
    n_mix = n_kr + D_ROPE
    qs = HEAD_DIM ** -0.5 * LOG2E
    gate_w = w_in.shape[2] - n_mix
    zeros = jnp.zeros((depth, D_MODEL, W_MIX_COL * W_MIX - gate_w), BF16)
    w_kr = jnp.pad(w_in[:, :, n_kr:n_mix], ((0, 0), (0, 0), (D_NOPE, LANES - D_NOPE - D_ROPE)))
    w = jnp.concatenate([w_in[:, :, n_mix:].astype(BF16), zeros, w_in[:, :, :n_kr].astype(BF16),
                         w_kr.astype(BF16)], axis=2)
    cgain = jnp.concatenate([jnp.tile(c_q_norm * qs, (1, N_HEADS)), jnp.tile(c_k_norm, (1, KV_GROUPED))],
                            axis=1).astype(F32)[:, None, :]

    def pad_heads(a, dim):
        return jnp.pad(a, ((0, 0), (0, 0), (0, 0), (0, LANES - dim))).reshape(depth, a.shape[1], QK_W)

    dwq = pad_heads(d_w_q_up.reshape(depth, D_Q_LORA, N_HEADS, D_NOPE + D_ROPE), D_NOPE + D_ROPE)
    kvu = d_w_kv_up.reshape(depth, D_KV_LORA, N_HEADS, D_NOPE + D_V)
    dwkv = jnp.concatenate([pad_heads(kvu[..., :D_NOPE], D_NOPE), pad_heads(kvu[..., D_NOPE:], D_V)], axis=2)
    return w, cgain, dwq.astype(BF16), dwkv.astype(BF16)


def kernel(x, w_in, a_rpb, b_sink, t5_bias, c_q_norm, c_k_norm, d_q_norm, d_kv_norm, d_w_q_up, d_w_kv_up,
           w_branch, w_out, ln_pre_mix, ln_post_mix, ln_pre_ffn, ln_post_ffn, ffn_w_gu, ffn_w_down):
    b, s, d = x.shape
    assert (s, d) == (SEQ, D_MODEL)
    depth = w_in.shape[0]
    x2 = x.reshape(b * s, d)
    ctab, dtab = _rope_tables()
    gmat = _group_mean_matrix()
    wbias = _win_bias(t5_bias)
    abias = _nbr_bias(a_rpb)
    w, cgain, dwq, dwkv = _stacked_weights(w_in, c_q_norm, c_k_norm, d_w_q_up, d_w_kv_up)
    wbr, wo, wgu, wdn = (a.astype(BF16) for a in (w_branch, w_out, ffn_w_gu, ffn_w_down))
    g_pre_mix, g_post_mix, g_pre_ffn, g_post_ffn, dqn, dkvn = (
        v.astype(F32)[:, None, :] for v in (ln_pre_mix, ln_post_mix, ln_pre_ffn, ln_post_ffn, d_q_norm, d_kv_norm))
    sink = b_sink.astype(F32) * LOG2E
    for l in range(depth):
        (aq, ak, av, bq, bk, bv, cq, ck, cv, dq, dk, dv) = _proj(
            l, x2, g_pre_mix, w, gmat, cgain, ctab, dtab, dqn, dkvn, dwq, dwkv)
        oa, ob = _local_attn(l, sink[l], aq, ak, av, abias, bq, bk, bv, wbias)
        oc = _dense_attn(cq, ck, cv, KV_GROUPED, "dense_c")
        od = _dense_attn(dq, dk, dv, N_HEADS, "dense_d")
        x2 = _merge(l, x2, g_pre_mix, g_post_mix, oa, ob, oc, od, w, wbr, wo)
        x2 = _ffn(l, x2, g_pre_ffn, g_post_ffn, wgu, wdn)
    return x2.reshape(b, s, d)
```
